```python
import math
import jax
import jax.numpy as jnp
from jax import lax
import numpy as np

D_MODEL = 2048
BATCH = 8
SEQ = 4096
DEPTH = 2
DEC_BATCH = 2
DEC_SEQ = 8192
PAST_LEN = 128

HEAD_DIM = 64
BLOCK = 128
GRID_W = 64
PLE_DIM = 256
EPS = 1e-6
NEG_INF = -1e30

A_HEADS = 8
A_KV_HEADS = 2
A_WINDOW = 128
A_WIDTH = A_HEADS * HEAD_DIM
A_KV_WIDTH = A_KV_HEADS * HEAD_DIM

B_GROUPS = 4
B_GROUP_W = 128
B_WIDTH = B_GROUPS * B_GROUP_W
B_POOL_SIZES = (2, 4, 8, 16)

C_HEADS = 4
C_VDIM = 2 * HEAD_DIM
C_QK_WIDTH = C_HEADS * 2 * HEAD_DIM
C_WIDTH = C_HEADS * C_VDIM

D_HEADS = 8
D_KV_HEADS = 2
D_WIDTH = D_HEADS * HEAD_DIM
D_KV_WIDTH = D_KV_HEADS * HEAD_DIM
ROPE_THETA = 10000.0

REL_BUCKETS = 32
REL_MAX_DIST = 128
REL_HEADS = A_HEADS + C_HEADS

MIX_WIDTH = A_WIDTH + B_WIDTH + C_WIDTH + D_WIDTH
IN_SIZES = (A_WIDTH, A_KV_WIDTH, A_KV_WIDTH, A_WIDTH,
            B_WIDTH, B_WIDTH,
            C_QK_WIDTH, C_QK_WIDTH, C_WIDTH, C_WIDTH,
            D_WIDTH, D_KV_WIDTH, D_KV_WIDTH, D_WIDTH)
IN_WIDTH = 5632

kernel_name = 'hybrid_parallel_group_encoder'


def rms_norm(x, g):
    xf = x.astype(jnp.float32)
    y = xf * lax.rsqrt(jnp.mean(xf * xf, axis=-1, keepdims=True) + EPS)
    return y * g.astype(jnp.float32)


def split_columns(u):
    parts, start = [], 0
    for n in IN_SIZES:
        parts.append(u[..., start:start + n])
        start += n
    return parts


def rel_bucket(rel):
    nb = REL_BUCKETS // 2
    max_exact = nb // 2
    ret = jnp.where(rel > 0, nb, 0)
    n = jnp.abs(rel)
    nf = jnp.maximum(n, 1).astype(jnp.float32)
    large = max_exact + (jnp.log(nf / max_exact) / math.log(REL_MAX_DIST / max_exact)
                         * (nb - max_exact)).astype(jnp.int32)
    large = jnp.minimum(large, nb - 1)
    return ret + jnp.where(n < max_exact, n, large)


def window_attention(q, k, v, sink, rel_tab):
    bsz, s_len = q.shape[:2]
    nb = s_len // BLOCK
    grp = A_HEADS // A_KV_HEADS
    f32 = jnp.float32
    qb = jnp.moveaxis(q.astype(f32).reshape(bsz, nb, BLOCK, A_KV_HEADS, grp, HEAD_DIM), 1, 0)
    pad = ((0, 0), (BLOCK, BLOCK), (0, 0), (0, 0))
    kp = jnp.pad(k.astype(f32), pad)
    vp = jnp.pad(v.astype(f32), pad)
    rel = jnp.arange(3 * BLOCK)[None, :] - BLOCK - jnp.arange(BLOCK)[:, None]
    in_band = jnp.abs(rel) <= A_WINDOW
    bias = jnp.transpose(rel_tab.astype(f32)[rel_bucket(rel)], (2, 0, 1))
    bias = bias.reshape(A_KV_HEADS, grp, BLOCK, 3 * BLOCK)
    sink_col = jnp.broadcast_to(sink.astype(f32).reshape(1, A_KV_HEADS, grp, 1, 1),
                                (bsz, A_KV_HEADS, grp, BLOCK, 1))
    scale = HEAD_DIM ** -0.5

    def one_block(args):
        qi, b = args
        kw = lax.dynamic_slice_in_dim(kp, b * BLOCK, 3 * BLOCK, axis=1)
        vw = lax.dynamic_slice_in_dim(vp, b * BLOCK, 3 * BLOCK, axis=1)
        kpos = b * BLOCK - BLOCK + jnp.arange(3 * BLOCK)
        valid = in_band & ((kpos >= 0) & (kpos < s_len))[None, :]
        sc = jnp.einsum('bqkgd,bjkd->bkgqj', qi, kw) * scale + bias
        sc = jnp.where(valid, sc, NEG_INF)
        pr = jax.nn.softmax(jnp.concatenate([sc, sink_col], axis=-1), axis=-1)[..., :-1]
        return jnp.einsum('bkgqj,bjkd->bqkgd', pr, vw)

    o = lax.map(one_block, (qb, jnp.arange(nb)))
    return jnp.moveaxis(o, 0, 1).reshape(bsz, s_len, A_WIDTH)


def multi_scale_pool(x, w, scale):
    bsz, s_len, _ = x.shape
    xf = x.astype(jnp.float32)
    csum = jnp.concatenate([jnp.zeros((bsz, 1, B_WIDTH), jnp.float32), jnp.cumsum(xf, axis=1)], axis=1)
    t = jnp.arange(s_len)
    outs = []
    for gi, size in enumerate(B_POOL_SIZES):
        lo = size // 2
        hi = size - lo - 1
        start = jnp.maximum(t - lo, 0)
        end = jnp.minimum(t + hi, s_len - 1) + 1
        cg = csum[..., gi * B_GROUP_W:(gi + 1) * B_GROUP_W]
        win_sum = jnp.take(cg, end, axis=1) - jnp.take(cg, start, axis=1)
        outs.append(win_sum / (end - start).astype(jnp.float32)[None, :, None])
    pooled = (jnp.concatenate(outs, axis=-1) - xf).reshape(bsz, s_len, B_GROUPS, B_GROUP_W)
    y = jnp.einsum('bsgc,gce->bsge', pooled, w.astype(jnp.float32)).reshape(bsz, s_len, B_WIDTH)
    return y * scale.astype(jnp.float32)


def diff_attention(q, k, v, lam, rel_tab):
    bsz, s_len = q.shape[:2]
    nb = s_len // BLOCK
    f32 = jnp.float32
    qb = jnp.moveaxis(q.astype(f32).reshape(bsz, nb, BLOCK, C_HEADS, 2, HEAD_DIM), 1, 0)
    kf = k.astype(f32)
    vf = v.astype(f32)
    tab = rel_tab.astype(f32)
    kpos = jnp.arange(s_len)
    scale = HEAD_DIM ** -0.5

    def one_block(args):
        qi, b = args
        qpos = b * BLOCK + jnp.arange(BLOCK)
        bias = jnp.transpose(tab[rel_bucket(kpos[None, :] - qpos[:, None])], (2, 0, 1))[:, None]
        sc = jnp.einsum('bqhmd,bkhmd->bhmqk', qi, kf) * scale + bias
        pr = jax.nn.softmax(sc, axis=-1)
        attn = pr[:, :, 0] - lam * pr[:, :, 1]
        return jnp.einsum('bhqk,bkhe->bqhe', attn, vf)

    o = lax.map(one_block, (qb, jnp.arange(nb)))
    return jnp.moveaxis(o, 0, 1).reshape(bsz, s_len, C_HEADS, C_VDIM)


def axial_rope(s_len):
    rows = s_len // GRID_W
    t_row = jnp.repeat(jnp.arange(rows), GRID_W).astype(jnp.float32)
    t_col = jnp.tile(jnp.arange(GRID_W), rows).astype(jnp.float32)
    half = HEAD_DIM // 2
    inv = ROPE_THETA ** (-jnp.arange(0, half, 2, dtype=jnp.float32) / half)
    ang_r = t_row[:, None] * inv[None, :]
    ang_c = t_col[:, None] * inv[None, :]
    ang = jnp.concatenate([ang_r, ang_r, ang_c, ang_c], axis=-1)[:, None, :]
    return jnp.cos(ang), jnp.sin(ang)


def rotate_half(z):
    h = z.shape[-1] // 2
    return jnp.concatenate([-z[..., h:], z[..., :h]], axis=-1)


def apply_axial_rope(x, cos, sin):
    xr, xc = jnp.split(x, 2, axis=-1)
    xrot = jnp.concatenate([rotate_half(xr), rotate_half(xc)], axis=-1)
    return x * cos + xrot * sin


def grid_attention(q, k, v):
    bsz, s_len = q.shape[:2]
    nb = s_len // BLOCK
    grp = D_HEADS // D_KV_HEADS
    qb = jnp.moveaxis(q.reshape(bsz, nb, BLOCK, D_KV_HEADS, grp, HEAD_DIM), 1, 0)
    scale = HEAD_DIM ** -0.5

    def one_block(qi):
        sc = jnp.einsum('bqkgd,bjkd->bkgqj', qi, k) * scale
        pr = jax.nn.softmax(sc, axis=-1)
        return jnp.einsum('bkgqj,bjkd->bqkgd', pr, v)

    o = lax.map(one_block, qb)
    return jnp.moveaxis(o, 0, 1).reshape(bsz, s_len, D_WIDTH)


def encoder_layer(x, pe, rel_bias, layer_idx, w_in, w_o, g_pre, g_post, sink_a, pool_w, pool_scale,
                  lam_q1, lam_k1, lam_q2, lam_k2, diff_subln, qnorm_d, knorm_d, w_pe, w_pg):
    bsz, s_len, _ = x.shape
    dt = x.dtype
    f32 = jnp.float32
    h = rms_norm(x, g_pre).astype(dt)
    u = h @ w_in
    (aq, ak, av, ag, bx, bg, cq, ck, cv, cg, dq, dk, dv, dg) = split_columns(u)

    def heads(t, n):
        return t.reshape(bsz, s_len, n, -1)

    ya = window_attention(heads(aq, A_HEADS), heads(ak, A_KV_HEADS), heads(av, A_KV_HEADS),
                          sink_a, rel_bias[:, :A_HEADS])
    yb = multi_scale_pool(bx, pool_w, pool_scale)
    lambda_init = 0.8 - 0.6 * math.exp(-0.3 * layer_idx)
    lam = (jnp.exp(jnp.sum(lam_q1.astype(f32) * lam_k1.astype(f32)))
           - jnp.exp(jnp.sum(lam_q2.astype(f32) * lam_k2.astype(f32))) + lambda_init)
    yc = diff_attention(cq.reshape(bsz, s_len, C_HEADS, 2, HEAD_DIM),
                        ck.reshape(bsz, s_len, C_HEADS, 2, HEAD_DIM),
                        heads(cv, C_HEADS), lam, rel_bias[:, A_HEADS:])
    yc = (rms_norm(yc, diff_subln) * (1.0 - lambda_init)).reshape(bsz, s_len, C_WIDTH)
    cos, sin = axial_rope(s_len)
    qd = apply_axial_rope(rms_norm(heads(dq, D_HEADS), qnorm_d), cos, sin)
    kd = apply_axial_rope(rms_norm(heads(dk, D_KV_HEADS), knorm_d), cos, sin)
    yd = grid_attention(qd, kd, heads(dv, D_KV_HEADS).astype(f32))

    mixed = jnp.concatenate([ya.astype(dt) * jax.nn.silu(ag),
                             yb.astype(dt) * jax.nn.silu(bg),
                             yc.astype(dt) * jax.nn.silu(cg),
                             yd.astype(dt) * jax.nn.silu(dg)], axis=-1)
    x = x + rms_norm(mixed @ w_o, g_post).astype(dt)
    gate = jax.nn.sigmoid(x @ w_pg)
    return x + (pe @ w_pe) * gate


def run_trunk(x, p, rel_bias, w_in, w_o, g_pre, g_post, sink_a, pool_w, pool_scale,
              lam_q1, lam_k1, lam_q2, lam_k2, diff_subln, qnorm_d, knorm_d, w_pe, w_pg):
    for i in range(DEPTH):
        x = encoder_layer(x, p[i], rel_bias, i, w_in[i], w_o[i], g_pre[i], g_post[i], sink_a[i],
                          pool_w[i], pool_scale[i], lam_q1[i], lam_k1[i], lam_q2[i], lam_k2[i],
                          diff_subln[i], qnorm_d[i], knorm_d[i], w_pe[i], w_pg[i])
    return x


def setup_inputs(seed: int = 0) -> dict:
    key = jax.random.key(seed)
    ks = jax.random.split(key, 24)
    f32 = jnp.float32

    def nrm(k, shape, s):
        return jax.random.normal(k, shape, f32) * s

    def gain(k, shape):
        return 1.0 + 0.1 * jax.random.normal(k, shape, f32)

    return {
        'x_prompt': nrm(ks[0], (BATCH, SEQ, D_MODEL), 1.0),
        'x_sample': nrm(ks[1], (DEC_BATCH, DEC_SEQ, D_MODEL), 1.0),
        'p_prompt': nrm(ks[2], (DEPTH, BATCH, SEQ, PLE_DIM), 1.0),
        'p_sample': nrm(ks[3], (DEPTH, DEC_BATCH, DEC_SEQ, PLE_DIM), 1.0),
        'w_in': nrm(ks[4], (DEPTH, D_MODEL, IN_WIDTH), D_MODEL ** -0.5),
        'w_o': nrm(ks[5], (DEPTH, MIX_WIDTH, D_MODEL), MIX_WIDTH ** -0.5),
        'g_pre': gain(ks[6], (DEPTH, D_MODEL)),
        'g_post': gain(ks[7], (DEPTH, D_MODEL)),
        'sink_a': nrm(ks[8], (DEPTH, A_HEADS), 0.5),
        'pool_w': nrm(ks[9], (DEPTH, B_GROUPS, B_GROUP_W, B_GROUP_W), B_GROUP_W ** -0.5),
        'pool_scale': gain(ks[10], (DEPTH, B_WIDTH)),
        'lam_q1': nrm(ks[11], (DEPTH, HEAD_DIM), 0.1),
        'lam_k1': nrm(ks[12], (DEPTH, HEAD_DIM), 0.1),
        'lam_q2': nrm(ks[13], (DEPTH, HEAD_DIM), 0.1),
        'lam_k2': nrm(ks[14], (DEPTH, HEAD_DIM), 0.1),
        'diff_subln': gain(ks[15], (DEPTH, C_VDIM)),
        'qnorm_d': gain(ks[16], (DEPTH, HEAD_DIM)),
        'knorm_d': gain(ks[17], (DEPTH, HEAD_DIM)),
        'rel_bias': nrm(ks[18], (REL_BUCKETS, REL_HEADS), 0.5),
        'w_pe': nrm(ks[19], (DEPTH, PLE_DIM, D_MODEL), PLE_DIM ** -0.5),
        'w_pg': nrm(ks[20], (DEPTH, D_MODEL, D_MODEL), D_MODEL ** -0.5),
    }


def reference(x_prompt, x_sample, p_prompt, p_sample, w_in, w_o, g_pre, g_post, sink_a, pool_w,
              pool_scale, lam_q1, lam_k1, lam_q2, lam_k2, diff_subln, qnorm_d, knorm_d, rel_bias,
              w_pe, w_pg):
    y_prompt = run_trunk(x_prompt, p_prompt, rel_bias, w_in, w_o, g_pre, g_post, sink_a, pool_w,
                         pool_scale, lam_q1, lam_k1, lam_q2, lam_k2, diff_subln, qnorm_d, knorm_d,
                         w_pe, w_pg)
    y_sample = run_trunk(x_sample, p_sample, rel_bias, w_in, w_o, g_pre, g_post, sink_a, pool_w,
                         pool_scale, lam_q1, lam_k1, lam_q2, lam_k2, diff_subln, qnorm_d, knorm_d,
                         w_pe, w_pg)
    return (y_prompt, y_sample)
```

```python
import functools
import math

import jax
import jax.numpy as jnp
from jax import lax
from jax.experimental import pallas as pl
from jax.experimental.pallas import tpu as pltpu

F32 = jnp.float32
BF16 = jnp.bfloat16

D_MODEL = 2048
HEAD_DIM = 64
LANES = 128
PLE_DIM = 256
EPS = 1e-6
NEG_INF = -1e30
SCALE = HEAD_DIM ** -0.5

A_HEADS = 8
A_WINDOW = 128
A_BLOCK = 128
B_GROUPS = 4
B_POOL_SIZES = (2, 4, 8, 16)
B_HALO = 64
C_HEADS = 4
REL_BUCKETS = 32
REL_MAX_DIST = 128
ROPE_THETA = 10000.0
GRID_W = 64

IN_SIZES = (512, 128, 128, 512, 512, 512, 512, 512, 512, 512, 512, 128, 128, 512)
MIX_WIDTH = 2048

U_WIDTH = 6144
COL_AQ, COL_AG, COL_BX, COL_BG = 0, 4, 8, 12
COL_CQ, COL_CK, COL_CV, COL_CG = 16, 20, 24, 28
COL_DQ, COL_DG = 32, 36
COL_AK, COL_AV, COL_DK, COL_DV = 40, 42, 44, 46

VMEM_LIMIT = 56 * 1024 * 1024


def _params(sem, vmem=VMEM_LIMIT):
    return pltpu.CompilerParams(dimension_semantics=sem, vmem_limit_bytes=vmem)


def _dot(a, b):
    return jnp.dot(a, b, preferred_element_type=F32)


def _dot_nt(a, b):
    return lax.dot_general(a, b, (((1,), (1,)), ((), ())), preferred_element_type=F32)


def _silu(x):
    return x / (1.0 + jnp.exp(-x))


def _lane_is_low(width=LANES):
    return (lax.broadcasted_iota(jnp.int32, (1, width), 1) % LANES) < HEAD_DIM


def _in_proj_kernel(x_ref, g_ref, w_ref, o_ref, h_ref):
    @pl.when(pl.program_id(1) == 0)
    def _():
        x = x_ref[...]
        ms = jnp.mean(x * x, axis=-1, keepdims=True)
        h_ref[...] = (x * lax.rsqrt(ms + EPS) * g_ref[...]).astype(BF16)

    o_ref[...] = _dot(h_ref[...], w_ref[...]).astype(o_ref.dtype)


def _in_proj(x2, g_pre, w_in_p, tm, tn):
    m = x2.shape[0]
    return pl.pallas_call(
        _in_proj_kernel,
        grid=(m // tm, U_WIDTH // tn),
        in_specs=[
            pl.BlockSpec((tm, D_MODEL), lambda i, j: (i, 0)),
            pl.BlockSpec((1, D_MODEL), lambda i, j: (0, 0)),
            pl.BlockSpec((D_MODEL, tn), lambda i, j: (0, j)),
        ],
        out_specs=pl.BlockSpec((tm, tn), lambda i, j: (i, j)),
        out_shape=jax.ShapeDtypeStruct((m, U_WIDTH), BF16),
        scratch_shapes=[pltpu.VMEM((tm, D_MODEL), BF16)],
        compiler_params=_params(("parallel", "arbitrary")),
        name="in_proj",
    )(x2, g_pre.reshape(1, D_MODEL), w_in_p)


def _win_attn_kernel(q_ref, kp_ref, kc_ref, kn_ref, vp_ref, vc_ref, vn_ref, g_ref, bias_ref,
                     sink_ref, o_ref, *, s_len):
    b = pl.program_id(1)
    low = _lane_is_low()
    row = lax.broadcasted_iota(jnp.int32, (A_BLOCK, 3 * A_BLOCK), 0)
    col = lax.broadcasted_iota(jnp.int32, (A_BLOCK, 3 * A_BLOCK), 1)
    rel = col - A_BLOCK - row
    kpos = b * A_BLOCK - A_BLOCK + col
    valid = (jnp.abs(rel) <= A_WINDOW) & (kpos >= 0) & (kpos < s_len)
    for pair in range(A_HEADS // 2):
        kvh = pair // 2
        lanes = slice(kvh * LANES, (kvh + 1) * LANES)
        kk = jnp.concatenate([kp_ref[0, :, lanes], kc_ref[0, :, lanes], kn_ref[0, :, lanes]], axis=0)
        vv = jnp.concatenate([vp_ref[0, :, lanes], vc_ref[0, :, lanes], vn_ref[0, :, lanes]], axis=0)
        q = q_ref[0, :, pair * LANES:(pair + 1) * LANES]
        outs = []
        for half in range(2):
            h = 2 * pair + half
            qm = jnp.where(low if half == 0 else ~low, q, jnp.zeros_like(q))
            sc = _dot_nt(qm, kk) + bias_ref[h]
            sc = jnp.where(valid, sc, NEG_INF)
            sink = sink_ref[h:h + 1, 0:1]
            m = jnp.maximum(jnp.max(sc, axis=-1, keepdims=True), sink)
            p = jnp.exp(sc - m)
            denom = jnp.sum(p, axis=-1, keepdims=True) + jnp.exp(sink - m)
            outs.append(_dot(p.astype(BF16), vv) / denom)
        y = jnp.where(low, outs[0], outs[1])
        gate = g_ref[0, :, pair * LANES:(pair + 1) * LANES].astype(F32)
        o_ref[0, :, pair * LANES:(pair + 1) * LANES] = (y * _silu(gate)).astype(o_ref.dtype)


def _win_attn(u3, bias_a, sink_b):
    bsz, s_len, _ = u3.shape
    nb = s_len // A_BLOCK
    kv = (1, A_BLOCK, 2 * LANES)
    prev = lambda c: (lambda bi, b: (bi, jnp.maximum(b - 1, 0), c))
    cur = lambda c: (lambda bi, b: (bi, b, c))
    nxt = lambda c: (lambda bi, b: (bi, jnp.minimum(b + 1, nb - 1), c))
    wide = (1, A_BLOCK, 4 * LANES)
    return pl.pallas_call(
        functools.partial(_win_attn_kernel, s_len=s_len),
        grid=(bsz, nb),
        in_specs=[
            pl.BlockSpec(wide, cur(COL_AQ // 4)),
            pl.BlockSpec(kv, prev(COL_AK // 2)),
            pl.BlockSpec(kv, cur(COL_AK // 2)),
            pl.BlockSpec(kv, nxt(COL_AK // 2)),
            pl.BlockSpec(kv, prev(COL_AV // 2)),
            pl.BlockSpec(kv, cur(COL_AV // 2)),
            pl.BlockSpec(kv, nxt(COL_AV // 2)),
            pl.BlockSpec(wide, cur(COL_AG // 4)),
            pl.BlockSpec((A_HEADS, A_BLOCK, 3 * A_BLOCK), lambda bi, b: (0, 0, 0)),
            pl.BlockSpec((A_HEADS, LANES), lambda bi, b: (0, 0)),
        ],
        out_specs=pl.BlockSpec(wide, lambda bi, b: (bi, b, 0)),
        out_shape=jax.ShapeDtypeStruct((bsz, s_len, 4 * LANES), BF16),
        compiler_params=_params(("parallel", "parallel")),
        name="win_attn",
    )(u3, u3, u3, u3, u3, u3, u3, u3, bias_a, sink_b)


def _pool_kernel(xp_ref, xc_ref, xn_ref, g_ref, band_ref, w_ref, sc_ref, o_ref, *, s_len, tq):
    i = pl.program_id(1)
    n = pl.num_programs(1)
    xp = xp_ref[0]
    xn = xn_ref[0]
    xp = jnp.where(i > 0, xp, jnp.zeros_like(xp))
    xn = jnp.where(i < n - 1, xn, jnp.zeros_like(xn))
    xc = xc_ref[0]
    xext = jnp.concatenate([xp, xc, xn], axis=0)
    t = i * tq + lax.broadcasted_iota(jnp.int32, (tq, 1), 0)
    for gi, size in enumerate(B_POOL_SIZES):
        lo = size // 2
        hi = size - lo - 1
        lanes = slice(gi * LANES, (gi + 1) * LANES)
        win_sum = _dot(band_ref[gi], xext[:, lanes])
        start = jnp.maximum(t - lo, 0)
        end = jnp.minimum(t + hi, s_len - 1) + 1
        pooled = win_sum / (end - start).astype(F32) - xc[:, lanes].astype(F32)
        y = _dot(pooled.astype(BF16), w_ref[gi]) * sc_ref[:, lanes]
        gate = g_ref[0, :, lanes].astype(F32)
        o_ref[0, :, lanes] = (y * _silu(gate)).astype(o_ref.dtype)


def _pool_bands(tq):
    r = jnp.arange(tq)[:, None]
    c = jnp.arange(tq + 2 * B_HALO)[None, :] - B_HALO
    bands = []
    for size in B_POOL_SIZES:
        lo = size // 2
        hi = size - lo - 1
        bands.append(((c - r >= -lo) & (c - r <= hi)).astype(BF16))
    return jnp.stack(bands)


def _pool(u3, pool_w, pool_scale, tq):
    bsz, s_len, _ = u3.shape
    nt = s_len // tq
    per = tq // B_HALO
    nh = s_len // B_HALO
    wide = (1, tq, 4 * LANES)
    halo = (1, B_HALO, 4 * LANES)
    return pl.pallas_call(
        functools.partial(_pool_kernel, s_len=s_len, tq=tq),
        grid=(bsz, nt),
        in_specs=[
            pl.BlockSpec(halo, lambda bi, i: (bi, jnp.maximum(i * per - 1, 0), COL_BX // 4)),
            pl.BlockSpec(wide, lambda bi, i: (bi, i, COL_BX // 4)),
            pl.BlockSpec(halo, lambda bi, i: (bi, jnp.minimum((i + 1) * per, nh - 1), COL_BX // 4)),
            pl.BlockSpec(wide, lambda bi, i: (bi, i, COL_BG // 4)),
            pl.BlockSpec((B_GROUPS, tq, tq + 2 * B_HALO), lambda bi, i: (0, 0, 0)),
            pl.BlockSpec((B_GROUPS, LANES, LANES), lambda bi, i: (0, 0, 0)),
            pl.BlockSpec((1, 4 * LANES), lambda bi, i: (0, 0)),
        ],
        out_specs=pl.BlockSpec(wide, lambda bi, i: (bi, i, 0)),
        out_shape=jax.ShapeDtypeStruct((bsz, s_len, 4 * LANES), BF16),
        compiler_params=_params(("parallel", "parallel")),
        name="pool",
    )(u3, u3, u3, u3, _pool_bands(tq), pool_w.astype(BF16), pool_scale.reshape(1, 4 * LANES))


def _online_update(sc, v, m, l, acc):
    m_new = jnp.maximum(m, jnp.max(sc, axis=-1, keepdims=True))
    alpha = jnp.exp(m - m_new)
    p = jnp.exp(sc - m_new)
    l_new = alpha * l + jnp.sum(p, axis=-1, keepdims=True)
    acc_new = alpha * acc + _dot(p.astype(BF16), v)
    return m_new, l_new, acc_new


def _diff_attn_kernel(q_ref, k_ref, v_ref, g_ref, near_ref, far_ref, lam_ref, sub_ref, o_ref,
                      *, t, n, lambda_init):
    i = pl.program_id(2)
    low = _lane_is_low()
    q = q_ref[0]
    zero = jnp.zeros_like(q)
    q1 = jnp.where(low, q, zero)
    q2 = jnp.where(low, zero, q)

    def step(j, carry, bias):
        rows = pl.ds(pl.multiple_of(j * t, t), t)
        k = k_ref[0, rows, :]
        v = v_ref[0, rows, :]
        c1, c2 = carry
        c1 = _online_update(_dot_nt(q1, k) + bias, v, *c1)
        c2 = _online_update(_dot_nt(q2, k) + bias, v, *c2)
        return c1, c2

    def init():
        return (jnp.full((t, 1), NEG_INF, F32), jnp.zeros((t, 1), F32), jnp.zeros((t, LANES), F32))

    carry = (init(), init())
    near_lo = jnp.maximum(i - 1, 0)
    near_hi = jnp.minimum(i + 2, n)
    carry = lax.fori_loop(0, near_lo, lambda j, c: step(j, c, far_ref[0, 0:1, 0:1]), carry)
    carry = lax.fori_loop(near_lo, near_hi, lambda j, c: step(j, c, near_ref[0, j - i + 1]), carry)
    carry = lax.fori_loop(near_hi, n, lambda j, c: step(j, c, far_ref[0, 1:2, 0:1]), carry)
    (_, l1, a1), (_, l2, a2) = carry

    dot1 = jnp.sum(lam_ref[0:1, :] * lam_ref[1:2, :], axis=-1, keepdims=True)
    dot2 = jnp.sum(lam_ref[2:3, :] * lam_ref[3:4, :], axis=-1, keepdims=True)
    lam = jnp.exp(dot1) - jnp.exp(dot2) + lambda_init
    y = a1 / l1 - lam * (a2 / l2)
    ms = jnp.mean(y * y, axis=-1, keepdims=True)
    y = y * lax.rsqrt(ms + EPS) * sub_ref[...] * (1.0 - lambda_init)
    o_ref[0] = (y * _silu(g_ref[0].astype(F32))).astype(o_ref.dtype)


def _diff_attn(u3, near_c, far_c, lam_vecs, subln, lambda_init, t):
    bsz, s_len, _ = u3.shape
    n = s_len // t
    return pl.pallas_call(
        functools.partial(_diff_attn_kernel, t=t, n=n, lambda_init=lambda_init),
        grid=(bsz, C_HEADS, n),
        in_specs=[
            pl.BlockSpec((1, t, LANES), lambda bi, h, i: (bi, i, COL_CQ + h)),
            pl.BlockSpec((1, s_len, LANES), lambda bi, h, i: (bi, 0, COL_CK + h)),
            pl.BlockSpec((1, s_len, LANES), lambda bi, h, i: (bi, 0, COL_CV + h)),
            pl.BlockSpec((1, t, LANES), lambda bi, h, i: (bi, i, COL_CG + h)),
            pl.BlockSpec((1, 3, t, t), lambda bi, h, i: (h, 0, 0, 0)),
            pl.BlockSpec((1, 2, LANES), lambda bi, h, i: (h, 0, 0)),
            pl.BlockSpec((4, HEAD_DIM), lambda bi, h, i: (0, 0)),
            pl.BlockSpec((1, LANES), lambda bi, h, i: (0, 0)),
        ],
        out_specs=pl.BlockSpec((1, t, LANES), lambda bi, h, i: (bi, i, h)),
        out_shape=jax.ShapeDtypeStruct((bsz, s_len, 4 * LANES), BF16),
        compiler_params=_params(("parallel", "parallel", "parallel")),
        name="diff_attn",
    )(u3, u3, u3, u3, near_c, far_c, lam_vecs, subln.reshape(1, LANES))


def _norm_rope(x, gain, cos, sin, seg):
    w = x.shape[-1]
    x2 = x * x
    hi = x2.astype(BF16)
    lo = (x2 - hi.astype(F32)).astype(BF16)
    ss = _dot(hi, seg) + _dot(lo, seg)
    y = x * lax.rsqrt(ss * (1.0 / HEAD_DIM) + EPS) * gain
    quarter = HEAD_DIM // 4
    ahead = pltpu.roll(y, w - quarter, 1)
    behind = pltpu.roll(y, quarter, 1)
    first = (lax.broadcasted_iota(jnp.int32, (1, w), 1) % (2 * quarter)) < quarter
    rot = jnp.where(first, -ahead, behind)
    reps = w // LANES
    cos_w = jnp.concatenate([cos] * reps, axis=1)
    sin_w = jnp.concatenate([sin] * reps, axis=1)
    return y * cos_w + rot * sin_w


def _qk_prep_kernel(q_ref, k_ref, cos_ref, sin_ref, gq_ref, gk_ref, seg_ref, qo_ref, ko_ref):
    cos = cos_ref[...]
    sin = sin_ref[...]
    seg = seg_ref[...]
    q = _norm_rope(q_ref[...].astype(F32), gq_ref[...], cos, sin, seg)
    qo_ref[...] = (q * SCALE).astype(qo_ref.dtype)
    k = _norm_rope(k_ref[...].astype(F32), gk_ref[...], cos, sin, seg[:2 * LANES, :2 * LANES])
    ko_ref[...] = k.astype(ko_ref.dtype)


def _qk_prep(u2, cos, sin, qnorm, knorm, s_len, tm):
    m = u2.shape[0]
    per_seq = s_len // tm
    lane = jnp.arange(4 * LANES)
    seg = (lane[:, None] // HEAD_DIM == lane[None, :] // HEAD_DIM).astype(BF16)
    return pl.pallas_call(
        _qk_prep_kernel,
        grid=(m // tm,),
        in_specs=[
            pl.BlockSpec((tm, 4 * LANES), lambda i: (i, COL_DQ // 4)),
            pl.BlockSpec((tm, 2 * LANES), lambda i: (i, COL_DK // 2)),
            pl.BlockSpec((tm, LANES), lambda i: (i % per_seq, 0)),
            pl.BlockSpec((tm, LANES), lambda i: (i % per_seq, 0)),
            pl.BlockSpec((1, 4 * LANES), lambda i: (0, 0)),
            pl.BlockSpec((1, 2 * LANES), lambda i: (0, 0)),
            pl.BlockSpec((4 * LANES, 4 * LANES), lambda i: (0, 0)),
        ],
        out_specs=[
            pl.BlockSpec((tm, 4 * LANES), lambda i: (i, 0)),
            pl.BlockSpec((tm, 2 * LANES), lambda i: (i, 0)),
        ],
        out_shape=[jax.ShapeDtypeStruct((m, 4 * LANES), BF16),
                   jax.ShapeDtypeStruct((m, 2 * LANES), BF16)],
        compiler_params=_params(("parallel",)),
        name="qk_prep",
    )(u2, u2, cos, sin, jnp.tile(qnorm, 8).reshape(1, 4 * LANES),
      jnp.tile(knorm, 4).reshape(1, 2 * LANES), seg)


def _grid_attn_kernel(q_ref, k_ref, v_ref, g_ref, o_ref, *, tq, tk, nk):
    low = _lane_is_low()
    qs = []
    for pair in range(2):
        q = q_ref[0, :, pair * LANES:(pair + 1) * LANES]
        zero = jnp.zeros_like(q)
        qs.append(jnp.where(low, q, zero))
        qs.append(jnp.where(low, zero, q))

    def step(j, carry):
        rows = pl.ds(pl.multiple_of(j * tk, tk), tk)
        k = k_ref[0, rows, :]
        v = v_ref[0, rows, :]
        return tuple(_online_update(_dot_nt(qh, k), v, *c) for qh, c in zip(qs, carry))

    init = tuple((jnp.full((tq, 1), NEG_INF, F32), jnp.zeros((tq, 1), F32),
                  jnp.zeros((tq, LANES), F32)) for _ in range(4))
    carry = lax.fori_loop(0, nk, step, init)
    for pair in range(2):
        (_, l0, a0), (_, l1, a1) = carry[2 * pair], carry[2 * pair + 1]
        y = jnp.where(low, a0 / l0, a1 / l1)
        gate = g_ref[0, :, pair * LANES:(pair + 1) * LANES].astype(F32)
        o_ref[0, :, pair * LANES:(pair + 1) * LANES] = (y * _silu(gate)).astype(o_ref.dtype)


def _grid_attn(qd3, kd3, u3, tq, tk):
    bsz, s_len, _ = qd3.shape
    nk = s_len // tk
    return pl.pallas_call(
        functools.partial(_grid_attn_kernel, tq=tq, tk=tk, nk=nk),
        grid=(bsz, 2, s_len // tq),
        in_specs=[
            pl.BlockSpec((1, tq, 2 * LANES), lambda bi, h, i: (bi, i, h)),
            pl.BlockSpec((1, s_len, LANES), lambda bi, h, i: (bi, 0, h)),
            pl.BlockSpec((1, s_len, LANES), lambda bi, h, i: (bi, 0, COL_DV + h)),
            pl.BlockSpec((1, tq, 2 * LANES), lambda bi, h, i: (bi, i, COL_DG // 2 + h)),
        ],
        out_specs=pl.BlockSpec((1, tq, 2 * LANES), lambda bi, h, i: (bi, i, h)),
        out_shape=jax.ShapeDtypeStruct((bsz, s_len, 4 * LANES), BF16),
        compiler_params=_params(("parallel", "parallel", "parallel")),
        name="grid_attn",
    )(qd3, kd3, u3, u3)


def _out_kernel(x_ref, ya_ref, yb_ref, yc_ref, yd_ref, pe_ref, wo_ref, gp_ref, wpe_ref, wpg_ref, o_ref):
    width = 4 * LANES
    z = _dot(ya_ref[...], wo_ref[0 * width:1 * width, :])
    z += _dot(yb_ref[...], wo_ref[1 * width:2 * width, :])
    z += _dot(yc_ref[...], wo_ref[2 * width:3 * width, :])
    z += _dot(yd_ref[...], wo_ref[3 * width:4 * width, :])
    ms = jnp.mean(z * z, axis=-1, keepdims=True)
    x = x_ref[...] + z * lax.rsqrt(ms + EPS) * gp_ref[...]
    logits = _dot(x.astype(BF16), wpg_ref[...])
    gate = 1.0 / (1.0 + jnp.exp(-logits))
    pe = _dot(pe_ref[...].astype(BF16), wpe_ref[...])
    o_ref[...] = x + pe * gate


def _out_stage(x2, ya, yb, yc, yd, pe2, w_o, g_post, w_pe, w_pg, tm):
    m = x2.shape[0]
    row = lambda w: pl.BlockSpec((tm, w), lambda i: (i, 0))
    const = lambda r, c: pl.BlockSpec((r, c), lambda i: (0, 0), pipeline_mode=pl.Buffered(1))
    return pl.pallas_call(
        _out_kernel,
        grid=(m // tm,),
        in_specs=[row(D_MODEL), row(4 * LANES), row(4 * LANES), row(4 * LANES), row(4 * LANES),
                  row(PLE_DIM), const(MIX_WIDTH, D_MODEL), const(1, D_MODEL),
                  const(PLE_DIM, D_MODEL), const(D_MODEL, D_MODEL)],
        out_specs=row(D_MODEL),
        out_shape=jax.ShapeDtypeStruct((m, D_MODEL), F32),
        compiler_params=_params(("parallel",)),
        name="out_stage",
    )(x2, ya, yb, yc, yd, pe2, w_o, g_post.reshape(1, D_MODEL), w_pe, w_pg)


def _prep_w_in(w):
    parts, start = [], 0
    for n in IN_SIZES:
        parts.append(w[:, start:start + n])
        start += n
    aq, ak, av, ag, bx, bg, cq, ck, cv, cg, dq, dk, dv, dg = parts

    def dup(t):
        return jnp.concatenate([t[:, :HEAD_DIM], t[:, :HEAD_DIM], t[:, HEAD_DIM:], t[:, HEAD_DIM:]], axis=1)

    cols = [aq * SCALE, ag, bx, bg, cq * SCALE, ck, cv, cg, dq, dg, dup(ak), dup(av), dup(dk), dup(dv)]
    return jnp.concatenate(cols, axis=1).astype(BF16)


def _rel_bucket(rel):
    nb = REL_BUCKETS // 2
    max_exact = nb // 2
    ret = jnp.where(rel > 0, nb, 0)
    n = jnp.abs(rel)
    nf = jnp.maximum(n, 1).astype(F32)
    large = max_exact + (jnp.log(nf / max_exact) / math.log(REL_MAX_DIST / max_exact)
                         * (nb - max_exact)).astype(jnp.int32)
    large = jnp.minimum(large, nb - 1)
    return ret + jnp.where(n < max_exact, n, large)


def _bias_tables(rel_bias, t):
    tab = rel_bias.astype(F32)
    rel_a = jnp.arange(3 * A_BLOCK)[None, :] - A_BLOCK - jnp.arange(A_BLOCK)[:, None]
    bias_a = jnp.transpose(tab[:, :A_HEADS][_rel_bucket(rel_a)], (2, 0, 1))
    tab_c = tab[:, A_HEADS:]
    r = jnp.arange(t)[:, None]
    c = jnp.arange(t)[None, :]
    near = jnp.stack([tab_c[_rel_bucket(c + d * t - r)] for d in (-1, 0, 1)])
    near_c = jnp.transpose(near, (3, 0, 1, 2))
    far = tab_c[_rel_bucket(jnp.array([-REL_MAX_DIST, REL_MAX_DIST]))]
    far_c = jnp.broadcast_to(jnp.transpose(far)[:, :, None], (C_HEADS, 2, LANES))
    return bias_a, near_c, far_c


def _axial_rope(s_len):
    rows = s_len // GRID_W
    t_row = jnp.repeat(jnp.arange(rows), GRID_W).astype(F32)
    t_col = jnp.tile(jnp.arange(GRID_W), rows).astype(F32)
    half = HEAD_DIM // 2
    inv = ROPE_THETA ** (-jnp.arange(0, half, 2, dtype=F32) / half)
    ang_r = t_row[:, None] * inv[None, :]
    ang_c = t_col[:, None] * inv[None, :]
    ang = jnp.concatenate([ang_r, ang_r, ang_c, ang_c] * 2, axis=-1)
    return jnp.cos(ang), jnp.sin(ang)


def _tiles(s_len):
    return dict(tm_in=min(1024, s_len), tn_in=512, tq_pool=min(256, s_len), t_diff=min(512, s_len),
                tm_prep=min(512, s_len), tq_grid=min(256, s_len), tk_grid=min(512, s_len),
                tm_out=min(256, s_len))


def _layer(x, pe, layer_idx, tables, rope, w, tiles):
    bsz, s_len, _ = x.shape
    m = bsz * s_len
    bias_a, near_c, far_c = tables
    cos, sin = rope
    x2 = x.reshape(m, D_MODEL)
    u2 = _in_proj(x2, w["g_pre"], w["w_in"], tiles["tm_in"], tiles["tn_in"])
    u3 = u2.reshape(bsz, s_len, U_WIDTH)
    ya = _win_attn(u3, bias_a, w["sink"])
    yb = _pool(u3, w["pool_w"], w["pool_scale"], tiles["tq_pool"])
    lambda_init = 0.8 - 0.6 * math.exp(-0.3 * layer_idx)
    yc = _diff_attn(u3, near_c, far_c, w["lam"], w["diff_subln"], lambda_init, tiles["t_diff"])
    qd, kd = _qk_prep(u2, cos, sin, w["qnorm_d"], w["knorm_d"], s_len, tiles["tm_prep"])
    yd = _grid_attn(qd.reshape(bsz, s_len, -1), kd.reshape(bsz, s_len, -1), u3,
                    tiles["tq_grid"], tiles["tk_grid"])
    flat = lambda y: y.reshape(m, 4 * LANES)
    out = _out_stage(x2, flat(ya), flat(yb), flat(yc), flat(yd), pe.reshape(m, PLE_DIM),
                     w["w_o"], w["g_post"], w["w_pe"], w["w_pg"], tiles["tm_out"])
    return out.reshape(bsz, s_len, D_MODEL)


def _trunk(x, p, rel_bias, layers, tiles=None):
    s_len = x.shape[1]
    tiles = tiles or _tiles(s_len)
    tables = _bias_tables(rel_bias, tiles["t_diff"])
    rope = _axial_rope(s_len)
    for i, w in enumerate(layers):
        x = _layer(x, p[i], i, tables, rope, w, tiles)
    return x


def _prep_layers(w_in, w_o, g_pre, g_post, sink_a, pool_w, pool_scale, lam_q1, lam_k1, lam_q2, lam_k2,
                 diff_subln, qnorm_d, knorm_d, w_pe, w_pg):
    layers = []
    for i in range(w_in.shape[0]):
        layers.append(dict(
            w_in=_prep_w_in(w_in[i]), w_o=w_o[i].astype(BF16), g_pre=g_pre[i], g_post=g_post[i],
            sink=jnp.broadcast_to(sink_a[i].astype(F32)[:, None], (A_HEADS, LANES)),
            pool_w=pool_w[i], pool_scale=pool_scale[i],
            lam=jnp.stack([lam_q1[i], lam_k1[i], lam_q2[i], lam_k2[i]]).astype(F32),
            diff_subln=diff_subln[i], qnorm_d=qnorm_d[i], knorm_d=knorm_d[i],
            w_pe=w_pe[i].astype(BF16), w_pg=w_pg[i].astype(BF16)))
    return layers


def kernel(x_prompt, x_sample, p_prompt, p_sample, w_in, w_o, g_pre, g_post, sink_a, pool_w, pool_scale,
           lam_q1, lam_k1, lam_q2, lam_k2, diff_subln, qnorm_d, knorm_d, rel_bias, w_pe, w_pg):
    layers = _prep_layers(w_in, w_o, g_pre, g_post, sink_a, pool_w, pool_scale, lam_q1, lam_k1, lam_q2,
                          lam_k2, diff_subln, qnorm_d, knorm_d, w_pe, w_pg)
    y_prompt = _trunk(x_prompt, p_prompt, rel_bias, layers)
    y_sample = _trunk(x_sample, p_sample, rel_bias, layers)
    return (y_prompt, y_sample)
```

```python
import functools
import math

import jax
import jax.numpy as jnp
from jax import lax
from jax.experimental import pallas as pl
from jax.experimental.pallas import tpu as pltpu

F32 = jnp.float32
BF16 = jnp.bfloat16

D_MODEL = 2048
HEAD_DIM = 64
LANES = 128
PLE_DIM = 256
EPS = 1e-6
NEG_INF = -1e30
SCALE = HEAD_DIM ** -0.5
LOG2E = math.log2(math.e)
VT_ONES = 16

A_HEADS = 8
A_WINDOW = 128
A_BLOCK = 128
B_GROUPS = 4
B_POOL_SIZES = (2, 4, 8, 16)
B_HALO = 64
C_HEADS = 4
REL_BUCKETS = 32
REL_MAX_DIST = 128
ROPE_THETA = 10000.0
GRID_W = 64

IN_SIZES = (512, 128, 128, 512, 512, 512, 512, 512, 512, 512, 512, 128, 128, 512)
MIX_WIDTH = 2048

U_WIDTH = 6144
COL_AQ, COL_AG, COL_BX, COL_BG = 0, 4, 8, 12
COL_CQ, COL_CK, COL_CV, COL_CG = 16, 20, 24, 28
COL_DQ, COL_DG = 32, 36
COL_AK, COL_AV, COL_DK, COL_DV = 40, 42, 44, 46

VMEM_LIMIT = 56 * 1024 * 1024


def _params(sem, vmem=VMEM_LIMIT):
    return pltpu.CompilerParams(dimension_semantics=sem, vmem_limit_bytes=vmem)


def _dot(a, b):
    return jnp.dot(a, b, preferred_element_type=F32)


def _dot_nt(a, b):
    return lax.dot_general(a, b, (((1,), (1,)), ((), ())), preferred_element_type=F32)


def _silu(x):
    return x / (1.0 + jnp.exp(-x))


def _lane_is_low(width=LANES):
    return (lax.broadcasted_iota(jnp.int32, (1, width), 1) % LANES) < HEAD_DIM


def _in_proj_kernel(x_ref, g_ref, w_ref, o_ref, h_ref):
    @pl.when(pl.program_id(1) == 0)
    def _():
        x = x_ref[...]
        ms = jnp.mean(x * x, axis=-1, keepdims=True)
        h_ref[...] = (x * lax.rsqrt(ms + EPS) * g_ref[...]).astype(BF16)

    o_ref[...] = _dot(h_ref[...], w_ref[...]).astype(o_ref.dtype)


def _in_proj(x2, g_pre, w_in_p, tm, tn):
    m = x2.shape[0]
    return pl.pallas_call(
        _in_proj_kernel,
        grid=(m // tm, U_WIDTH // tn),
        in_specs=[
            pl.BlockSpec((tm, D_MODEL), lambda i, j: (i, 0)),
            pl.BlockSpec((1, D_MODEL), lambda i, j: (0, 0)),
            pl.BlockSpec((D_MODEL, tn), lambda i, j: (0, j)),
        ],
        out_specs=pl.BlockSpec((tm, tn), lambda i, j: (i, j)),
        out_shape=jax.ShapeDtypeStruct((m, U_WIDTH), BF16),
        scratch_shapes=[pltpu.VMEM((tm, D_MODEL), BF16)],
        compiler_params=_params(("parallel", "arbitrary")),
        name="in_proj",
    )(x2, g_pre.reshape(1, D_MODEL), w_in_p)


def _win_attn_kernel(q_ref, kp_ref, kc_ref, kn_ref, vp_ref, vc_ref, vn_ref, g_ref, bias_ref,
                     sink_ref, o_ref, *, s_len):
    b = pl.program_id(1)
    low = _lane_is_low()
    row = lax.broadcasted_iota(jnp.int32, (A_BLOCK, 3 * A_BLOCK), 0)
    col = lax.broadcasted_iota(jnp.int32, (A_BLOCK, 3 * A_BLOCK), 1)
    rel = col - A_BLOCK - row
    kpos = b * A_BLOCK - A_BLOCK + col
    valid = (jnp.abs(rel) <= A_WINDOW) & (kpos >= 0) & (kpos < s_len)
    for pair in range(A_HEADS // 2):
        kvh = pair // 2
        lanes = slice(kvh * LANES, (kvh + 1) * LANES)
        kk = jnp.concatenate([kp_ref[0, :, lanes], kc_ref[0, :, lanes], kn_ref[0, :, lanes]], axis=0)
        vv = jnp.concatenate([vp_ref[0, :, lanes], vc_ref[0, :, lanes], vn_ref[0, :, lanes]], axis=0)
        q = q_ref[0, :, pair * LANES:(pair + 1) * LANES]
        outs = []
        for half in range(2):
            h = 2 * pair + half
            qm = jnp.where(low if half == 0 else ~low, q, jnp.zeros_like(q))
            sc = _dot_nt(qm, kk) + bias_ref[h]
            sc = jnp.where(valid, sc, NEG_INF)
            sink = sink_ref[h:h + 1, 0:1]
            m = jnp.maximum(jnp.max(sc, axis=-1, keepdims=True), sink)
            p = jnp.exp(sc - m)
            denom = jnp.sum(p, axis=-1, keepdims=True) + jnp.exp(sink - m)
            outs.append(_dot(p.astype(BF16), vv) / denom)
        y = jnp.where(low, outs[0], outs[1])
        gate = g_ref[0, :, pair * LANES:(pair + 1) * LANES].astype(F32)
        o_ref[0, :, pair * LANES:(pair + 1) * LANES] = (y * _silu(gate)).astype(o_ref.dtype)


def _win_attn(u3, bias_a, sink_b):
    bsz, s_len, _ = u3.shape
    nb = s_len // A_BLOCK
    kv = (1, A_BLOCK, 2 * LANES)
    prev = lambda c: (lambda bi, b: (bi, jnp.maximum(b - 1, 0), c))
    cur = lambda c: (lambda bi, b: (bi, b, c))
    nxt = lambda c: (lambda bi, b: (bi, jnp.minimum(b + 1, nb - 1), c))
    wide = (1, A_BLOCK, 4 * LANES)
    return pl.pallas_call(
        functools.partial(_win_attn_kernel, s_len=s_len),
        grid=(bsz, nb),
        in_specs=[
            pl.BlockSpec(wide, cur(COL_AQ // 4)),
            pl.BlockSpec(kv, prev(COL_AK // 2)),
            pl.BlockSpec(kv, cur(COL_AK // 2)),
            pl.BlockSpec(kv, nxt(COL_AK // 2)),
            pl.BlockSpec(kv, prev(COL_AV // 2)),
            pl.BlockSpec(kv, cur(COL_AV // 2)),
            pl.BlockSpec(kv, nxt(COL_AV // 2)),
            pl.BlockSpec(wide, cur(COL_AG // 4)),
            pl.BlockSpec((A_HEADS, A_BLOCK, 3 * A_BLOCK), lambda bi, b: (0, 0, 0)),
            pl.BlockSpec((A_HEADS, LANES), lambda bi, b: (0, 0)),
        ],
        out_specs=pl.BlockSpec(wide, lambda bi, b: (bi, b, 0)),
        out_shape=jax.ShapeDtypeStruct((bsz, s_len, 4 * LANES), BF16),
        compiler_params=_params(("parallel", "parallel")),
        name="win_attn",
    )(u3, u3, u3, u3, u3, u3, u3, u3, bias_a, sink_b)


def _pool_kernel(xp_ref, xc_ref, xn_ref, g_ref, band_ref, w_ref, sc_ref, o_ref, *, s_len, tq):
    i = pl.program_id(1)
    n = pl.num_programs(1)
    xp = xp_ref[0]
    xn = xn_ref[0]
    xp = jnp.where(i > 0, xp, jnp.zeros_like(xp))
    xn = jnp.where(i < n - 1, xn, jnp.zeros_like(xn))
    xc = xc_ref[0]
    xext = jnp.concatenate([xp, xc, xn], axis=0)
    t = i * tq + lax.broadcasted_iota(jnp.int32, (tq, 1), 0)
    for gi, size in enumerate(B_POOL_SIZES):
        lo = size // 2
        hi = size - lo - 1
        lanes = slice(gi * LANES, (gi + 1) * LANES)
        win_sum = _dot(band_ref[gi], xext[:, lanes])
        start = jnp.maximum(t - lo, 0)
        end = jnp.minimum(t + hi, s_len - 1) + 1
        pooled = win_sum / (end - start).astype(F32) - xc[:, lanes].astype(F32)
        y = _dot(pooled.astype(BF16), w_ref[gi]) * sc_ref[:, lanes]
        gate = g_ref[0, :, lanes].astype(F32)
        o_ref[0, :, lanes] = (y * _silu(gate)).astype(o_ref.dtype)


def _pool_bands(tq):
    r = jnp.arange(tq)[:, None]
    c = jnp.arange(tq + 2 * B_HALO)[None, :] - B_HALO
    bands = []
    for size in B_POOL_SIZES:
        lo = size // 2
        hi = size - lo - 1
        bands.append(((c - r >= -lo) & (c - r <= hi)).astype(BF16))
    return jnp.stack(bands)


def _pool(u3, pool_w, pool_scale, tq):
    bsz, s_len, _ = u3.shape
    nt = s_len // tq
    per = tq // B_HALO
    nh = s_len // B_HALO
    wide = (1, tq, 4 * LANES)
    halo = (1, B_HALO, 4 * LANES)
    return pl.pallas_call(
        functools.partial(_pool_kernel, s_len=s_len, tq=tq),
        grid=(bsz, nt),
        in_specs=[
            pl.BlockSpec(halo, lambda bi, i: (bi, jnp.maximum(i * per - 1, 0), COL_BX // 4)),
            pl.BlockSpec(wide, lambda bi, i: (bi, i, COL_BX // 4)),
            pl.BlockSpec(halo, lambda bi, i: (bi, jnp.minimum((i + 1) * per, nh - 1), COL_BX // 4)),
            pl.BlockSpec(wide, lambda bi, i: (bi, i, COL_BG // 4)),
            pl.BlockSpec((B_GROUPS, tq, tq + 2 * B_HALO), lambda bi, i: (0, 0, 0)),
            pl.BlockSpec((B_GROUPS, LANES, LANES), lambda bi, i: (0, 0, 0)),
            pl.BlockSpec((1, 4 * LANES), lambda bi, i: (0, 0)),
        ],
        out_specs=pl.BlockSpec(wide, lambda bi, i: (bi, i, 0)),
        out_shape=jax.ShapeDtypeStruct((bsz, s_len, 4 * LANES), BF16),
        compiler_params=_params(("parallel", "parallel")),
        name="pool",
    )(u3, u3, u3, u3, _pool_bands(tq), pool_w.astype(BF16), pool_scale.reshape(1, 4 * LANES))


def _softmax_t(st, m):
    m_new = jnp.maximum(m, jnp.max(st, axis=0, keepdims=True))
    return m_new, jnp.exp2(m - m_new), jnp.exp2(st - m_new).astype(BF16)


def _flash_units(n_units, ahead, qk, softmax, pv):
    scores = {u: qk(u) for u in range(min(ahead, n_units))}
    for u in range(n_units):
        probs = softmax(u, scores.pop(u))
        if u + ahead < n_units:
            scores[u + ahead] = qk(u + ahead)
        pv(u, probs)


def _same_bits(x):
    return lax.bitcast_convert_type(lax.bitcast_convert_type(x, jnp.int32), F32)


def _masked_halves(q, low):
    zero = jnp.zeros_like(q)
    return jnp.where(low, q, zero), jnp.where(low, zero, q)


def _diff_attn_kernel(q_ref, k_ref, vt_ref, g_ref, bias_ref, lam_ref, sub_ref, o_ref,
                      *, tq, tk, nb, n_iter, ahead, lambda_init):
    i = pl.program_id(2)
    ratio = tq // tk
    n_tab = bias_ref.shape[1]
    rows = vt_ref.shape[-2]
    qs = _masked_halves(q_ref[0], _lane_is_low())

    def step(j, carry):
        state = list(carry)
        blocks = [nb * j + b for b in range(nb)]
        ks = [k_ref[0, pl.ds(pl.multiple_of(blk * tk, tk), tk), :] for blk in blocks]
        vts = [vt_ref[0, 0, blk] for blk in blocks]
        tabs = [jnp.clip(blk - ratio * i + 2, 0, n_tab - 1) for blk in blocks]

        def qk(u):
            return _dot_nt(ks[u // 2], qs[u % 2])

        def softmax(u, st):
            return _softmax_t(_same_bits(st) + bias_ref[0, tabs[u // 2]], state[u % 2][0])

        def pv(u, probs):
            m_new, alpha, pt = probs
            state[u % 2] = (m_new, alpha * state[u % 2][1] + _dot(vts[u // 2], pt))

        _flash_units(2 * nb, ahead, qk, softmax, pv)
        return tuple(state)

    init = tuple((jnp.full((1, tq), NEG_INF, F32), jnp.zeros((rows, tq), F32)) for _ in range(2))
    (_, a1), (_, a2) = lax.fori_loop(0, n_iter, step, init)

    dot1 = jnp.sum(lam_ref[0:1, :] * lam_ref[1:2, :], axis=-1, keepdims=True)
    dot2 = jnp.sum(lam_ref[2:3, :] * lam_ref[3:4, :], axis=-1, keepdims=True)
    lam = jnp.exp(dot1) - jnp.exp(dot2) + lambda_init
    yt = a1[:LANES] / a1[LANES:LANES + 1] - lam * (a2[:LANES] / a2[LANES:LANES + 1])
    y = yt.T
    ms = jnp.mean(y * y, axis=-1, keepdims=True)
    y = y * lax.rsqrt(ms + EPS) * sub_ref[...] * (1.0 - lambda_init)
    o_ref[0] = (y * _silu(g_ref[0].astype(F32))).astype(o_ref.dtype)


def _diff_attn(u3, vtc, bias_c, lam_vecs, subln, lambda_init, tq, tk, nb, ahead):
    bsz, s_len, _ = u3.shape
    nk = s_len // tk
    rows = vtc.shape[-2]
    n_tab = bias_c.shape[1]
    return pl.pallas_call(
        functools.partial(_diff_attn_kernel, tq=tq, tk=tk, nb=nb, n_iter=nk // nb, ahead=ahead,
                          lambda_init=lambda_init),
        grid=(bsz, C_HEADS, s_len // tq),
        in_specs=[
            pl.BlockSpec((1, tq, LANES), lambda bi, h, i: (bi, i, COL_CQ + h)),
            pl.BlockSpec((1, s_len, LANES), lambda bi, h, i: (bi, 0, COL_CK + h)),
            pl.BlockSpec((1, 1, nk, rows, tk), lambda bi, h, i: (bi, h, 0, 0, 0)),
            pl.BlockSpec((1, tq, LANES), lambda bi, h, i: (bi, i, COL_CG + h)),
            pl.BlockSpec((1, n_tab, tk, tq), lambda bi, h, i: (h, 0, 0, 0)),
            pl.BlockSpec((4, HEAD_DIM), lambda bi, h, i: (0, 0)),
            pl.BlockSpec((1, LANES), lambda bi, h, i: (0, 0)),
        ],
        out_specs=pl.BlockSpec((1, tq, LANES), lambda bi, h, i: (bi, i, h)),
        out_shape=jax.ShapeDtypeStruct((bsz, s_len, 4 * LANES), BF16),
        compiler_params=_params(("parallel", "parallel", "parallel")),
        name="diff_attn",
    )(u3, u3, vtc, u3, bias_c, lam_vecs, subln.reshape(1, LANES))


def _grid_attn_kernel(q_ref, k_ref, vt_ref, g_ref, o_ref, *, tq, tk, nb, n_iter, ahead):
    low = _lane_is_low()
    rows = vt_ref.shape[-2]
    qs = (_masked_halves(q_ref[0, :, :LANES], low) + _masked_halves(q_ref[0, :, LANES:], low))

    def step(j, carry):
        state = list(carry)
        blocks = [nb * j + b for b in range(nb)]
        ks = [k_ref[0, pl.ds(pl.multiple_of(blk * tk, tk), tk), :] for blk in blocks]
        vts = [vt_ref[0, 0, blk] for blk in blocks]

        def qk(u):
            return _dot_nt(ks[u // 4], qs[u % 4])

        def softmax(u, st):
            return _softmax_t(st, state[u % 4][0])

        def pv(u, probs):
            m_new, alpha, pt = probs
            state[u % 4] = (m_new, alpha * state[u % 4][1] + _dot(vts[u // 4], pt))

        _flash_units(4 * nb, ahead, qk, softmax, pv)
        return tuple(state)

    init = tuple((jnp.full((1, tq), NEG_INF, F32), jnp.zeros((rows, tq), F32)) for _ in range(4))
    state = lax.fori_loop(0, n_iter, step, init)
    for pair in range(2):
        a0, a1 = state[2 * pair][1], state[2 * pair + 1][1]
        yt = jnp.concatenate([a0[:HEAD_DIM] / a0[HEAD_DIM:HEAD_DIM + 1],
                              a1[:HEAD_DIM] / a1[HEAD_DIM:HEAD_DIM + 1]], axis=0)
        gate = g_ref[0, :, pair * LANES:(pair + 1) * LANES].astype(F32)
        o_ref[0, :, pair * LANES:(pair + 1) * LANES] = (yt.T * _silu(gate)).astype(o_ref.dtype)


def _grid_attn(qd3, kd3, vtd, u3, tq, tk, nb, ahead):
    bsz, s_len, _ = qd3.shape
    nk = s_len // tk
    rows = vtd.shape[-2]
    return pl.pallas_call(
        functools.partial(_grid_attn_kernel, tq=tq, tk=tk, nb=nb, n_iter=nk // nb, ahead=ahead),
        grid=(bsz, 2, s_len // tq),
        in_specs=[
            pl.BlockSpec((1, tq, 2 * LANES), lambda bi, h, i: (bi, i, h)),
            pl.BlockSpec((1, s_len, LANES), lambda bi, h, i: (bi, 0, h)),
            pl.BlockSpec((1, 1, nk, rows, tk), lambda bi, h, i: (bi, h, 0, 0, 0)),
            pl.BlockSpec((1, tq, 2 * LANES), lambda bi, h, i: (bi, i, COL_DG // 2 + h)),
        ],
        out_specs=pl.BlockSpec((1, tq, 2 * LANES), lambda bi, h, i: (bi, i, h)),
        out_shape=jax.ShapeDtypeStruct((bsz, s_len, 4 * LANES), BF16),
        compiler_params=_params(("parallel", "parallel", "parallel")),
        name="grid_attn",
    )(qd3, kd3, vtd, u3)


def _norm_rope(x, gain, cos, sin, seg):
    w = x.shape[-1]
    x2 = x * x
    hi = x2.astype(BF16)
    lo = (x2 - hi.astype(F32)).astype(BF16)
    ss = _dot(hi, seg) + _dot(lo, seg)
    y = x * lax.rsqrt(ss * (1.0 / HEAD_DIM) + EPS) * gain
    quarter = HEAD_DIM // 4
    ahead = pltpu.roll(y, w - quarter, 1)
    behind = pltpu.roll(y, quarter, 1)
    first = (lax.broadcasted_iota(jnp.int32, (1, w), 1) % (2 * quarter)) < quarter
    rot = jnp.where(first, -ahead, behind)
    reps = w // LANES
    cos_w = jnp.concatenate([cos] * reps, axis=1)
    sin_w = jnp.concatenate([sin] * reps, axis=1)
    return y * cos_w + rot * sin_w


def _attn_prep_kernel(q_ref, k_ref, vc_ref, vd_ref, cos_ref, sin_ref, gq_ref, gk_ref, seg_ref,
                      qo_ref, ko_ref, vtc_ref, vtd_ref):
    cos = cos_ref[...]
    sin = sin_ref[...]
    seg = seg_ref[...]
    q = _norm_rope(q_ref[...].astype(F32), gq_ref[...], cos, sin, seg)
    qo_ref[...] = (q * (SCALE * LOG2E)).astype(qo_ref.dtype)
    k = _norm_rope(k_ref[...].astype(F32), gk_ref[...], cos, sin, seg[:2 * LANES, :2 * LANES])
    ko_ref[...] = k.astype(ko_ref.dtype)
    ones = jnp.ones((VT_ONES, q.shape[0]), BF16)
    vct = vc_ref[...].astype(F32).T
    for h in range(C_HEADS):
        vtc_ref[0, h, 0] = jnp.concatenate([vct[h * LANES:(h + 1) * LANES].astype(BF16), ones], axis=0)
    vdt = vd_ref[...].astype(F32).T
    for h in range(2):
        vtd_ref[0, h, 0] = jnp.concatenate(
            [vdt[h * LANES:h * LANES + HEAD_DIM].astype(BF16), ones], axis=0)


def _attn_prep(u2, cos, sin, qnorm, knorm, bsz, s_len, tm):
    m = u2.shape[0]
    per_seq = s_len // tm
    lane = jnp.arange(4 * LANES)
    seg = (lane[:, None] // HEAD_DIM == lane[None, :] // HEAD_DIM).astype(BF16)
    rows_c = LANES + VT_ONES
    rows_d = HEAD_DIM + VT_ONES
    return pl.pallas_call(
        _attn_prep_kernel,
        grid=(m // tm,),
        in_specs=[
            pl.BlockSpec((tm, 4 * LANES), lambda i: (i, COL_DQ // 4)),
            pl.BlockSpec((tm, 2 * LANES), lambda i: (i, COL_DK // 2)),
            pl.BlockSpec((tm, 4 * LANES), lambda i: (i, COL_CV // 4)),
            pl.BlockSpec((tm, 2 * LANES), lambda i: (i, COL_DV // 2)),
            pl.BlockSpec((tm, LANES), lambda i: (i % per_seq, 0)),
            pl.BlockSpec((tm, LANES), lambda i: (i % per_seq, 0)),
            pl.BlockSpec((1, 4 * LANES), lambda i: (0, 0)),
            pl.BlockSpec((1, 2 * LANES), lambda i: (0, 0)),
            pl.BlockSpec((4 * LANES, 4 * LANES), lambda i: (0, 0)),
        ],
        out_specs=[
            pl.BlockSpec((tm, 4 * LANES), lambda i: (i, 0)),
            pl.BlockSpec((tm, 2 * LANES), lambda i: (i, 0)),
            pl.BlockSpec((1, C_HEADS, 1, rows_c, tm), lambda i: (i // per_seq, 0, i % per_seq, 0, 0)),
            pl.BlockSpec((1, 2, 1, rows_d, tm), lambda i: (i // per_seq, 0, i % per_seq, 0, 0)),
        ],
        out_shape=[jax.ShapeDtypeStruct((m, 4 * LANES), BF16),
                   jax.ShapeDtypeStruct((m, 2 * LANES), BF16),
                   jax.ShapeDtypeStruct((bsz, C_HEADS, per_seq, rows_c, tm), BF16),
                   jax.ShapeDtypeStruct((bsz, 2, per_seq, rows_d, tm), BF16)],
        compiler_params=_params(("parallel",)),
        name="attn_prep",
    )(u2, u2, u2, u2, cos, sin, jnp.tile(qnorm, 8).reshape(1, 4 * LANES),
      jnp.tile(knorm, 4).reshape(1, 2 * LANES), seg)


def _out_kernel(x_ref, ya_ref, yb_ref, yc_ref, yd_ref, pe_ref, wo_ref, gp_ref, wpe_ref, wpg_ref, o_ref):
    width = 4 * LANES
    z = _dot(ya_ref[...], wo_ref[0 * width:1 * width, :])
    z += _dot(yb_ref[...], wo_ref[1 * width:2 * width, :])
    z += _dot(yc_ref[...], wo_ref[2 * width:3 * width, :])
    z += _dot(yd_ref[...], wo_ref[3 * width:4 * width, :])
    ms = jnp.mean(z * z, axis=-1, keepdims=True)
    x = x_ref[...] + z * lax.rsqrt(ms + EPS) * gp_ref[...]
    logits = _dot(x.astype(BF16), wpg_ref[...])
    gate = 1.0 / (1.0 + jnp.exp(-logits))
    pe = _dot(pe_ref[...].astype(BF16), wpe_ref[...])
    o_ref[...] = x + pe * gate


def _out_stage(x2, ya, yb, yc, yd, pe2, w_o, g_post, w_pe, w_pg, tm):
    m = x2.shape[0]
    row = lambda w: pl.BlockSpec((tm, w), lambda i: (i, 0))
    const = lambda r, c: pl.BlockSpec((r, c), lambda i: (0, 0), pipeline_mode=pl.Buffered(1))
    return pl.pallas_call(
        _out_kernel,
        grid=(m // tm,),
        in_specs=[row(D_MODEL), row(4 * LANES), row(4 * LANES), row(4 * LANES), row(4 * LANES),
                  row(PLE_DIM), const(MIX_WIDTH, D_MODEL), const(1, D_MODEL),
                  const(PLE_DIM, D_MODEL), const(D_MODEL, D_MODEL)],
        out_specs=row(D_MODEL),
        out_shape=jax.ShapeDtypeStruct((m, D_MODEL), F32),
        compiler_params=_params(("parallel",)),
        name="out_stage",
    )(x2, ya, yb, yc, yd, pe2, w_o, g_post.reshape(1, D_MODEL), w_pe, w_pg)


def _prep_w_in(w):
    parts, start = [], 0
    for n in IN_SIZES:
        parts.append(w[:, start:start + n])
        start += n
    aq, ak, av, ag, bx, bg, cq, ck, cv, cg, dq, dk, dv, dg = parts

    def dup(t):
        return jnp.concatenate([t[:, :HEAD_DIM], t[:, :HEAD_DIM], t[:, HEAD_DIM:], t[:, HEAD_DIM:]], axis=1)

    cols = [aq * SCALE, ag, bx, bg, cq * (SCALE * LOG2E), ck, cv, cg, dq, dg, dup(ak), dup(av), dup(dk), dup(dv)]
    return jnp.concatenate(cols, axis=1).astype(BF16)


def _rel_bucket(rel):
    nb = REL_BUCKETS // 2
    max_exact = nb // 2
    ret = jnp.where(rel > 0, nb, 0)
    n = jnp.abs(rel)
    nf = jnp.maximum(n, 1).astype(F32)
    large = max_exact + (jnp.log(nf / max_exact) / math.log(REL_MAX_DIST / max_exact)
                         * (nb - max_exact)).astype(jnp.int32)
    large = jnp.minimum(large, nb - 1)
    return ret + jnp.where(n < max_exact, n, large)


def _bucket_values(tab, rel):
    bucket = _rel_bucket(rel)[None]
    out = jnp.zeros((tab.shape[1],) + rel.shape, F32)
    for b in range(REL_BUCKETS):
        out = jnp.where(bucket == b, tab[b].reshape((-1,) + (1,) * rel.ndim), out)
    return out


def _bias_tables(rel_bias, tq, tk):
    tab = rel_bias.astype(F32)
    rel_a = jnp.arange(3 * A_BLOCK)[None, :] - A_BLOCK - jnp.arange(A_BLOCK)[:, None]
    bias_a = _bucket_values(tab[:, :A_HEADS], rel_a)
    assert tk >= REL_MAX_DIST
    n_tab = tq // tk + 4
    d = jnp.arange(n_tab)[:, None, None]
    c = jnp.arange(tk)[None, :, None]
    r = jnp.arange(tq)[None, None, :]
    bias_c = _bucket_values(tab[:, A_HEADS:], c + (d - 2) * tk - r) * LOG2E
    return bias_a, bias_c


def _axial_rope(s_len):
    rows = s_len // GRID_W
    t_row = jnp.repeat(jnp.arange(rows), GRID_W).astype(F32)
    t_col = jnp.tile(jnp.arange(GRID_W), rows).astype(F32)
    half = HEAD_DIM // 2
    inv = ROPE_THETA ** (-jnp.arange(0, half, 2, dtype=F32) / half)
    ang_r = t_row[:, None] * inv[None, :]
    ang_c = t_col[:, None] * inv[None, :]
    ang = jnp.concatenate([ang_r, ang_r, ang_c, ang_c] * 2, axis=-1)
    return jnp.cos(ang), jnp.sin(ang)


def _tiles(s_len):
    return dict(tm_in=min(1024, s_len), tn_in=512, tq_pool=min(256, s_len), tm_out=min(256, s_len),
                tq_attn=512, tk_attn=256, nb_c=8, nb_d=8, ahead=3)


def _layer(x, pe, layer_idx, tables, rope, w, tiles):
    bsz, s_len, _ = x.shape
    m = bsz * s_len
    bias_a, bias_c = tables
    cos, sin = rope
    tq, tk, ahead = tiles["tq_attn"], tiles["tk_attn"], tiles["ahead"]
    x2 = x.reshape(m, D_MODEL)
    u2 = _in_proj(x2, w["g_pre"], w["w_in"], tiles["tm_in"], tiles["tn_in"])
    u3 = u2.reshape(bsz, s_len, U_WIDTH)
    ya = _win_attn(u3, bias_a, w["sink"])
    yb = _pool(u3, w["pool_w"], w["pool_scale"], tiles["tq_pool"])
    qd, kd, vtc, vtd = _attn_prep(u2, cos, sin, w["qnorm_d"], w["knorm_d"], bsz, s_len, tk)
    lambda_init = 0.8 - 0.6 * math.exp(-0.3 * layer_idx)
    yc = _diff_attn(u3, vtc, bias_c, w["lam"], w["diff_subln"], lambda_init, tq, tk, tiles["nb_c"], ahead)
    yd = _grid_attn(qd.reshape(bsz, s_len, -1), kd.reshape(bsz, s_len, -1), vtd, u3, tq, tk,
                    tiles["nb_d"], ahead)
    flat = lambda y: y.reshape(m, 4 * LANES)
    out = _out_stage(x2, flat(ya), flat(yb), flat(yc), flat(yd), pe.reshape(m, PLE_DIM),
                     w["w_o"], w["g_post"], w["w_pe"], w["w_pg"], tiles["tm_out"])
    return out.reshape(bsz, s_len, D_MODEL)


def _trunk(x, p, rel_bias, layers, tiles=None):
    s_len = x.shape[1]
    tiles = tiles or _tiles(s_len)
    tables = _bias_tables(rel_bias, tiles["tq_attn"], tiles["tk_attn"])
    rope = _axial_rope(s_len)
    for i, w in enumerate(layers):
        x = _layer(x, p[i], i, tables, rope, w, tiles)
    return x


def _prep_layers(w_in, w_o, g_pre, g_post, sink_a, pool_w, pool_scale, lam_q1, lam_k1, lam_q2, lam_k2,
                 diff_subln, qnorm_d, knorm_d, w_pe, w_pg):
    layers = []
    for i in range(w_in.shape[0]):
        layers.append(dict(
            w_in=_prep_w_in(w_in[i]), w_o=w_o[i].astype(BF16), g_pre=g_pre[i], g_post=g_post[i],
            sink=jnp.broadcast_to(sink_a[i].astype(F32)[:, None], (A_HEADS, LANES)),
            pool_w=pool_w[i], pool_scale=pool_scale[i],
            lam=jnp.stack([lam_q1[i], lam_k1[i], lam_q2[i], lam_k2[i]]).astype(F32),
            diff_subln=diff_subln[i], qnorm_d=qnorm_d[i], knorm_d=knorm_d[i],
            w_pe=w_pe[i].astype(BF16), w_pg=w_pg[i].astype(BF16)))
    return layers


def kernel(x_prompt, x_sample, p_prompt, p_sample, w_in, w_o, g_pre, g_post, sink_a, pool_w, pool_scale,
           lam_q1, lam_k1, lam_q2, lam_k2, diff_subln, qnorm_d, knorm_d, rel_bias, w_pe, w_pg):
    layers = _prep_layers(w_in, w_o, g_pre, g_post, sink_a, pool_w, pool_scale, lam_q1, lam_k1, lam_q2,
                          lam_k2, diff_subln, qnorm_d, knorm_d, w_pe, w_pg)
    y_prompt = _trunk(x_prompt, p_prompt, rel_bias, layers)
    y_sample = _trunk(x_sample, p_sample, rel_bias, layers)
    return (y_prompt, y_sample)
```

```python
import functools
import math

import jax
import jax.numpy as jnp
from jax import lax
from jax.experimental import pallas as pl
from jax.experimental.pallas import tpu as pltpu

F32 = jnp.float32
BF16 = jnp.bfloat16

D_MODEL = 2048
HEAD_DIM = 64
LANES = 128
PLE_DIM = 256
EPS = 1e-6
NEG_INF = -1e30
SCALE = HEAD_DIM ** -0.5
LOG2E = math.log2(math.e)
VT_ONES = 16
KV_ROWS = HEAD_DIM + VT_ONES

A_HEADS = 8
A_WINDOW = 128
A_BLOCK = 128
B_GROUPS = 4
B_POOL_SIZES = (2, 4, 8, 16)
B_HALO = 64
C_HEADS = 4
REL_BUCKETS = 32
REL_MAX_DIST = 128
ROPE_THETA = 10000.0
GRID_W = 64

IN_SIZES = (512, 128, 128, 512, 512, 512, 512, 512, 512, 512, 512, 128, 128, 512)
MIX_WIDTH = 2048

U_WIDTH = 5632
COL_AQ, COL_AG, COL_BX, COL_BG = 0, 4, 8, 12
COL_CQ, COL_CK, COL_CV, COL_CG = 16, 20, 24, 28
COL_DQ, COL_DG = 32, 36
COL_AK, COL_AV, COL_DK, COL_DV = 40, 41, 42, 43

VMEM_LIMIT = 56 * 1024 * 1024


def _params(sem, vmem=VMEM_LIMIT):
    return pltpu.CompilerParams(dimension_semantics=sem, vmem_limit_bytes=vmem)


def _dot(a, b):
    return jnp.dot(a, b, preferred_element_type=F32)


def _dot_nt(a, b):
    return lax.dot_general(a, b, (((1,), (1,)), ((), ())), preferred_element_type=F32)


def _silu(x):
    return x / (1.0 + jnp.exp(-x))


def _lane_is_low(width=LANES):
    return (lax.broadcasted_iota(jnp.int32, (1, width), 1) % LANES) < HEAD_DIM


def _in_proj_kernel(x_ref, g_ref, w_ref, o_ref, h_ref):
    @pl.when(pl.program_id(1) == 0)
    def _():
        x = x_ref[...]
        ms = jnp.mean(x * x, axis=-1, keepdims=True)
        h_ref[...] = (x * lax.rsqrt(ms + EPS) * g_ref[...]).astype(BF16)

    o_ref[...] = _dot(h_ref[...], w_ref[...]).astype(o_ref.dtype)


def _in_proj(x2, g_pre, w_in_p, tm, tn):
    m = x2.shape[0]
    return pl.pallas_call(
        _in_proj_kernel,
        grid=(m // tm, U_WIDTH // tn),
        in_specs=[
            pl.BlockSpec((tm, D_MODEL), lambda i, j: (i, 0)),
            pl.BlockSpec((1, D_MODEL), lambda i, j: (0, 0)),
            pl.BlockSpec((D_MODEL, tn), lambda i, j: (0, j)),
        ],
        out_specs=pl.BlockSpec((tm, tn), lambda i, j: (i, j)),
        out_shape=jax.ShapeDtypeStruct((m, U_WIDTH), BF16),
        scratch_shapes=[pltpu.VMEM((tm, D_MODEL), BF16)],
        compiler_params=_params(("parallel", "arbitrary")),
        name="in_proj",
    )(x2, g_pre.reshape(1, D_MODEL), w_in_p)


def _group_queries(x, group, low):
    xr = pltpu.roll(x, HEAD_DIM, 1)
    zero = jnp.zeros_like(x)
    if group == 0:
        return [jnp.where(low, x, zero), jnp.where(low, xr, zero)]
    return [jnp.where(low, zero, xr), jnp.where(low, zero, x)]


def _win_attn_kernel(q_ref, kp_ref, kc_ref, kn_ref, vp_ref, vc_ref, vn_ref, g_ref, bias_ref,
                     sink_ref, o_ref, *, nq, ahead):
    i = pl.program_id(1)
    n = pl.num_programs(1)
    low = _lane_is_low()
    kext = jnp.concatenate([kp_ref[0], kc_ref[0], kn_ref[0]], axis=0)
    vts = [vp_ref[0, 0]] + [vc_ref[0, b] for b in range(nq)] + [vn_ref[0, 0]]
    units = [(qb, grp) for qb in range(nq) for grp in range(2)]

    def qk(u):
        qb, grp = units[u]
        rows = slice(qb * A_BLOCK, (qb + 1) * A_BLOCK)
        heads = []
        for pair in (2 * grp, 2 * grp + 1):
            heads += _group_queries(q_ref[0, rows, pair * LANES:(pair + 1) * LANES], grp, low)
        return _dot_nt(kext[qb * A_BLOCK:(qb + 3) * A_BLOCK], jnp.concatenate(heads, axis=0))

    def softmax(u, st):
        qb, grp = units[u]
        variant = 1
        if qb == 0:
            variant = jnp.where(i == 0, 0, variant)
        if qb == nq - 1:
            variant = jnp.where(i == n - 1, 2, variant)
        sb = _same_bits(st) + bias_ref[variant, grp]
        m = jnp.maximum(jnp.max(sb, axis=0, keepdims=True), sink_ref[grp])
        return m, jnp.exp2(sb - m).astype(BF16)

    def pv(u, probs):
        qb, grp = units[u]
        m, pt = probs
        vt = jnp.concatenate([vts[qb + d][grp] for d in range(3)], axis=1)
        acc = _dot(vt, pt)
        ot = acc[:HEAD_DIM] / (acc[HEAD_DIM:HEAD_DIM + 1] + jnp.exp2(sink_ref[grp] - m))
        rows = slice(qb * A_BLOCK, (qb + 1) * A_BLOCK)
        for pair in range(2):
            yt = jnp.concatenate([ot[:, (2 * pair) * A_BLOCK:(2 * pair + 1) * A_BLOCK],
                                  ot[:, (2 * pair + 1) * A_BLOCK:(2 * pair + 2) * A_BLOCK]], axis=0)
            cols = slice((2 * grp + pair) * LANES, (2 * grp + pair + 1) * LANES)
            gate = g_ref[0, rows, cols].astype(F32)
            o_ref[0, rows, cols] = (yt.T * _silu(gate)).astype(o_ref.dtype)

    _flash_units(len(units), ahead, qk, softmax, pv)


def _win_attn(u3, vta, bias_a, sink_a, tq, ahead):
    bsz, s_len, _ = u3.shape
    nq = tq // A_BLOCK
    nblk = s_len // A_BLOCK
    wide = (1, tq, 4 * LANES)
    const = lambda shape: pl.BlockSpec(shape, lambda bi, i: (0,) * len(shape), pipeline_mode=pl.Buffered(1))
    return pl.pallas_call(
        functools.partial(_win_attn_kernel, nq=nq, ahead=ahead),
        grid=(bsz, s_len // tq),
        in_specs=[
            pl.BlockSpec(wide, lambda bi, i: (bi, i, COL_AQ // 4)),
            pl.BlockSpec((1, A_BLOCK, LANES), lambda bi, i: (bi, jnp.maximum(i * nq - 1, 0), COL_AK)),
            pl.BlockSpec((1, tq, LANES), lambda bi, i: (bi, i, COL_AK)),
            pl.BlockSpec((1, A_BLOCK, LANES), lambda bi, i: (bi, jnp.minimum((i + 1) * nq, nblk - 1), COL_AK)),
            pl.BlockSpec((1, 1, 2, KV_ROWS, A_BLOCK), lambda bi, i: (bi, jnp.maximum(i * nq - 1, 0), 0, 0, 0)),
            pl.BlockSpec((1, nq, 2, KV_ROWS, A_BLOCK), lambda bi, i: (bi, i, 0, 0, 0)),
            pl.BlockSpec((1, 1, 2, KV_ROWS, A_BLOCK),
                         lambda bi, i: (bi, jnp.minimum((i + 1) * nq, nblk - 1), 0, 0, 0)),
            pl.BlockSpec(wide, lambda bi, i: (bi, i, COL_AG // 4)),
            const(bias_a.shape),
            const(sink_a.shape),
        ],
        out_specs=pl.BlockSpec(wide, lambda bi, i: (bi, i, 0)),
        out_shape=jax.ShapeDtypeStruct((bsz, s_len, 4 * LANES), BF16),
        compiler_params=_params(("parallel", "parallel")),
        name="win_attn",
    )(u3, u3, u3, u3, vta, vta, vta, u3, bias_a, sink_a)


def _pool_kernel(xp_ref, xc_ref, xn_ref, g_ref, band_ref, w_ref, sc_ref, o_ref, *, s_len, tq):
    i = pl.program_id(1)
    n = pl.num_programs(1)
    xp = xp_ref[0]
    xn = xn_ref[0]
    xp = jnp.where(i > 0, xp, jnp.zeros_like(xp))
    xn = jnp.where(i < n - 1, xn, jnp.zeros_like(xn))
    xc = xc_ref[0]
    xext = jnp.concatenate([xp, xc, xn], axis=0)
    t = i * tq + lax.broadcasted_iota(jnp.int32, (tq, 1), 0)
    for gi, size in enumerate(B_POOL_SIZES):
        lo = size // 2
        hi = size - lo - 1
        lanes = slice(gi * LANES, (gi + 1) * LANES)
        win_sum = _dot(band_ref[gi], xext[:, lanes])
        start = jnp.maximum(t - lo, 0)
        end = jnp.minimum(t + hi, s_len - 1) + 1
        pooled = win_sum / (end - start).astype(F32) - xc[:, lanes].astype(F32)
        y = _dot(pooled.astype(BF16), w_ref[gi]) * sc_ref[:, lanes]
        gate = g_ref[0, :, lanes].astype(F32)
        o_ref[0, :, lanes] = (y * _silu(gate)).astype(o_ref.dtype)


def _pool_bands(tq):
    r = jnp.arange(tq)[:, None]
    c = jnp.arange(tq + 2 * B_HALO)[None, :] - B_HALO
    bands = []
    for size in B_POOL_SIZES:
        lo = size // 2
        hi = size - lo - 1
        bands.append(((c - r >= -lo) & (c - r <= hi)).astype(BF16))
    return jnp.stack(bands)


def _pool(u3, pool_w, pool_scale, tq):
    bsz, s_len, _ = u3.shape
    nt = s_len // tq
    per = tq // B_HALO
    nh = s_len // B_HALO
    wide = (1, tq, 4 * LANES)
    halo = (1, B_HALO, 4 * LANES)
    return pl.pallas_call(
        functools.partial(_pool_kernel, s_len=s_len, tq=tq),
        grid=(bsz, nt),
        in_specs=[
            pl.BlockSpec(halo, lambda bi, i: (bi, jnp.maximum(i * per - 1, 0), COL_BX // 4)),
            pl.BlockSpec(wide, lambda bi, i: (bi, i, COL_BX // 4)),
            pl.BlockSpec(halo, lambda bi, i: (bi, jnp.minimum((i + 1) * per, nh - 1), COL_BX // 4)),
            pl.BlockSpec(wide, lambda bi, i: (bi, i, COL_BG // 4)),
            pl.BlockSpec((B_GROUPS, tq, tq + 2 * B_HALO), lambda bi, i: (0, 0, 0)),
            pl.BlockSpec((B_GROUPS, LANES, LANES), lambda bi, i: (0, 0, 0)),
            pl.BlockSpec((1, 4 * LANES), lambda bi, i: (0, 0)),
        ],
        out_specs=pl.BlockSpec(wide, lambda bi, i: (bi, i, 0)),
        out_shape=jax.ShapeDtypeStruct((bsz, s_len, 4 * LANES), BF16),
        compiler_params=_params(("parallel", "parallel")),
        name="pool",
    )(u3, u3, u3, u3, _pool_bands(tq), pool_w.astype(BF16), pool_scale.reshape(1, 4 * LANES))


def _softmax_t(st, m):
    m_new = jnp.maximum(m, jnp.max(st, axis=0, keepdims=True))
    return m_new, jnp.exp2(m - m_new), jnp.exp2(st - m_new).astype(BF16)


def _flash_units(n_units, ahead, qk, softmax, pv):
    scores = {u: qk(u) for u in range(min(ahead, n_units))}
    for u in range(n_units):
        probs = softmax(u, scores.pop(u))
        if u + ahead < n_units:
            scores[u + ahead] = qk(u + ahead)
        pv(u, probs)


def _same_bits(x):
    return lax.bitcast_convert_type(lax.bitcast_convert_type(x, jnp.int32), F32)


def _masked_halves(q, low):
    zero = jnp.zeros_like(q)
    return jnp.where(low, q, zero), jnp.where(low, zero, q)


def _diff_attn_kernel(q_ref, k_ref, vt_ref, g_ref, bias_ref, lam_ref, sub_ref, o_ref,
                      *, tq, tk, nb, n_iter, ahead, lambda_init):
    i = pl.program_id(2)
    ratio = tq // tk
    n_tab = bias_ref.shape[1]
    rows = vt_ref.shape[-2]
    qs = _masked_halves(q_ref[0], _lane_is_low())

    def step(j, carry):
        state = list(carry)
        blocks = [nb * j + b for b in range(nb)]
        ks = [k_ref[0, pl.ds(pl.multiple_of(blk * tk, tk), tk), :] for blk in blocks]
        vts = [vt_ref[0, 0, blk] for blk in blocks]
        tabs = [jnp.clip(blk - ratio * i + 2, 0, n_tab - 1) for blk in blocks]

        def qk(u):
            return _dot_nt(ks[u // 2], qs[u % 2])

        def softmax(u, st):
            return _softmax_t(_same_bits(st) + bias_ref[0, tabs[u // 2]], state[u % 2][0])

        def pv(u, probs):
            m_new, alpha, pt = probs
            state[u % 2] = (m_new, alpha * state[u % 2][1] + _dot(vts[u // 2], pt))

        _flash_units(2 * nb, ahead, qk, softmax, pv)
        return tuple(state)

    init = tuple((jnp.full((1, tq), NEG_INF, F32), jnp.zeros((rows, tq), F32)) for _ in range(2))
    (_, a1), (_, a2) = lax.fori_loop(0, n_iter, step, init)

    dot1 = jnp.sum(lam_ref[0:1, :] * lam_ref[1:2, :], axis=-1, keepdims=True)
    dot2 = jnp.sum(lam_ref[2:3, :] * lam_ref[3:4, :], axis=-1, keepdims=True)
    lam = jnp.exp(dot1) - jnp.exp(dot2) + lambda_init
    yt = a1[:LANES] / a1[LANES:LANES + 1] - lam * (a2[:LANES] / a2[LANES:LANES + 1])
    y = yt.T
    ms = jnp.mean(y * y, axis=-1, keepdims=True)
    y = y * lax.rsqrt(ms + EPS) * sub_ref[...] * (1.0 - lambda_init)
    o_ref[0] = (y * _silu(g_ref[0].astype(F32))).astype(o_ref.dtype)


def _diff_attn(u3, vtc, bias_c, lam_vecs, subln, lambda_init, tq, tk, nb, ahead):
    bsz, s_len, _ = u3.shape
    nk = s_len // tk
    rows = vtc.shape[-2]
    n_tab = bias_c.shape[1]
    return pl.pallas_call(
        functools.partial(_diff_attn_kernel, tq=tq, tk=tk, nb=nb, n_iter=nk // nb, ahead=ahead,
                          lambda_init=lambda_init),
        grid=(bsz, C_HEADS, s_len // tq),
        in_specs=[
            pl.BlockSpec((1, tq, LANES), lambda bi, h, i: (bi, i, COL_CQ + h)),
            pl.BlockSpec((1, s_len, LANES), lambda bi, h, i: (bi, 0, COL_CK + h)),
            pl.BlockSpec((1, 1, nk, rows, tk), lambda bi, h, i: (bi, h, 0, 0, 0)),
            pl.BlockSpec((1, tq, LANES), lambda bi, h, i: (bi, i, COL_CG + h)),
            pl.BlockSpec((1, n_tab, tk, tq), lambda bi, h, i: (h, 0, 0, 0)),
            pl.BlockSpec((4, HEAD_DIM), lambda bi, h, i: (0, 0)),
            pl.BlockSpec((1, LANES), lambda bi, h, i: (0, 0)),
        ],
        out_specs=pl.BlockSpec((1, tq, LANES), lambda bi, h, i: (bi, i, h)),
        out_shape=jax.ShapeDtypeStruct((bsz, s_len, 4 * LANES), BF16),
        compiler_params=_params(("parallel", "parallel", "parallel")),
        name="diff_attn",
    )(u3, u3, vtc, u3, bias_c, lam_vecs, subln.reshape(1, LANES))


def _grid_attn_kernel(q_ref, k_ref, vt_ref, g_ref, o_ref, *, tq, tk, nb, n_iter, ahead):
    grp = pl.program_id(1)
    lane = lax.broadcasted_iota(jnp.int32, (1, LANES), 1)
    keep = (lane >= grp * HEAD_DIM) & (lane < (grp + 1) * HEAD_DIM)
    rows = vt_ref.shape[-2]
    qs = []
    for pair in range(2):
        x = q_ref[0, :, pair * LANES:(pair + 1) * LANES]
        xr = pltpu.roll(x, HEAD_DIM, 1)
        zero = jnp.zeros_like(x)
        qs.append(jnp.where(keep, jnp.where(grp == 0, x, xr), zero))
        qs.append(jnp.where(keep, jnp.where(grp == 0, xr, x), zero))

    def step(j, carry):
        state = list(carry)
        blocks = [nb * j + b for b in range(nb)]
        ks = [k_ref[0, pl.ds(pl.multiple_of(blk * tk, tk), tk), :] for blk in blocks]
        vts = [vt_ref[0, 0, blk] for blk in blocks]

        def qk(u):
            return _dot_nt(ks[u // 4], qs[u % 4])

        def softmax(u, st):
            return _softmax_t(st, state[u % 4][0])

        def pv(u, probs):
            m_new, alpha, pt = probs
            state[u % 4] = (m_new, alpha * state[u % 4][1] + _dot(vts[u // 4], pt))

        _flash_units(4 * nb, ahead, qk, softmax, pv)
        return tuple(state)

    init = tuple((jnp.full((1, tq), NEG_INF, F32), jnp.zeros((rows, tq), F32)) for _ in range(4))
    state = lax.fori_loop(0, n_iter, step, init)
    for pair in range(2):
        a0, a1 = state[2 * pair][1], state[2 * pair + 1][1]
        yt = jnp.concatenate([a0[:HEAD_DIM] / a0[HEAD_DIM:HEAD_DIM + 1],
                              a1[:HEAD_DIM] / a1[HEAD_DIM:HEAD_DIM + 1]], axis=0)
        gate = g_ref[0, :, pair * LANES:(pair + 1) * LANES].astype(F32)
        o_ref[0, :, pair * LANES:(pair + 1) * LANES] = (yt.T * _silu(gate)).astype(o_ref.dtype)


def _grid_attn(qd3, kd3, vtd, u3, tq, tk, nb, ahead):
    bsz, s_len, _ = qd3.shape
    nk = s_len // tk
    rows = vtd.shape[-2]
    return pl.pallas_call(
        functools.partial(_grid_attn_kernel, tq=tq, tk=tk, nb=nb, n_iter=nk // nb, ahead=ahead),
        grid=(bsz, 2, s_len // tq),
        in_specs=[
            pl.BlockSpec((1, tq, 2 * LANES), lambda bi, h, i: (bi, i, h)),
            pl.BlockSpec((1, s_len, LANES), lambda bi, h, i: (bi, 0, 0)),
            pl.BlockSpec((1, 1, nk, rows, tk), lambda bi, h, i: (bi, h, 0, 0, 0)),
            pl.BlockSpec((1, tq, 2 * LANES), lambda bi, h, i: (bi, i, COL_DG // 2 + h)),
        ],
        out_specs=pl.BlockSpec((1, tq, 2 * LANES), lambda bi, h, i: (bi, i, h)),
        out_shape=jax.ShapeDtypeStruct((bsz, s_len, 4 * LANES), BF16),
        compiler_params=_params(("parallel", "parallel", "parallel")),
        name="grid_attn",
    )(qd3, kd3, vtd, u3)


def _norm_rope(x, gain, cos, sin, seg):
    w = x.shape[-1]
    x2 = x * x
    hi = x2.astype(BF16)
    lo = (x2 - hi.astype(F32)).astype(BF16)
    ss = _dot(hi, seg) + _dot(lo, seg)
    y = x * lax.rsqrt(ss * (1.0 / HEAD_DIM) + EPS) * gain
    quarter = HEAD_DIM // 4
    ahead = pltpu.roll(y, w - quarter, 1)
    behind = pltpu.roll(y, quarter, 1)
    first = (lax.broadcasted_iota(jnp.int32, (1, w), 1) % (2 * quarter)) < quarter
    rot = jnp.where(first, -ahead, behind)
    reps = w // LANES
    cos_w = jnp.concatenate([cos] * reps, axis=1)
    sin_w = jnp.concatenate([sin] * reps, axis=1)
    return y * cos_w + rot * sin_w


def _attn_prep_kernel(q_ref, k_ref, vc_ref, vd_ref, va_ref, cos_ref, sin_ref, gq_ref, gk_ref, seg_ref,
                      qo_ref, ko_ref, vtc_ref, vtd_ref, vta_ref):
    cos = cos_ref[...]
    sin = sin_ref[...]
    seg = seg_ref[...]
    q = _norm_rope(q_ref[...].astype(F32), gq_ref[...], cos, sin, seg)
    qo_ref[...] = (q * (SCALE * LOG2E)).astype(qo_ref.dtype)
    k = _norm_rope(k_ref[...].astype(F32), gk_ref[...], cos, sin, seg[:LANES, :LANES])
    ko_ref[...] = k.astype(ko_ref.dtype)
    tm = q.shape[0]
    ones = jnp.ones((VT_ONES, tm), BF16)
    vct = vc_ref[...].astype(F32).T
    for h in range(C_HEADS):
        vtc_ref[0, h, 0] = jnp.concatenate([vct[h * LANES:(h + 1) * LANES].astype(BF16), ones], axis=0)
    vdt = vd_ref[...].astype(F32).T.astype(BF16)
    vat = va_ref[...].astype(F32).T.astype(BF16)
    for h in range(2):
        vtd_ref[0, h, 0] = jnp.concatenate([vdt[h * HEAD_DIM:(h + 1) * HEAD_DIM], ones], axis=0)
        for kb in range(tm // A_BLOCK):
            cols = slice(kb * A_BLOCK, (kb + 1) * A_BLOCK)
            vta_ref[0, kb, h] = jnp.concatenate([vat[h * HEAD_DIM:(h + 1) * HEAD_DIM, cols], ones[:, cols]], axis=0)


def _attn_prep(u2, cos, sin, qnorm, knorm, bsz, s_len, tm):
    m = u2.shape[0]
    per_seq = s_len // tm
    per_tile = tm // A_BLOCK
    lane = jnp.arange(4 * LANES)
    seg = (lane[:, None] // HEAD_DIM == lane[None, :] // HEAD_DIM).astype(BF16)
    rows_c = LANES + VT_ONES
    return pl.pallas_call(
        _attn_prep_kernel,
        grid=(m // tm,),
        in_specs=[
            pl.BlockSpec((tm, 4 * LANES), lambda i: (i, COL_DQ // 4)),
            pl.BlockSpec((tm, LANES), lambda i: (i, COL_DK)),
            pl.BlockSpec((tm, 4 * LANES), lambda i: (i, COL_CV // 4)),
            pl.BlockSpec((tm, LANES), lambda i: (i, COL_DV)),
            pl.BlockSpec((tm, LANES), lambda i: (i, COL_AV)),
            pl.BlockSpec((tm, LANES), lambda i: (i % per_seq, 0)),
            pl.BlockSpec((tm, LANES), lambda i: (i % per_seq, 0)),
            pl.BlockSpec((1, 4 * LANES), lambda i: (0, 0)),
            pl.BlockSpec((1, LANES), lambda i: (0, 0)),
            pl.BlockSpec((4 * LANES, 4 * LANES), lambda i: (0, 0)),
        ],
        out_specs=[
            pl.BlockSpec((tm, 4 * LANES), lambda i: (i, 0)),
            pl.BlockSpec((tm, LANES), lambda i: (i, 0)),
            pl.BlockSpec((1, C_HEADS, 1, rows_c, tm), lambda i: (i // per_seq, 0, i % per_seq, 0, 0)),
            pl.BlockSpec((1, 2, 1, KV_ROWS, tm), lambda i: (i // per_seq, 0, i % per_seq, 0, 0)),
            pl.BlockSpec((1, per_tile, 2, KV_ROWS, A_BLOCK), lambda i: (i // per_seq, i % per_seq, 0, 0, 0)),
        ],
        out_shape=[jax.ShapeDtypeStruct((m, 4 * LANES), BF16),
                   jax.ShapeDtypeStruct((m, LANES), BF16),
                   jax.ShapeDtypeStruct((bsz, C_HEADS, per_seq, rows_c, tm), BF16),
                   jax.ShapeDtypeStruct((bsz, 2, per_seq, KV_ROWS, tm), BF16),
                   jax.ShapeDtypeStruct((bsz, s_len // A_BLOCK, 2, KV_ROWS, A_BLOCK), BF16)],
        compiler_params=_params(("parallel",)),
        name="attn_prep",
    )(u2, u2, u2, u2, u2, cos, sin, jnp.tile(qnorm, 8).reshape(1, 4 * LANES),
      jnp.tile(knorm, 2).reshape(1, LANES), seg)


def _out_kernel(x_ref, ya_ref, yb_ref, yc_ref, yd_ref, pe_ref, wo_ref, gp_ref, wpe_ref, wpg_ref, o_ref):
    width = 4 * LANES
    z = _dot(ya_ref[...], wo_ref[0 * width:1 * width, :])
    z += _dot(yb_ref[...], wo_ref[1 * width:2 * width, :])
    z += _dot(yc_ref[...], wo_ref[2 * width:3 * width, :])
    z += _dot(yd_ref[...], wo_ref[3 * width:4 * width, :])
    ms = jnp.mean(z * z, axis=-1, keepdims=True)
    x = x_ref[...] + z * lax.rsqrt(ms + EPS) * gp_ref[...]
    logits = _dot(x.astype(BF16), wpg_ref[...])
    gate = 1.0 / (1.0 + jnp.exp(-logits))
    pe = _dot(pe_ref[...].astype(BF16), wpe_ref[...])
    o_ref[...] = x + pe * gate


def _out_stage(x2, ya, yb, yc, yd, pe2, w_o, g_post, w_pe, w_pg, tm):
    m = x2.shape[0]
    row = lambda w: pl.BlockSpec((tm, w), lambda i: (i, 0))
    const = lambda r, c: pl.BlockSpec((r, c), lambda i: (0, 0), pipeline_mode=pl.Buffered(1))
    return pl.pallas_call(
        _out_kernel,
        grid=(m // tm,),
        in_specs=[row(D_MODEL), row(4 * LANES), row(4 * LANES), row(4 * LANES), row(4 * LANES),
                  row(PLE_DIM), const(MIX_WIDTH, D_MODEL), const(1, D_MODEL),
                  const(PLE_DIM, D_MODEL), const(D_MODEL, D_MODEL)],
        out_specs=row(D_MODEL),
        out_shape=jax.ShapeDtypeStruct((m, D_MODEL), F32),
        compiler_params=_params(("parallel",)),
        name="out_stage",
    )(x2, ya, yb, yc, yd, pe2, w_o, g_post.reshape(1, D_MODEL), w_pe, w_pg)


def _prep_w_in(w):
    parts, start = [], 0
    for n in IN_SIZES:
        parts.append(w[:, start:start + n])
        start += n
    aq, ak, av, ag, bx, bg, cq, ck, cv, cg, dq, dk, dv, dg = parts

    log2_scale = SCALE * LOG2E
    cols = [aq * log2_scale, ag, bx, bg, cq * log2_scale, ck, cv, cg, dq, dg, ak, av, dk, dv]
    return jnp.concatenate(cols, axis=1).astype(BF16)


def _rel_bucket(rel):
    nb = REL_BUCKETS // 2
    max_exact = nb // 2
    ret = jnp.where(rel > 0, nb, 0)
    n = jnp.abs(rel)
    nf = jnp.maximum(n, 1).astype(F32)
    large = max_exact + (jnp.log(nf / max_exact) / math.log(REL_MAX_DIST / max_exact)
                         * (nb - max_exact)).astype(jnp.int32)
    large = jnp.minimum(large, nb - 1)
    return ret + jnp.where(n < max_exact, n, large)


def _bucket_values(tab, rel):
    bucket = _rel_bucket(rel)[None]
    out = jnp.zeros((tab.shape[1],) + rel.shape, F32)
    for b in range(REL_BUCKETS):
        out = jnp.where(bucket == b, tab[b].reshape((-1,) + (1,) * rel.ndim), out)
    return out


def _bias_tables(rel_bias, tq, tk):
    tab = rel_bias.astype(F32)
    ck = jnp.arange(3 * A_BLOCK)[:, None]
    rq = jnp.arange(A_BLOCK)[None, :]
    rel_a = ck - A_BLOCK - rq
    vals = _bucket_values(tab[:, :A_HEADS], rel_a) * LOG2E
    in_band = jnp.abs(rel_a) <= A_WINDOW
    variants = []
    for edge in (ck >= A_BLOCK, ck >= 0, ck < 2 * A_BLOCK):
        masked = jnp.where(in_band & edge, vals, NEG_INF)
        variants.append(jnp.transpose(masked.reshape(2, 4, 3 * A_BLOCK, A_BLOCK), (0, 2, 1, 3))
                        .reshape(2, 3 * A_BLOCK, 4 * A_BLOCK))
    bias_a = jnp.stack(variants)
    assert tk >= REL_MAX_DIST
    n_tab = tq // tk + 4
    d = jnp.arange(n_tab)[:, None, None]
    c = jnp.arange(tk)[None, :, None]
    r = jnp.arange(tq)[None, None, :]
    bias_c = _bucket_values(tab[:, A_HEADS:], c + (d - 2) * tk - r) * LOG2E
    return bias_a, bias_c


def _axial_rope(s_len):
    rows = s_len // GRID_W
    t_row = jnp.repeat(jnp.arange(rows), GRID_W).astype(F32)
    t_col = jnp.tile(jnp.arange(GRID_W), rows).astype(F32)
    half = HEAD_DIM // 2
    inv = ROPE_THETA ** (-jnp.arange(0, half, 2, dtype=F32) / half)
    ang_r = t_row[:, None] * inv[None, :]
    ang_c = t_col[:, None] * inv[None, :]
    ang = jnp.concatenate([ang_r, ang_r, ang_c, ang_c] * 2, axis=-1)
    return jnp.cos(ang), jnp.sin(ang)


def _tiles(s_len):
    return dict(tm_in=min(1024, s_len), tn_in=512, tq_pool=min(256, s_len), tm_out=min(256, s_len),
                tq_attn=512, tk_attn=256, nb_c=8, nb_d=8, ahead=3, tq_win=512, ahead_win=1)


def _layer(x, pe, layer_idx, tables, rope, w, tiles):
    bsz, s_len, _ = x.shape
    m = bsz * s_len
    bias_a, bias_c = tables
    cos, sin = rope
    tq, tk, ahead = tiles["tq_attn"], tiles["tk_attn"], tiles["ahead"]
    x2 = x.reshape(m, D_MODEL)
    u2 = _in_proj(x2, w["g_pre"], w["w_in"], tiles["tm_in"], tiles["tn_in"])
    u3 = u2.reshape(bsz, s_len, U_WIDTH)
    qd, kd, vtc, vtd, vta = _attn_prep(u2, cos, sin, w["qnorm_d"], w["knorm_d"], bsz, s_len, tk)
    ya = _win_attn(u3, vta, bias_a, w["sink"], tiles["tq_win"], tiles["ahead_win"])
    yb = _pool(u3, w["pool_w"], w["pool_scale"], tiles["tq_pool"])
    lambda_init = 0.8 - 0.6 * math.exp(-0.3 * layer_idx)
    yc = _diff_attn(u3, vtc, bias_c, w["lam"], w["diff_subln"], lambda_init, tq, tk, tiles["nb_c"], ahead)
    yd = _grid_attn(qd.reshape(bsz, s_len, -1), kd.reshape(bsz, s_len, -1), vtd, u3, tq, tk,
                    tiles["nb_d"], ahead)
    flat = lambda y: y.reshape(m, 4 * LANES)
    out = _out_stage(x2, flat(ya), flat(yb), flat(yc), flat(yd), pe.reshape(m, PLE_DIM),
                     w["w_o"], w["g_post"], w["w_pe"], w["w_pg"], tiles["tm_out"])
    return out.reshape(bsz, s_len, D_MODEL)


def _trunk(x, p, rel_bias, layers, tiles=None):
    s_len = x.shape[1]
    tiles = tiles or _tiles(s_len)
    tables = _bias_tables(rel_bias, tiles["tq_attn"], tiles["tk_attn"])
    rope = _axial_rope(s_len)
    for i, w in enumerate(layers):
        x = _layer(x, p[i], i, tables, rope, w, tiles)
    return x


def _prep_layers(w_in, w_o, g_pre, g_post, sink_a, pool_w, pool_scale, lam_q1, lam_k1, lam_q2, lam_k2,
                 diff_subln, qnorm_d, knorm_d, w_pe, w_pg):
    layers = []
    for i in range(w_in.shape[0]):
        layers.append(dict(
            w_in=_prep_w_in(w_in[i]), w_o=w_o[i].astype(BF16), g_pre=g_pre[i], g_post=g_post[i],
            sink=jnp.repeat(sink_a[i].astype(F32) * LOG2E, A_BLOCK).reshape(2, 1, 4 * A_BLOCK),
            pool_w=pool_w[i], pool_scale=pool_scale[i],
            lam=jnp.stack([lam_q1[i], lam_k1[i], lam_q2[i], lam_k2[i]]).astype(F32),
            diff_subln=diff_subln[i], qnorm_d=qnorm_d[i], knorm_d=knorm_d[i],
            w_pe=w_pe[i].astype(BF16), w_pg=w_pg[i].astype(BF16)))
    return layers


def kernel(x_prompt, x_sample, p_prompt, p_sample, w_in, w_o, g_pre, g_post, sink_a, pool_w, pool_scale,
           lam_q1, lam_k1, lam_q2, lam_k2, diff_subln, qnorm_d, knorm_d, rel_bias, w_pe, w_pg):
    layers = _prep_layers(w_in, w_o, g_pre, g_post, sink_a, pool_w, pool_scale, lam_q1, lam_k1, lam_q2,
                          lam_k2, diff_subln, qnorm_d, knorm_d, w_pe, w_pg)
    y_prompt = _trunk(x_prompt, p_prompt, rel_bias, layers)
    y_sample = _trunk(x_sample, p_sample, rel_bias, layers)
    return (y_prompt, y_sample)
```

```python
import functools
import math

import jax
import jax.numpy as jnp
from jax import lax
from jax.experimental import pallas as pl
from jax.experimental.pallas import tpu as pltpu

F32 = jnp.float32
BF16 = jnp.bfloat16

D_MODEL = 2048
HEAD_DIM = 64
LANES = 128
PLE_DIM = 256
EPS = 1e-6
NEG_INF = -1e30
SCALE = HEAD_DIM ** -0.5
LOG2E = math.log2(math.e)
VT_ONES = 16
KV_ROWS = HEAD_DIM + VT_ONES

A_HEADS = 8
A_WINDOW = 128
A_BLOCK = 128
B_GROUPS = 4
B_POOL_SIZES = (2, 4, 8, 16)
B_HALO = 64
C_HEADS = 4
REL_BUCKETS = 32
REL_MAX_DIST = 128
ROPE_THETA = 10000.0
GRID_W = 64

IN_SIZES = (512, 128, 128, 512, 512, 512, 512, 512, 512, 512, 512, 128, 128, 512)
MIX_WIDTH = 2048

U_WIDTH = 5632
COL_AQ, COL_AG, COL_BX, COL_BG = 0, 4, 8, 12
COL_CQ, COL_CK, COL_CV, COL_CG = 16, 20, 24, 28
COL_DQ, COL_DG = 32, 36
COL_AK, COL_AV, COL_DK, COL_DV = 40, 41, 42, 43

VMEM_LIMIT = 56 * 1024 * 1024


def _params(sem, vmem=VMEM_LIMIT):
    return pltpu.CompilerParams(dimension_semantics=sem, vmem_limit_bytes=vmem)


def _dot(a, b):
    return jnp.dot(a, b, preferred_element_type=F32)


def _dot_nt(a, b):
    return lax.dot_general(a, b, (((1,), (1,)), ((), ())), preferred_element_type=F32)


def _silu(x):
    return x / (1.0 + jnp.exp(-x))


def _lane_is_low(width=LANES):
    return (lax.broadcasted_iota(jnp.int32, (1, width), 1) % LANES) < HEAD_DIM


def _in_proj_kernel(x_ref, g_ref, w_ref, o_ref, h_ref):
    @pl.when(pl.program_id(1) == 0)
    def _():
        x = x_ref[...]
        ms = jnp.mean(x * x, axis=-1, keepdims=True)
        h_ref[...] = (x * lax.rsqrt(ms + EPS) * g_ref[...]).astype(BF16)

    o_ref[...] = _dot(h_ref[...], w_ref[...]).astype(o_ref.dtype)


def _in_proj(x2, g_pre, w_in_p, tm, tn):
    m = x2.shape[0]
    return pl.pallas_call(
        _in_proj_kernel,
        grid=(m // tm, U_WIDTH // tn),
        in_specs=[
            pl.BlockSpec((tm, D_MODEL), lambda i, j: (i, 0)),
            pl.BlockSpec((1, D_MODEL), lambda i, j: (0, 0)),
            pl.BlockSpec((D_MODEL, tn), lambda i, j: (0, j)),
        ],
        out_specs=pl.BlockSpec((tm, tn), lambda i, j: (i, j)),
        out_shape=jax.ShapeDtypeStruct((m, U_WIDTH), BF16),
        scratch_shapes=[pltpu.VMEM((tm, D_MODEL), BF16)],
        compiler_params=_params(("parallel", "arbitrary")),
        name="in_proj",
    )(x2, g_pre.reshape(1, D_MODEL), w_in_p)


def _group_queries(x, group, low):
    xr = pltpu.roll(x, HEAD_DIM, 1)
    zero = jnp.zeros_like(x)
    if group == 0:
        return [jnp.where(low, x, zero), jnp.where(low, xr, zero)]
    return [jnp.where(low, zero, xr), jnp.where(low, zero, x)]


def _win_attn_kernel(q_ref, kp_ref, kc_ref, kn_ref, vp_ref, vc_ref, vn_ref, g_ref, bias_ref,
                     sink_ref, o_ref, *, nq, ahead):
    i = pl.program_id(1)
    n = pl.num_programs(1)
    low = _lane_is_low()
    kext = jnp.concatenate([kp_ref[0], kc_ref[0], kn_ref[0]], axis=0)
    vts = [vp_ref[0, 0]] + [vc_ref[0, b] for b in range(nq)] + [vn_ref[0, 0]]
    units = [(qb, grp) for qb in range(nq) for grp in range(2)]

    def qk(u):
        qb, grp = units[u]
        rows = slice(qb * A_BLOCK, (qb + 1) * A_BLOCK)
        heads = []
        for pair in (2 * grp, 2 * grp + 1):
            heads += _group_queries(q_ref[0, rows, pair * LANES:(pair + 1) * LANES], grp, low)
        return _dot_nt(kext[qb * A_BLOCK:(qb + 3) * A_BLOCK], jnp.concatenate(heads, axis=0))

    def softmax(u, st):
        qb, grp = units[u]
        variant = 1
        if qb == 0:
            variant = jnp.where(i == 0, 0, variant)
        if qb == nq - 1:
            variant = jnp.where(i == n - 1, 2, variant)
        sb = _same_bits(st) + bias_ref[variant, grp]
        m = jnp.maximum(jnp.max(sb, axis=0, keepdims=True), sink_ref[grp])
        return m, jnp.exp2(sb - m).astype(BF16)

    def pv(u, probs):
        qb, grp = units[u]
        m, pt = probs
        vt = jnp.concatenate([vts[qb + d][grp] for d in range(3)], axis=1)
        acc = _dot(vt, pt)
        ot = acc[:HEAD_DIM] / (acc[HEAD_DIM:HEAD_DIM + 1] + jnp.exp2(sink_ref[grp] - m))
        rows = slice(qb * A_BLOCK, (qb + 1) * A_BLOCK)
        for pair in range(2):
            yt = jnp.concatenate([ot[:, (2 * pair) * A_BLOCK:(2 * pair + 1) * A_BLOCK],
                                  ot[:, (2 * pair + 1) * A_BLOCK:(2 * pair + 2) * A_BLOCK]], axis=0)
            cols = slice((2 * grp + pair) * LANES, (2 * grp + pair + 1) * LANES)
            gate = g_ref[0, rows, cols].astype(F32)
            o_ref[0, rows, cols] = (yt.T * _silu(gate)).astype(o_ref.dtype)

    _flash_units(len(units), ahead, qk, softmax, pv)


def _win_attn(u3, vta, bias_a, sink_a, tq, ahead):
    bsz, s_len, _ = u3.shape
    nq = tq // A_BLOCK
    nblk = s_len // A_BLOCK
    wide = (1, tq, 4 * LANES)
    const = lambda shape: pl.BlockSpec(shape, lambda bi, i: (0,) * len(shape), pipeline_mode=pl.Buffered(1))
    return pl.pallas_call(
        functools.partial(_win_attn_kernel, nq=nq, ahead=ahead),
        grid=(bsz, s_len // tq),
        in_specs=[
            pl.BlockSpec(wide, lambda bi, i: (bi, i, COL_AQ // 4)),
            pl.BlockSpec((1, A_BLOCK, LANES), lambda bi, i: (bi, jnp.maximum(i * nq - 1, 0), COL_AK)),
            pl.BlockSpec((1, tq, LANES), lambda bi, i: (bi, i, COL_AK)),
            pl.BlockSpec((1, A_BLOCK, LANES), lambda bi, i: (bi, jnp.minimum((i + 1) * nq, nblk - 1), COL_AK)),
            pl.BlockSpec((1, 1, 2, KV_ROWS, A_BLOCK), lambda bi, i: (bi, jnp.maximum(i * nq - 1, 0), 0, 0, 0)),
            pl.BlockSpec((1, nq, 2, KV_ROWS, A_BLOCK), lambda bi, i: (bi, i, 0, 0, 0)),
            pl.BlockSpec((1, 1, 2, KV_ROWS, A_BLOCK),
                         lambda bi, i: (bi, jnp.minimum((i + 1) * nq, nblk - 1), 0, 0, 0)),
            pl.BlockSpec(wide, lambda bi, i: (bi, i, COL_AG // 4)),
            const(bias_a.shape),
            const(sink_a.shape),
        ],
        out_specs=pl.BlockSpec(wide, lambda bi, i: (bi, i, 0)),
        out_shape=jax.ShapeDtypeStruct((bsz, s_len, 4 * LANES), BF16),
        compiler_params=_params(("parallel", "parallel")),
        name="win_attn",
    )(u3, u3, u3, u3, vta, vta, vta, u3, bias_a, sink_a)


def _pool_kernel(xp_ref, xc_ref, xn_ref, g_ref, band_ref, w_ref, sc_ref, o_ref, *, s_len, tq):
    i = pl.program_id(1)
    n = pl.num_programs(1)
    xp = xp_ref[0]
    xn = xn_ref[0]
    xp = jnp.where(i > 0, xp, jnp.zeros_like(xp))
    xn = jnp.where(i < n - 1, xn, jnp.zeros_like(xn))
    xc = xc_ref[0]
    xext = jnp.concatenate([xp, xc, xn], axis=0)
    t = i * tq + lax.broadcasted_iota(jnp.int32, (tq, 1), 0)
    for gi, size in enumerate(B_POOL_SIZES):
        lo = size // 2
        hi = size - lo - 1
        lanes = slice(gi * LANES, (gi + 1) * LANES)
        win_sum = _dot(band_ref[gi], xext[:, lanes])
        start = jnp.maximum(t - lo, 0)
        end = jnp.minimum(t + hi, s_len - 1) + 1
        pooled = win_sum / (end - start).astype(F32) - xc[:, lanes].astype(F32)
        y = _dot(pooled.astype(BF16), w_ref[gi]) * sc_ref[:, lanes]
        gate = g_ref[0, :, lanes].astype(F32)
        o_ref[0, :, lanes] = (y * _silu(gate)).astype(o_ref.dtype)


def _pool_bands(tq):
    r = jnp.arange(tq)[:, None]
    c = jnp.arange(tq + 2 * B_HALO)[None, :] - B_HALO
    bands = []
    for size in B_POOL_SIZES:
        lo = size // 2
        hi = size - lo - 1
        bands.append(((c - r >= -lo) & (c - r <= hi)).astype(BF16))
    return jnp.stack(bands)


def _pool(u3, pool_w, pool_scale, tq):
    bsz, s_len, _ = u3.shape
    nt = s_len // tq
    per = tq // B_HALO
    nh = s_len // B_HALO
    wide = (1, tq, 4 * LANES)
    halo = (1, B_HALO, 4 * LANES)
    return pl.pallas_call(
        functools.partial(_pool_kernel, s_len=s_len, tq=tq),
        grid=(bsz, nt),
        in_specs=[
            pl.BlockSpec(halo, lambda bi, i: (bi, jnp.maximum(i * per - 1, 0), COL_BX // 4)),
            pl.BlockSpec(wide, lambda bi, i: (bi, i, COL_BX // 4)),
            pl.BlockSpec(halo, lambda bi, i: (bi, jnp.minimum((i + 1) * per, nh - 1), COL_BX // 4)),
            pl.BlockSpec(wide, lambda bi, i: (bi, i, COL_BG // 4)),
            pl.BlockSpec((B_GROUPS, tq, tq + 2 * B_HALO), lambda bi, i: (0, 0, 0)),
            pl.BlockSpec((B_GROUPS, LANES, LANES), lambda bi, i: (0, 0, 0)),
            pl.BlockSpec((1, 4 * LANES), lambda bi, i: (0, 0)),
        ],
        out_specs=pl.BlockSpec(wide, lambda bi, i: (bi, i, 0)),
        out_shape=jax.ShapeDtypeStruct((bsz, s_len, 4 * LANES), BF16),
        compiler_params=_params(("parallel", "parallel")),
        name="pool",
    )(u3, u3, u3, u3, _pool_bands(tq), pool_w.astype(BF16), pool_scale.reshape(1, 4 * LANES))


def _softmax_t(st, m):
    m_new = jnp.maximum(m, jnp.max(st, axis=0, keepdims=True))
    return m_new, jnp.exp2(m - m_new), jnp.exp2(st - m_new).astype(BF16)


def _flash_units(n_units, ahead, qk, softmax, pv):
    scores = {u: qk(u) for u in range(min(ahead, n_units))}
    for u in range(n_units):
        probs = softmax(u, scores.pop(u))
        if u + ahead < n_units:
            scores[u + ahead] = qk(u + ahead)
        pv(u, probs)


def _same_bits(x):
    return lax.bitcast_convert_type(lax.bitcast_convert_type(x, jnp.int32), F32)


def _masked_halves(q, low):
    zero = jnp.zeros_like(q)
    return jnp.where(low, q, zero), jnp.where(low, zero, q)


def _diff_attn_kernel(q_ref, k_ref, vt_ref, g_ref, bias_ref, lam_ref, sub_ref, o_ref,
                      *, tq, tk, nb, n_iter, ahead, lambda_init):
    i = pl.program_id(2)
    ratio = tq // tk
    n_tab = bias_ref.shape[1]
    rows = vt_ref.shape[-2]
    qs = _masked_halves(q_ref[0], _lane_is_low())

    def step(j, carry):
        state = list(carry)
        blocks = [nb * j + b for b in range(nb)]
        ks = [k_ref[0, pl.ds(pl.multiple_of(blk * tk, tk), tk), :] for blk in blocks]
        vts = [vt_ref[0, 0, blk] for blk in blocks]
        tabs = [jnp.clip(blk - ratio * i + 2, 0, n_tab - 1) for blk in blocks]

        def qk(u):
            return _dot_nt(ks[u // 2], qs[u % 2])

        def softmax(u, st):
            return _softmax_t(_same_bits(st) + bias_ref[0, tabs[u // 2]], state[u % 2][0])

        def pv(u, probs):
            m_new, alpha, pt = probs
            state[u % 2] = (m_new, alpha * state[u % 2][1] + _dot(vts[u // 2], pt))

        _flash_units(2 * nb, ahead, qk, softmax, pv)
        return tuple(state)

    init = tuple((jnp.full((1, tq), NEG_INF, F32), jnp.zeros((rows, tq), F32)) for _ in range(2))
    (_, a1), (_, a2) = lax.fori_loop(0, n_iter, step, init)

    dot1 = jnp.sum(lam_ref[0:1, :] * lam_ref[1:2, :], axis=-1, keepdims=True)
    dot2 = jnp.sum(lam_ref[2:3, :] * lam_ref[3:4, :], axis=-1, keepdims=True)
    lam = jnp.exp(dot1) - jnp.exp(dot2) + lambda_init
    yt = a1[:LANES] / a1[LANES:LANES + 1] - lam * (a2[:LANES] / a2[LANES:LANES + 1])
    y = yt.T
    ms = jnp.mean(y * y, axis=-1, keepdims=True)
    y = y * lax.rsqrt(ms + EPS) * sub_ref[...] * (1.0 - lambda_init)
    o_ref[0] = (y * _silu(g_ref[0].astype(F32))).astype(o_ref.dtype)


def _diff_attn(u3, vtc, bias_c, lam_vecs, subln, lambda_init, tq, tk, nb, ahead):
    bsz, s_len, _ = u3.shape
    nk = s_len // tk
    rows = vtc.shape[-2]
    n_tab = bias_c.shape[1]
    return pl.pallas_call(
        functools.partial(_diff_attn_kernel, tq=tq, tk=tk, nb=nb, n_iter=nk // nb, ahead=ahead,
                          lambda_init=lambda_init),
        grid=(bsz, C_HEADS, s_len // tq),
        in_specs=[
            pl.BlockSpec((1, tq, LANES), lambda bi, h, i: (bi, i, COL_CQ + h)),
            pl.BlockSpec((1, s_len, LANES), lambda bi, h, i: (bi, 0, COL_CK + h)),
            pl.BlockSpec((1, 1, nk, rows, tk), lambda bi, h, i: (bi, h, 0, 0, 0)),
            pl.BlockSpec((1, tq, LANES), lambda bi, h, i: (bi, i, COL_CG + h)),
            pl.BlockSpec((1, n_tab, tk, tq), lambda bi, h, i: (h, 0, 0, 0)),
            pl.BlockSpec((4, HEAD_DIM), lambda bi, h, i: (0, 0)),
            pl.BlockSpec((1, LANES), lambda bi, h, i: (0, 0)),
        ],
        out_specs=pl.BlockSpec((1, tq, LANES), lambda bi, h, i: (bi, i, h)),
        out_shape=jax.ShapeDtypeStruct((bsz, s_len, 4 * LANES), BF16),
        compiler_params=_params(("parallel", "parallel", "parallel")),
        name="diff_attn",
    )(u3, u3, vtc, u3, bias_c, lam_vecs, subln.reshape(1, LANES))


def _grid_attn_kernel(q_ref, k_ref, vt_ref, g_ref, o_ref, *, tq, tk, nb, n_iter, ahead):
    grp = pl.program_id(1)
    lane = lax.broadcasted_iota(jnp.int32, (1, LANES), 1)
    keep = (lane >= grp * HEAD_DIM) & (lane < (grp + 1) * HEAD_DIM)
    rows = vt_ref.shape[-2]
    qs = []
    for pair in range(2):
        x = q_ref[0, :, pair * LANES:(pair + 1) * LANES]
        xr = pltpu.roll(x, HEAD_DIM, 1)
        zero = jnp.zeros_like(x)
        qs.append(jnp.where(keep, jnp.where(grp == 0, x, xr), zero))
        qs.append(jnp.where(keep, jnp.where(grp == 0, xr, x), zero))

    def step(j, carry):
        state = list(carry)
        blocks = [nb * j + b for b in range(nb)]
        ks = [k_ref[0, pl.ds(pl.multiple_of(blk * tk, tk), tk), :] for blk in blocks]
        vts = [vt_ref[0, 0, blk] for blk in blocks]

        def qk(u):
            return _dot_nt(ks[u // 4], qs[u % 4])

        def softmax(u, st):
            return _softmax_t(st, state[u % 4][0])

        def pv(u, probs):
            m_new, alpha, pt = probs
            state[u % 4] = (m_new, alpha * state[u % 4][1] + _dot(vts[u // 4], pt))

        _flash_units(4 * nb, ahead, qk, softmax, pv)
        return tuple(state)

    init = tuple((jnp.full((1, tq), NEG_INF, F32), jnp.zeros((rows, tq), F32)) for _ in range(4))
    state = lax.fori_loop(0, n_iter, step, init)
    for pair in range(2):
        a0, a1 = state[2 * pair][1], state[2 * pair + 1][1]
        yt = jnp.concatenate([a0[:HEAD_DIM] / a0[HEAD_DIM:HEAD_DIM + 1],
                              a1[:HEAD_DIM] / a1[HEAD_DIM:HEAD_DIM + 1]], axis=0)
        gate = g_ref[0, :, pair * LANES:(pair + 1) * LANES].astype(F32)
        o_ref[0, :, pair * LANES:(pair + 1) * LANES] = (yt.T * _silu(gate)).astype(o_ref.dtype)


def _grid_attn(qd3, kd3, vtd, u3, tq, tk, nb, ahead):
    bsz, s_len, _ = qd3.shape
    nk = s_len // tk
    rows = vtd.shape[-2]
    return pl.pallas_call(
        functools.partial(_grid_attn_kernel, tq=tq, tk=tk, nb=nb, n_iter=nk // nb, ahead=ahead),
        grid=(bsz, 2, s_len // tq),
        in_specs=[
            pl.BlockSpec((1, tq, 2 * LANES), lambda bi, h, i: (bi, i, h)),
            pl.BlockSpec((1, s_len, LANES), lambda bi, h, i: (bi, 0, 0)),
            pl.BlockSpec((1, 1, nk, rows, tk), lambda bi, h, i: (bi, h, 0, 0, 0)),
            pl.BlockSpec((1, tq, 2 * LANES), lambda bi, h, i: (bi, i, COL_DG // 2 + h)),
        ],
        out_specs=pl.BlockSpec((1, tq, 2 * LANES), lambda bi, h, i: (bi, i, h)),
        out_shape=jax.ShapeDtypeStruct((bsz, s_len, 4 * LANES), BF16),
        compiler_params=_params(("parallel", "parallel", "parallel")),
        name="grid_attn",
    )(qd3, kd3, vtd, u3)


def _norm_rope(x, gain, cos, sin, seg):
    w = x.shape[-1]
    x2 = x * x
    hi = x2.astype(BF16)
    lo = (x2 - hi.astype(F32)).astype(BF16)
    ss = _dot(hi, seg) + _dot(lo, seg)
    y = x * lax.rsqrt(ss * (1.0 / HEAD_DIM) + EPS) * gain
    quarter = HEAD_DIM // 4
    ahead = pltpu.roll(y, w - quarter, 1)
    behind = pltpu.roll(y, quarter, 1)
    first = (lax.broadcasted_iota(jnp.int32, (1, w), 1) % (2 * quarter)) < quarter
    rot = jnp.where(first, -ahead, behind)
    reps = w // LANES
    cos_w = jnp.concatenate([cos] * reps, axis=1)
    sin_w = jnp.concatenate([sin] * reps, axis=1)
    return y * cos_w + rot * sin_w


def _attn_prep_kernel(q_ref, k_ref, vc_ref, vd_ref, va_ref, cos_ref, sin_ref, gq_ref, gk_ref, seg_ref,
                      qo_ref, ko_ref, vtc_ref, vtd_ref, vta_ref):
    cos = cos_ref[...]
    sin = sin_ref[...]
    seg = seg_ref[...]
    q = _norm_rope(q_ref[...].astype(F32), gq_ref[...], cos, sin, seg)
    qo_ref[...] = (q * (SCALE * LOG2E)).astype(qo_ref.dtype)
    k = _norm_rope(k_ref[...].astype(F32), gk_ref[...], cos, sin, seg[:LANES, :LANES])
    ko_ref[...] = k.astype(ko_ref.dtype)
    tm = q.shape[0]
    ones = jnp.ones((VT_ONES, tm), BF16)
    vct = vc_ref[...].astype(F32).T
    for h in range(C_HEADS):
        vtc_ref[0, h, 0] = jnp.concatenate([vct[h * LANES:(h + 1) * LANES].astype(BF16), ones], axis=0)
    vdt = vd_ref[...].astype(F32).T.astype(BF16)
    vat = va_ref[...].astype(F32).T.astype(BF16)
    for h in range(2):
        vtd_ref[0, h, 0] = jnp.concatenate([vdt[h * HEAD_DIM:(h + 1) * HEAD_DIM], ones], axis=0)
        for kb in range(tm // A_BLOCK):
            cols = slice(kb * A_BLOCK, (kb + 1) * A_BLOCK)
            vta_ref[0, kb, h] = jnp.concatenate([vat[h * HEAD_DIM:(h + 1) * HEAD_DIM, cols], ones[:, cols]], axis=0)


def _attn_prep(u2, cos, sin, qnorm, knorm, bsz, s_len, tm):
    m = u2.shape[0]
    per_seq = s_len // tm
    per_tile = tm // A_BLOCK
    lane = jnp.arange(4 * LANES)
    seg = (lane[:, None] // HEAD_DIM == lane[None, :] // HEAD_DIM).astype(BF16)
    rows_c = LANES + VT_ONES
    return pl.pallas_call(
        _attn_prep_kernel,
        grid=(m // tm,),
        in_specs=[
            pl.BlockSpec((tm, 4 * LANES), lambda i: (i, COL_DQ // 4)),
            pl.BlockSpec((tm, LANES), lambda i: (i, COL_DK)),
            pl.BlockSpec((tm, 4 * LANES), lambda i: (i, COL_CV // 4)),
            pl.BlockSpec((tm, LANES), lambda i: (i, COL_DV)),
            pl.BlockSpec((tm, LANES), lambda i: (i, COL_AV)),
            pl.BlockSpec((tm, LANES), lambda i: (i % per_seq, 0)),
            pl.BlockSpec((tm, LANES), lambda i: (i % per_seq, 0)),
            pl.BlockSpec((1, 4 * LANES), lambda i: (0, 0)),
            pl.BlockSpec((1, LANES), lambda i: (0, 0)),
            pl.BlockSpec((4 * LANES, 4 * LANES), lambda i: (0, 0)),
        ],
        out_specs=[
            pl.BlockSpec((tm, 4 * LANES), lambda i: (i, 0)),
            pl.BlockSpec((tm, LANES), lambda i: (i, 0)),
            pl.BlockSpec((1, C_HEADS, 1, rows_c, tm), lambda i: (i // per_seq, 0, i % per_seq, 0, 0)),
            pl.BlockSpec((1, 2, 1, KV_ROWS, tm), lambda i: (i // per_seq, 0, i % per_seq, 0, 0)),
            pl.BlockSpec((1, per_tile, 2, KV_ROWS, A_BLOCK), lambda i: (i // per_seq, i % per_seq, 0, 0, 0)),
        ],
        out_shape=[jax.ShapeDtypeStruct((m, 4 * LANES), BF16),
                   jax.ShapeDtypeStruct((m, LANES), BF16),
                   jax.ShapeDtypeStruct((bsz, C_HEADS, per_seq, rows_c, tm), BF16),
                   jax.ShapeDtypeStruct((bsz, 2, per_seq, KV_ROWS, tm), BF16),
                   jax.ShapeDtypeStruct((bsz, s_len // A_BLOCK, 2, KV_ROWS, A_BLOCK), BF16)],
        compiler_params=_params(("parallel",)),
        name="attn_prep",
    )(u2, u2, u2, u2, u2, cos, sin, jnp.tile(qnorm, 8).reshape(1, 4 * LANES),
      jnp.tile(knorm, 2).reshape(1, LANES), seg)


def _out_kernel(x_ref, ya_ref, yb_ref, yc_ref, yd_ref, pe_ref, wo_ref, gp_ref, wpe_ref, wpg_ref, o_ref):
    width = 4 * LANES
    z = _dot(ya_ref[...], wo_ref[0 * width:1 * width, :])
    z += _dot(yb_ref[...], wo_ref[1 * width:2 * width, :])
    z += _dot(yc_ref[...], wo_ref[2 * width:3 * width, :])
    z += _dot(yd_ref[...], wo_ref[3 * width:4 * width, :])
    ms = jnp.mean(z * z, axis=-1, keepdims=True)
    x = x_ref[...] + z * lax.rsqrt(ms + EPS) * gp_ref[...]
    logits = _dot(x.astype(BF16), wpg_ref[...])
    gate = 1.0 / (1.0 + jnp.exp(-logits))
    pe = _dot(pe_ref[...].astype(BF16), wpe_ref[...])
    o_ref[...] = x + pe * gate


def _out_stage(x2, ya, yb, yc, yd, pe2, w_o, g_post, w_pe, w_pg, tm):
    m = x2.shape[0]
    row = lambda w: pl.BlockSpec((tm, w), lambda i: (i, 0))
    const = lambda r, c: pl.BlockSpec((r, c), lambda i: (0, 0), pipeline_mode=pl.Buffered(1))
    return pl.pallas_call(
        _out_kernel,
        grid=(m // tm,),
        in_specs=[row(D_MODEL), row(4 * LANES), row(4 * LANES), row(4 * LANES), row(4 * LANES),
                  row(PLE_DIM), const(MIX_WIDTH, D_MODEL), const(1, D_MODEL),
                  const(PLE_DIM, D_MODEL), const(D_MODEL, D_MODEL)],
        out_specs=row(D_MODEL),
        out_shape=jax.ShapeDtypeStruct((m, D_MODEL), F32),
        compiler_params=_params(("parallel",)),
        name="out_stage",
    )(x2, ya, yb, yc, yd, pe2, w_o, g_post.reshape(1, D_MODEL), w_pe, w_pg)


def _prep_w_in(w):
    parts, start = [], 0
    for n in IN_SIZES:
        parts.append(w[:, start:start + n])
        start += n
    aq, ak, av, ag, bx, bg, cq, ck, cv, cg, dq, dk, dv, dg = parts

    log2_scale = SCALE * LOG2E
    cols = [aq * log2_scale, ag, bx, bg, cq * log2_scale, ck, cv, cg, dq, dg, ak, av, dk, dv]
    return jnp.concatenate(cols, axis=1).astype(BF16)


def _rel_bucket(rel):
    nb = REL_BUCKETS // 2
    max_exact = nb // 2
    ret = jnp.where(rel > 0, nb, 0)
    n = jnp.abs(rel)
    nf = jnp.maximum(n, 1).astype(F32)
    large = max_exact + (jnp.log(nf / max_exact) / math.log(REL_MAX_DIST / max_exact)
                         * (nb - max_exact)).astype(jnp.int32)
    large = jnp.minimum(large, nb - 1)
    return ret + jnp.where(n < max_exact, n, large)


def _bucket_values(tab, rel):
    bucket = _rel_bucket(rel)[None]
    out = jnp.zeros((tab.shape[1],) + rel.shape, F32)
    for b in range(REL_BUCKETS):
        out = jnp.where(bucket == b, tab[b].reshape((-1,) + (1,) * rel.ndim), out)
    return out


def _bias_tables(rel_bias, tq, tk):
    tab = rel_bias.astype(F32)
    ck = jnp.arange(3 * A_BLOCK)[:, None]
    rq = jnp.arange(A_BLOCK)[None, :]
    rel_a = ck - A_BLOCK - rq
    vals = _bucket_values(tab[:, :A_HEADS], rel_a) * LOG2E
    in_band = jnp.abs(rel_a) <= A_WINDOW
    variants = []
    for edge in (ck >= A_BLOCK, ck >= 0, ck < 2 * A_BLOCK):
        masked = jnp.where(in_band & edge, vals, NEG_INF)
        variants.append(jnp.transpose(masked.reshape(2, 4, 3 * A_BLOCK, A_BLOCK), (0, 2, 1, 3))
                        .reshape(2, 3 * A_BLOCK, 4 * A_BLOCK))
    bias_a = jnp.stack(variants)
    assert tk >= REL_MAX_DIST
    n_tab = tq // tk + 4
    d = jnp.arange(n_tab)[:, None, None]
    c = jnp.arange(tk)[None, :, None]
    r = jnp.arange(tq)[None, None, :]
    bias_c = _bucket_values(tab[:, A_HEADS:], c + (d - 2) * tk - r) * LOG2E
    return bias_a, bias_c


def _axial_rope(s_len):
    rows = s_len // GRID_W
    t_row = jnp.repeat(jnp.arange(rows), GRID_W).astype(F32)
    t_col = jnp.tile(jnp.arange(GRID_W), rows).astype(F32)
    half = HEAD_DIM // 2
    inv = ROPE_THETA ** (-jnp.arange(0, half, 2, dtype=F32) / half)
    ang_r = t_row[:, None] * inv[None, :]
    ang_c = t_col[:, None] * inv[None, :]
    ang = jnp.concatenate([ang_r, ang_r, ang_c, ang_c] * 2, axis=-1)
    return jnp.cos(ang), jnp.sin(ang)


def _tiles(s_len):
    return dict(tm_in=min(512, s_len), tn_in=U_WIDTH // 2, tq_pool=min(256, s_len), tm_out=min(256, s_len),
                tq_attn=512, tk_attn=256, nb_c=16, nb_d=16, ahead=3, tq_win=512, ahead_win=1)


def _layer(x, pe, layer_idx, tables, rope, w, tiles):
    bsz, s_len, _ = x.shape
    m = bsz * s_len
    bias_a, bias_c = tables
    cos, sin = rope
    tq, tk, ahead = tiles["tq_attn"], tiles["tk_attn"], tiles["ahead"]
    x2 = x.reshape(m, D_MODEL)
    u2 = _in_proj(x2, w["g_pre"], w["w_in"], tiles["tm_in"], tiles["tn_in"])
    u3 = u2.reshape(bsz, s_len, U_WIDTH)
    qd, kd, vtc, vtd, vta = _attn_prep(u2, cos, sin, w["qnorm_d"], w["knorm_d"], bsz, s_len, tk)
    ya = _win_attn(u3, vta, bias_a, w["sink"], tiles["tq_win"], tiles["ahead_win"])
    yb = _pool(u3, w["pool_w"], w["pool_scale"], tiles["tq_pool"])
    lambda_init = 0.8 - 0.6 * math.exp(-0.3 * layer_idx)
    yc = _diff_attn(u3, vtc, bias_c, w["lam"], w["diff_subln"], lambda_init, tq, tk, tiles["nb_c"], ahead)
    yd = _grid_attn(qd.reshape(bsz, s_len, -1), kd.reshape(bsz, s_len, -1), vtd, u3, tq, tk,
                    tiles["nb_d"], ahead)
    flat = lambda y: y.reshape(m, 4 * LANES)
    out = _out_stage(x2, flat(ya), flat(yb), flat(yc), flat(yd), pe.reshape(m, PLE_DIM),
                     w["w_o"], w["g_post"], w["w_pe"], w["w_pg"], tiles["tm_out"])
    return out.reshape(bsz, s_len, D_MODEL)


def _trunk(x, p, rel_bias, layers, tiles=None):
    s_len = x.shape[1]
    tiles = tiles or _tiles(s_len)
    tables = _bias_tables(rel_bias, tiles["tq_attn"], tiles["tk_attn"])
    rope = _axial_rope(s_len)
    for i, w in enumerate(layers):
        x = _layer(x, p[i], i, tables, rope, w, tiles)
    return x


def _prep_layers(w_in, w_o, g_pre, g_post, sink_a, pool_w, pool_scale, lam_q1, lam_k1, lam_q2, lam_k2,
                 diff_subln, qnorm_d, knorm_d, w_pe, w_pg):
    layers = []
    for i in range(w_in.shape[0]):
        layers.append(dict(
            w_in=_prep_w_in(w_in[i]), w_o=w_o[i].astype(BF16), g_pre=g_pre[i], g_post=g_post[i],
            sink=jnp.repeat(sink_a[i].astype(F32) * LOG2E, A_BLOCK).reshape(2, 1, 4 * A_BLOCK),
            pool_w=pool_w[i], pool_scale=pool_scale[i],
            lam=jnp.stack([lam_q1[i], lam_k1[i], lam_q2[i], lam_k2[i]]).astype(F32),
            diff_subln=diff_subln[i], qnorm_d=qnorm_d[i], knorm_d=knorm_d[i],
            w_pe=w_pe[i].astype(BF16), w_pg=w_pg[i].astype(BF16)))
    return layers


def kernel(x_prompt, x_sample, p_prompt, p_sample, w_in, w_o, g_pre, g_post, sink_a, pool_w, pool_scale,
           lam_q1, lam_k1, lam_q2, lam_k2, diff_subln, qnorm_d, knorm_d, rel_bias, w_pe, w_pg):
    layers = _prep_layers(w_in, w_o, g_pre, g_post, sink_a, pool_w, pool_scale, lam_q1, lam_k1, lam_q2,
                          lam_k2, diff_subln, qnorm_d, knorm_d, w_pe, w_pg)
    y_prompt = _trunk(x_prompt, p_prompt, rel_bias, layers)
    y_sample = _trunk(x_sample, p_sample, rel_bias, layers)
    return (y_prompt, y_sample)
```

```python
import functools
import math

import jax
import jax.numpy as jnp
from jax import lax
from jax.experimental import pallas as pl
from jax.experimental.pallas import tpu as pltpu

F32 = jnp.float32
BF16 = jnp.bfloat16

D_MODEL = 2048
HEAD_DIM = 64
LANES = 128
PLE_DIM = 256
EPS = 1e-6
NEG_INF = -1e30
SCALE = HEAD_DIM ** -0.5
LOG2E = math.log2(math.e)
VT_ONES = 16
KV_ROWS = HEAD_DIM + VT_ONES
BOUND_SLACK = 1.02
MAX_EXP2_SPAN = 100.0

A_HEADS = 8
A_WINDOW = 128
A_BLOCK = 128
B_GROUPS = 4
B_POOL_SIZES = (2, 4, 8, 16)
B_HALO = 64
C_HEADS = 4
REL_BUCKETS = 32
REL_MAX_DIST = 128
ROPE_THETA = 10000.0
GRID_W = 64

IN_SIZES = (512, 128, 128, 512, 512, 512, 512, 512, 512, 512, 512, 128, 128, 512)
MIX_WIDTH = 2048

U_WIDTH = 5632
COL_AQ, COL_AG, COL_BX, COL_BG = 0, 4, 8, 12
COL_CQ, COL_CK, COL_CV, COL_CG = 16, 20, 24, 28
COL_DQ, COL_DG = 32, 36
COL_AK, COL_AV, COL_DK, COL_DV = 40, 41, 42, 43

VMEM_LIMIT = 56 * 1024 * 1024


def _params(sem, vmem=VMEM_LIMIT):
    return pltpu.CompilerParams(dimension_semantics=sem, vmem_limit_bytes=vmem)


def _dot(a, b):
    return jnp.dot(a, b, preferred_element_type=F32)


def _dot_nt(a, b):
    return lax.dot_general(a, b, (((1,), (1,)), ((), ())), preferred_element_type=F32)


def _silu(x):
    return x / (1.0 + jnp.exp(-x))


def _lane_is_low(width=LANES):
    return (lax.broadcasted_iota(jnp.int32, (1, width), 1) % LANES) < HEAD_DIM


def _in_proj_kernel(x_ref, g_ref, w_ref, o_ref, h_ref):
    @pl.when(pl.program_id(1) == 0)
    def _():
        x = x_ref[...]
        ms = jnp.mean(x * x, axis=-1, keepdims=True)
        h_ref[...] = (x * lax.rsqrt(ms + EPS) * g_ref[...]).astype(BF16)

    o_ref[...] = _dot(h_ref[...], w_ref[...]).astype(o_ref.dtype)


def _in_proj(x2, g_pre, w_in_p, tm, tn):
    m = x2.shape[0]
    return pl.pallas_call(
        _in_proj_kernel,
        grid=(m // tm, U_WIDTH // tn),
        in_specs=[
            pl.BlockSpec((tm, D_MODEL), lambda i, j: (i, 0)),
            pl.BlockSpec((1, D_MODEL), lambda i, j: (0, 0)),
            pl.BlockSpec((D_MODEL, tn), lambda i, j: (0, j)),
        ],
        out_specs=pl.BlockSpec((tm, tn), lambda i, j: (i, j)),
        out_shape=jax.ShapeDtypeStruct((m, U_WIDTH), BF16),
        scratch_shapes=[pltpu.VMEM((tm, D_MODEL), BF16)],
        compiler_params=_params(("parallel", "arbitrary")),
        name="in_proj",
    )(x2, g_pre.reshape(1, D_MODEL), w_in_p)


def _group_queries(x, group, low):
    xr = pltpu.roll(x, HEAD_DIM, 1)
    zero = jnp.zeros_like(x)
    if group == 0:
        return [jnp.where(low, x, zero), jnp.where(low, xr, zero)]
    return [jnp.where(low, zero, xr), jnp.where(low, zero, x)]


def _win_attn_kernel(q_ref, kp_ref, kc_ref, kn_ref, vp_ref, vc_ref, vn_ref, g_ref, bias_ref,
                     sink_ref, o_ref, *, nq, ahead):
    i = pl.program_id(1)
    n = pl.num_programs(1)
    low = _lane_is_low()
    kext = jnp.concatenate([kp_ref[0], kc_ref[0], kn_ref[0]], axis=0)
    vts = [vp_ref[0, 0]] + [vc_ref[0, b] for b in range(nq)] + [vn_ref[0, 0]]
    units = [(qb, grp) for qb in range(nq) for grp in range(2)]

    def qk(u):
        qb, grp = units[u]
        rows = slice(qb * A_BLOCK, (qb + 1) * A_BLOCK)
        heads = []
        for pair in (2 * grp, 2 * grp + 1):
            heads += _group_queries(q_ref[0, rows, pair * LANES:(pair + 1) * LANES], grp, low)
        return _dot_nt(kext[qb * A_BLOCK:(qb + 3) * A_BLOCK], jnp.concatenate(heads, axis=0))

    def softmax(u, st):
        qb, grp = units[u]
        variant = 1
        if qb == 0:
            variant = jnp.where(i == 0, 0, variant)
        if qb == nq - 1:
            variant = jnp.where(i == n - 1, 2, variant)
        sb = _same_bits(st) + bias_ref[variant, grp]
        m = jnp.maximum(jnp.max(sb, axis=0, keepdims=True), sink_ref[grp])
        return m, jnp.exp2(sb - m).astype(BF16)

    def pv(u, probs):
        qb, grp = units[u]
        m, pt = probs
        vt = jnp.concatenate([vts[qb + d][grp] for d in range(3)], axis=1)
        acc = _dot(vt, pt)
        ot = acc[:HEAD_DIM] / (acc[HEAD_DIM:HEAD_DIM + 1] + jnp.exp2(sink_ref[grp] - m))
        rows = slice(qb * A_BLOCK, (qb + 1) * A_BLOCK)
        for pair in range(2):
            yt = jnp.concatenate([ot[:, (2 * pair) * A_BLOCK:(2 * pair + 1) * A_BLOCK],
                                  ot[:, (2 * pair + 1) * A_BLOCK:(2 * pair + 2) * A_BLOCK]], axis=0)
            cols = slice((2 * grp + pair) * LANES, (2 * grp + pair + 1) * LANES)
            gate = g_ref[0, rows, cols].astype(F32)
            o_ref[0, rows, cols] = (yt.T * _silu(gate)).astype(o_ref.dtype)

    _flash_units(len(units), ahead, qk, softmax, pv)


def _win_attn(u3, vta, bias_a, sink_a, tq, ahead):
    bsz, s_len, _ = u3.shape
    nq = tq // A_BLOCK
    nblk = s_len // A_BLOCK
    wide = (1, tq, 4 * LANES)
    const = lambda shape: pl.BlockSpec(shape, lambda bi, i: (0,) * len(shape), pipeline_mode=pl.Buffered(1))
    return pl.pallas_call(
        functools.partial(_win_attn_kernel, nq=nq, ahead=ahead),
        grid=(bsz, s_len // tq),
        in_specs=[
            pl.BlockSpec(wide, lambda bi, i: (bi, i, COL_AQ // 4)),
            pl.BlockSpec((1, A_BLOCK, LANES), lambda bi, i: (bi, jnp.maximum(i * nq - 1, 0), COL_AK)),
            pl.BlockSpec((1, tq, LANES), lambda bi, i: (bi, i, COL_AK)),
            pl.BlockSpec((1, A_BLOCK, LANES), lambda bi, i: (bi, jnp.minimum((i + 1) * nq, nblk - 1), COL_AK)),
            pl.BlockSpec((1, 1, 2, KV_ROWS, A_BLOCK), lambda bi, i: (bi, jnp.maximum(i * nq - 1, 0), 0, 0, 0)),
            pl.BlockSpec((1, nq, 2, KV_ROWS, A_BLOCK), lambda bi, i: (bi, i, 0, 0, 0)),
            pl.BlockSpec((1, 1, 2, KV_ROWS, A_BLOCK),
                         lambda bi, i: (bi, jnp.minimum((i + 1) * nq, nblk - 1), 0, 0, 0)),
            pl.BlockSpec(wide, lambda bi, i: (bi, i, COL_AG // 4)),
            const(bias_a.shape),
            const(sink_a.shape),
        ],
        out_specs=pl.BlockSpec(wide, lambda bi, i: (bi, i, 0)),
        out_shape=jax.ShapeDtypeStruct((bsz, s_len, 4 * LANES), BF16),
        compiler_params=_params(("parallel", "parallel")),
        name="win_attn",
    )(u3, u3, u3, u3, vta, vta, vta, u3, bias_a, sink_a)


def _pool_kernel(xp_ref, xc_ref, xn_ref, g_ref, band_ref, w_ref, sc_ref, o_ref, *, s_len, tq):
    i = pl.program_id(1)
    n = pl.num_programs(1)
    xp = xp_ref[0]
    xn = xn_ref[0]
    xp = jnp.where(i > 0, xp, jnp.zeros_like(xp))
    xn = jnp.where(i < n - 1, xn, jnp.zeros_like(xn))
    xc = xc_ref[0]
    xext = jnp.concatenate([xp, xc, xn], axis=0)
    t = i * tq + lax.broadcasted_iota(jnp.int32, (tq, 1), 0)
    for gi, size in enumerate(B_POOL_SIZES):
        lo = size // 2
        hi = size - lo - 1
        lanes = slice(gi * LANES, (gi + 1) * LANES)
        win_sum = _dot(band_ref[gi], xext[:, lanes])
        start = jnp.maximum(t - lo, 0)
        end = jnp.minimum(t + hi, s_len - 1) + 1
        pooled = win_sum / (end - start).astype(F32) - xc[:, lanes].astype(F32)
        y = _dot(pooled.astype(BF16), w_ref[gi]) * sc_ref[:, lanes]
        gate = g_ref[0, :, lanes].astype(F32)
        o_ref[0, :, lanes] = (y * _silu(gate)).astype(o_ref.dtype)


def _pool_bands(tq):
    r = jnp.arange(tq)[:, None]
    c = jnp.arange(tq + 2 * B_HALO)[None, :] - B_HALO
    bands = []
    for size in B_POOL_SIZES:
        lo = size // 2
        hi = size - lo - 1
        bands.append(((c - r >= -lo) & (c - r <= hi)).astype(BF16))
    return jnp.stack(bands)


def _pool(u3, pool_w, pool_scale, tq):
    bsz, s_len, _ = u3.shape
    nt = s_len // tq
    per = tq // B_HALO
    nh = s_len // B_HALO
    wide = (1, tq, 4 * LANES)
    halo = (1, B_HALO, 4 * LANES)
    return pl.pallas_call(
        functools.partial(_pool_kernel, s_len=s_len, tq=tq),
        grid=(bsz, nt),
        in_specs=[
            pl.BlockSpec(halo, lambda bi, i: (bi, jnp.maximum(i * per - 1, 0), COL_BX // 4)),
            pl.BlockSpec(wide, lambda bi, i: (bi, i, COL_BX // 4)),
            pl.BlockSpec(halo, lambda bi, i: (bi, jnp.minimum((i + 1) * per, nh - 1), COL_BX // 4)),
            pl.BlockSpec(wide, lambda bi, i: (bi, i, COL_BG // 4)),
            pl.BlockSpec((B_GROUPS, tq, tq + 2 * B_HALO), lambda bi, i: (0, 0, 0)),
            pl.BlockSpec((B_GROUPS, LANES, LANES), lambda bi, i: (0, 0, 0)),
            pl.BlockSpec((1, 4 * LANES), lambda bi, i: (0, 0)),
        ],
        out_specs=pl.BlockSpec(wide, lambda bi, i: (bi, i, 0)),
        out_shape=jax.ShapeDtypeStruct((bsz, s_len, 4 * LANES), BF16),
        compiler_params=_params(("parallel", "parallel")),
        name="pool",
    )(u3, u3, u3, u3, _pool_bands(tq), pool_w.astype(BF16), pool_scale.reshape(1, 4 * LANES))


def _softmax_t(st, m):
    m_new = jnp.maximum(m, jnp.max(st, axis=0, keepdims=True))
    return m_new, jnp.exp2(m - m_new), jnp.exp2(st - m_new).astype(BF16)


def _flash_units(n_units, ahead, qk, softmax, pv):
    scores = {u: qk(u) for u in range(min(ahead, n_units))}
    for u in range(n_units):
        probs = softmax(u, scores.pop(u))
        if u + ahead < n_units:
            scores[u + ahead] = qk(u + ahead)
        pv(u, probs)


def _same_bits(x):
    return lax.bitcast_convert_type(lax.bitcast_convert_type(x, jnp.int32), F32)


def _masked_halves(q, low):
    zero = jnp.zeros_like(q)
    return jnp.where(low, q, zero), jnp.where(low, zero, q)


def _diff_attn_kernel(q_ref, k_ref, vt_ref, g_ref, bias_ref, stat_ref, lam_ref, sub_ref, o_ref,
                      *, tq, tk, nb, n_iter, ahead, lambda_init, bounded):
    i = pl.program_id(2)
    ratio = tq // tk
    n_tab = bias_ref.shape[1]
    rows = vt_ref.shape[-2]
    low = _lane_is_low()
    q = q_ref[0]
    qs = _masked_halves(q, low)
    bounds = None
    if bounded:
        qf = q.astype(F32)
        halves = jnp.concatenate([jnp.where(low, 1.0, 0.0), jnp.where(low, 0.0, 1.0)] +
                                 [jnp.zeros((1, LANES), F32)] * 6, axis=0).astype(BF16)
        norm2 = _dot_nt(halves, (qf * qf).astype(BF16))
        bounds = [jnp.sqrt(norm2[h:h + 1]) * stat_ref[0, 0, h:h + 1, 0:1] + stat_ref[0, 0, 2:3, 0:1]
                  for h in range(2)]

    def step(j, carry):
        state = list(carry)
        blocks = [nb * j + b for b in range(nb)]
        ks = [k_ref[0, pl.ds(pl.multiple_of(blk * tk, tk), tk), :] for blk in blocks]
        vts = [vt_ref[0, 0, blk] for blk in blocks]
        tabs = [jnp.clip(blk - ratio * i + 2, 0, n_tab - 1) for blk in blocks]

        def qk(u):
            return _dot_nt(ks[u // 2], qs[u % 2])

        def softmax(u, st):
            sb = _same_bits(st) + bias_ref[0, tabs[u // 2]]
            if bounded:
                return None, None, jnp.exp2(sb - bounds[u % 2]).astype(BF16)
            return _softmax_t(sb, state[u % 2][0])

        def pv(u, probs):
            m_new, alpha, pt = probs
            acc = state[u % 2][1]
            state[u % 2] = (m_new, (acc if bounded else alpha * acc) + _dot(vts[u // 2], pt))

        _flash_units(2 * nb, ahead, qk, softmax, pv)
        return tuple(state)

    m0 = None if bounded else jnp.full((1, tq), NEG_INF, F32)
    init = tuple((m0, jnp.zeros((rows, tq), F32)) for _ in range(2))
    (_, a1), (_, a2) = lax.fori_loop(0, n_iter, step, init)

    dot1 = jnp.sum(lam_ref[0:1, :] * lam_ref[1:2, :], axis=-1, keepdims=True)
    dot2 = jnp.sum(lam_ref[2:3, :] * lam_ref[3:4, :], axis=-1, keepdims=True)
    lam = jnp.exp(dot1) - jnp.exp(dot2) + lambda_init
    yt = a1[:LANES] / a1[LANES:LANES + 1] - lam * (a2[:LANES] / a2[LANES:LANES + 1])
    y = yt.T
    ms = jnp.mean(y * y, axis=-1, keepdims=True)
    y = y * lax.rsqrt(ms + EPS) * sub_ref[...] * (1.0 - lambda_init)
    o_ref[0] = (y * _silu(g_ref[0].astype(F32))).astype(o_ref.dtype)


def _diff_attn(u3, vtc, bias_c, stat_c, lam_vecs, subln, lambda_init, tq, tk, nb, ahead, bounded):
    bsz, s_len, _ = u3.shape
    nk = s_len // tk
    rows = vtc.shape[-2]
    n_tab = bias_c.shape[1]
    return pl.pallas_call(
        functools.partial(_diff_attn_kernel, tq=tq, tk=tk, nb=nb, n_iter=nk // nb, ahead=ahead,
                          lambda_init=lambda_init, bounded=bounded),
        grid=(bsz, C_HEADS, s_len // tq),
        in_specs=[
            pl.BlockSpec((1, tq, LANES), lambda bi, h, i: (bi, i, COL_CQ + h)),
            pl.BlockSpec((1, s_len, LANES), lambda bi, h, i: (bi, 0, COL_CK + h)),
            pl.BlockSpec((1, 1, nk, rows, tk), lambda bi, h, i: (bi, h, 0, 0, 0)),
            pl.BlockSpec((1, tq, LANES), lambda bi, h, i: (bi, i, COL_CG + h)),
            pl.BlockSpec((1, n_tab, tk, tq), lambda bi, h, i: (h, 0, 0, 0)),
            pl.BlockSpec((1, 1, 8, LANES), lambda bi, h, i: (bi, h, 0, 0)),
            pl.BlockSpec((4, HEAD_DIM), lambda bi, h, i: (0, 0)),
            pl.BlockSpec((1, LANES), lambda bi, h, i: (0, 0)),
        ],
        out_specs=pl.BlockSpec((1, tq, LANES), lambda bi, h, i: (bi, i, h)),
        out_shape=jax.ShapeDtypeStruct((bsz, s_len, 4 * LANES), BF16),
        compiler_params=_params(("parallel", "parallel", "parallel")),
        name="diff_attn_bounded" if bounded else "diff_attn",
    )(u3, u3, vtc, u3, bias_c, stat_c, lam_vecs, subln.reshape(1, LANES))


def _grid_attn_kernel(q_ref, k_ref, vt_ref, g_ref, bound_ref, o_ref, *, tq, tk, nb, n_iter, ahead, bounded):
    grp = pl.program_id(1)
    lane = lax.broadcasted_iota(jnp.int32, (1, LANES), 1)
    keep = (lane >= grp * HEAD_DIM) & (lane < (grp + 1) * HEAD_DIM)
    rows = vt_ref.shape[-2]
    bound = bound_ref[0:1, 0:1]
    qs = []
    for pair in range(2):
        x = q_ref[0, :, pair * LANES:(pair + 1) * LANES]
        xr = pltpu.roll(x, HEAD_DIM, 1)
        zero = jnp.zeros_like(x)
        qs.append(jnp.where(keep, jnp.where(grp == 0, x, xr), zero))
        qs.append(jnp.where(keep, jnp.where(grp == 0, xr, x), zero))

    def step(j, carry):
        state = list(carry)
        blocks = [nb * j + b for b in range(nb)]
        ks = [k_ref[0, pl.ds(pl.multiple_of(blk * tk, tk), tk), :] for blk in blocks]
        vts = [vt_ref[0, 0, blk] for blk in blocks]

        def qk(u):
            return _dot_nt(ks[u // 4], qs[u % 4])

        def softmax(u, st):
            if bounded:
                return None, None, jnp.exp2(st - bound).astype(BF16)
            return _softmax_t(st, state[u % 4][0])

        def pv(u, probs):
            m_new, alpha, pt = probs
            acc = state[u % 4][1]
            state[u % 4] = (m_new, (acc if bounded else alpha * acc) + _dot(vts[u // 4], pt))

        _flash_units(4 * nb, ahead, qk, softmax, pv)
        return tuple(state)

    m0 = None if bounded else jnp.full((1, tq), NEG_INF, F32)
    init = tuple((m0, jnp.zeros((rows, tq), F32)) for _ in range(4))
    state = lax.fori_loop(0, n_iter, step, init)
    for pair in range(2):
        a0, a1 = state[2 * pair][1], state[2 * pair + 1][1]
        yt = jnp.concatenate([a0[:HEAD_DIM] / a0[HEAD_DIM:HEAD_DIM + 1],
                              a1[:HEAD_DIM] / a1[HEAD_DIM:HEAD_DIM + 1]], axis=0)
        gate = g_ref[0, :, pair * LANES:(pair + 1) * LANES].astype(F32)
        o_ref[0, :, pair * LANES:(pair + 1) * LANES] = (yt.T * _silu(gate)).astype(o_ref.dtype)


def _grid_attn(qd3, kd3, vtd, u3, bound_d, tq, tk, nb, ahead, bounded):
    bsz, s_len, _ = qd3.shape
    nk = s_len // tk
    rows = vtd.shape[-2]
    return pl.pallas_call(
        functools.partial(_grid_attn_kernel, tq=tq, tk=tk, nb=nb, n_iter=nk // nb, ahead=ahead,
                          bounded=bounded),
        grid=(bsz, 2, s_len // tq),
        in_specs=[
            pl.BlockSpec((1, tq, 2 * LANES), lambda bi, h, i: (bi, i, h)),
            pl.BlockSpec((1, s_len, LANES), lambda bi, h, i: (bi, 0, 0)),
            pl.BlockSpec((1, 1, nk, rows, tk), lambda bi, h, i: (bi, h, 0, 0, 0)),
            pl.BlockSpec((1, tq, 2 * LANES), lambda bi, h, i: (bi, i, COL_DG // 2 + h)),
            pl.BlockSpec((8, LANES), lambda bi, h, i: (0, 0)),
        ],
        out_specs=pl.BlockSpec((1, tq, 2 * LANES), lambda bi, h, i: (bi, i, h)),
        out_shape=jax.ShapeDtypeStruct((bsz, s_len, 4 * LANES), BF16),
        compiler_params=_params(("parallel", "parallel", "parallel")),
        name="grid_attn_bounded" if bounded else "grid_attn",
    )(qd3, kd3, vtd, u3, bound_d)


def _norm_rope(x, gain, cos, sin, seg):
    w = x.shape[-1]
    x2 = x * x
    hi = x2.astype(BF16)
    lo = (x2 - hi.astype(F32)).astype(BF16)
    ss = _dot(hi, seg) + _dot(lo, seg)
    y = x * lax.rsqrt(ss * (1.0 / HEAD_DIM) + EPS) * gain
    quarter = HEAD_DIM // 4
    ahead = pltpu.roll(y, w - quarter, 1)
    behind = pltpu.roll(y, quarter, 1)
    first = (lax.broadcasted_iota(jnp.int32, (1, w), 1) % (2 * quarter)) < quarter
    rot = jnp.where(first, -ahead, behind)
    reps = w // LANES
    cos_w = jnp.concatenate([cos] * reps, axis=1)
    sin_w = jnp.concatenate([sin] * reps, axis=1)
    return y * cos_w + rot * sin_w


def _attn_prep_kernel(q_ref, k_ref, vc_ref, vd_ref, va_ref, qc_ref, kc_ref, cos_ref, sin_ref, gq_ref, gk_ref,
                      seg_ref, qo_ref, ko_ref, vtc_ref, vtd_ref, vta_ref, qn_ref, kn_ref):
    cos = cos_ref[...]
    sin = sin_ref[...]
    seg = seg_ref[...]
    q = _norm_rope(q_ref[...].astype(F32), gq_ref[...], cos, sin, seg)
    qo_ref[...] = (q * (SCALE * LOG2E)).astype(qo_ref.dtype)
    k = _norm_rope(k_ref[...].astype(F32), gk_ref[...], cos, sin, seg[:LANES, :LANES])
    ko_ref[...] = k.astype(ko_ref.dtype)
    tm = q.shape[0]
    ones = jnp.ones((VT_ONES, tm), BF16)
    vct = vc_ref[...].astype(F32).T
    for h in range(C_HEADS):
        vtc_ref[0, h, 0] = jnp.concatenate([vct[h * LANES:(h + 1) * LANES].astype(BF16), ones], axis=0)
    vdt = vd_ref[...].astype(F32).T.astype(BF16)
    vat = va_ref[...].astype(F32).T.astype(BF16)
    for h in range(2):
        vtd_ref[0, h, 0] = jnp.concatenate([vdt[h * HEAD_DIM:(h + 1) * HEAD_DIM], ones], axis=0)
        for kb in range(tm // A_BLOCK):
            cols = slice(kb * A_BLOCK, (kb + 1) * A_BLOCK)
            vta_ref[0, kb, h] = jnp.concatenate([vat[h * HEAD_DIM:(h + 1) * HEAD_DIM, cols], ones[:, cols]], axis=0)
    for src, dst in ((qc_ref, qn_ref), (kc_ref, kn_ref)):
        x = src[...].astype(F32)
        seg_sums = _dot((x * x).astype(BF16), seg)
        dst[0] = jnp.broadcast_to(jnp.max(seg_sums, axis=0, keepdims=True), (8, 4 * LANES))


def _attn_prep(u2, cos, sin, qnorm, knorm, bsz, s_len, tm):
    m = u2.shape[0]
    per_seq = s_len // tm
    per_tile = tm // A_BLOCK
    lane = jnp.arange(4 * LANES)
    seg = (lane[:, None] // HEAD_DIM == lane[None, :] // HEAD_DIM).astype(BF16)
    rows_c = LANES + VT_ONES
    return pl.pallas_call(
        _attn_prep_kernel,
        grid=(m // tm,),
        in_specs=[
            pl.BlockSpec((tm, 4 * LANES), lambda i: (i, COL_DQ // 4)),
            pl.BlockSpec((tm, LANES), lambda i: (i, COL_DK)),
            pl.BlockSpec((tm, 4 * LANES), lambda i: (i, COL_CV // 4)),
            pl.BlockSpec((tm, LANES), lambda i: (i, COL_DV)),
            pl.BlockSpec((tm, LANES), lambda i: (i, COL_AV)),
            pl.BlockSpec((tm, 4 * LANES), lambda i: (i, COL_CQ // 4)),
            pl.BlockSpec((tm, 4 * LANES), lambda i: (i, COL_CK // 4)),
            pl.BlockSpec((tm, LANES), lambda i: (i % per_seq, 0)),
            pl.BlockSpec((tm, LANES), lambda i: (i % per_seq, 0)),
            pl.BlockSpec((1, 4 * LANES), lambda i: (0, 0)),
            pl.BlockSpec((1, LANES), lambda i: (0, 0)),
            pl.BlockSpec((4 * LANES, 4 * LANES), lambda i: (0, 0)),
        ],
        out_specs=[
            pl.BlockSpec((tm, 4 * LANES), lambda i: (i, 0)),
            pl.BlockSpec((tm, LANES), lambda i: (i, 0)),
            pl.BlockSpec((1, C_HEADS, 1, rows_c, tm), lambda i: (i // per_seq, 0, i % per_seq, 0, 0)),
            pl.BlockSpec((1, 2, 1, KV_ROWS, tm), lambda i: (i // per_seq, 0, i % per_seq, 0, 0)),
            pl.BlockSpec((1, per_tile, 2, KV_ROWS, A_BLOCK), lambda i: (i // per_seq, i % per_seq, 0, 0, 0)),
            pl.BlockSpec((1, 8, 4 * LANES), lambda i: (i, 0, 0)),
            pl.BlockSpec((1, 8, 4 * LANES), lambda i: (i, 0, 0)),
        ],
        out_shape=[jax.ShapeDtypeStruct((m, 4 * LANES), BF16),
                   jax.ShapeDtypeStruct((m, LANES), BF16),
                   jax.ShapeDtypeStruct((bsz, C_HEADS, per_seq, rows_c, tm), BF16),
                   jax.ShapeDtypeStruct((bsz, 2, per_seq, KV_ROWS, tm), BF16),
                   jax.ShapeDtypeStruct((bsz, s_len // A_BLOCK, 2, KV_ROWS, A_BLOCK), BF16),
                   jax.ShapeDtypeStruct((m // tm, 8, 4 * LANES), F32),
                   jax.ShapeDtypeStruct((m // tm, 8, 4 * LANES), F32)],
        compiler_params=_params(("parallel",)),
        name="attn_prep",
    )(u2, u2, u2, u2, u2, u2, u2, cos, sin, jnp.tile(qnorm, 8).reshape(1, 4 * LANES),
      jnp.tile(knorm, 2).reshape(1, LANES), seg)


def _out_kernel(x_ref, ya_ref, yb_ref, yc_ref, yd_ref, pe_ref, wo_ref, gp_ref, wpe_ref, wpg_ref, o_ref):
    width = 4 * LANES
    z = _dot(ya_ref[...], wo_ref[0 * width:1 * width, :])
    z += _dot(yb_ref[...], wo_ref[1 * width:2 * width, :])
    z += _dot(yc_ref[...], wo_ref[2 * width:3 * width, :])
    z += _dot(yd_ref[...], wo_ref[3 * width:4 * width, :])
    ms = jnp.mean(z * z, axis=-1, keepdims=True)
    x = x_ref[...] + z * lax.rsqrt(ms + EPS) * gp_ref[...]
    logits = _dot(x.astype(BF16), wpg_ref[...])
    gate = 1.0 / (1.0 + jnp.exp(-logits))
    pe = _dot(pe_ref[...].astype(BF16), wpe_ref[...])
    o_ref[...] = x + pe * gate


def _out_stage(x2, ya, yb, yc, yd, pe2, w_o, g_post, w_pe, w_pg, tm):
    m = x2.shape[0]
    row = lambda w: pl.BlockSpec((tm, w), lambda i: (i, 0))
    const = lambda r, c: pl.BlockSpec((r, c), lambda i: (0, 0), pipeline_mode=pl.Buffered(1))
    return pl.pallas_call(
        _out_kernel,
        grid=(m // tm,),
        in_specs=[row(D_MODEL), row(4 * LANES), row(4 * LANES), row(4 * LANES), row(4 * LANES),
                  row(PLE_DIM), const(MIX_WIDTH, D_MODEL), const(1, D_MODEL),
                  const(PLE_DIM, D_MODEL), const(D_MODEL, D_MODEL)],
        out_specs=row(D_MODEL),
        out_shape=jax.ShapeDtypeStruct((m, D_MODEL), F32),
        compiler_params=_params(("parallel",)),
        name="out_stage",
    )(x2, ya, yb, yc, yd, pe2, w_o, g_post.reshape(1, D_MODEL), w_pe, w_pg)


def _prep_w_in(w):
    parts, start = [], 0
    for n in IN_SIZES:
        parts.append(w[:, start:start + n])
        start += n
    aq, ak, av, ag, bx, bg, cq, ck, cv, cg, dq, dk, dv, dg = parts

    log2_scale = SCALE * LOG2E
    cols = [aq * log2_scale, ag, bx, bg, cq * log2_scale, ck, cv, cg, dq, dg, ak, av, dk, dv]
    return jnp.concatenate(cols, axis=1).astype(BF16)


def _rel_bucket(rel):
    nb = REL_BUCKETS // 2
    max_exact = nb // 2
    ret = jnp.where(rel > 0, nb, 0)
    n = jnp.abs(rel)
    nf = jnp.maximum(n, 1).astype(F32)
    large = max_exact + (jnp.log(nf / max_exact) / math.log(REL_MAX_DIST / max_exact)
                         * (nb - max_exact)).astype(jnp.int32)
    large = jnp.minimum(large, nb - 1)
    return ret + jnp.where(n < max_exact, n, large)


def _bucket_values(tab, rel):
    bucket = _rel_bucket(rel)[None]
    out = jnp.zeros((tab.shape[1],) + rel.shape, F32)
    for b in range(REL_BUCKETS):
        out = jnp.where(bucket == b, tab[b].reshape((-1,) + (1,) * rel.ndim), out)
    return out


def _bias_tables(rel_bias, tq, tk):
    tab = rel_bias.astype(F32)
    ck = jnp.arange(3 * A_BLOCK)[:, None]
    rq = jnp.arange(A_BLOCK)[None, :]
    rel_a = ck - A_BLOCK - rq
    vals = _bucket_values(tab[:, :A_HEADS], rel_a) * LOG2E
    in_band = jnp.abs(rel_a) <= A_WINDOW
    variants = []
    for edge in (ck >= A_BLOCK, ck >= 0, ck < 2 * A_BLOCK):
        masked = jnp.where(in_band & edge, vals, NEG_INF)
        variants.append(jnp.transpose(masked.reshape(2, 4, 3 * A_BLOCK, A_BLOCK), (0, 2, 1, 3))
                        .reshape(2, 3 * A_BLOCK, 4 * A_BLOCK))
    bias_a = jnp.stack(variants)
    assert tk >= REL_MAX_DIST
    n_tab = tq // tk + 4
    d = jnp.arange(n_tab)[:, None, None]
    c = jnp.arange(tk)[None, :, None]
    r = jnp.arange(tq)[None, None, :]
    bias_c = _bucket_values(tab[:, A_HEADS:], c + (d - 2) * tk - r) * LOG2E
    return bias_a, bias_c


def _axial_rope(s_len):
    rows = s_len // GRID_W
    t_row = jnp.repeat(jnp.arange(rows), GRID_W).astype(F32)
    t_col = jnp.tile(jnp.arange(GRID_W), rows).astype(F32)
    half = HEAD_DIM // 2
    inv = ROPE_THETA ** (-jnp.arange(0, half, 2, dtype=F32) / half)
    ang_r = t_row[:, None] * inv[None, :]
    ang_c = t_col[:, None] * inv[None, :]
    ang = jnp.concatenate([ang_r, ang_r, ang_c, ang_c] * 2, axis=-1)
    return jnp.cos(ang), jnp.sin(ang)


def _tiles(s_len):
    return dict(tm_in=min(512, s_len), tn_in=U_WIDTH // 2, tq_pool=min(256, s_len), tm_out=min(256, s_len),
                tq_attn=512, tk_attn=256, nb_c=16, nb_d=16, ahead=3, tq_win=512, ahead_win=1)


def _layer(x, pe, layer_idx, tables, rope, w, tiles):
    bsz, s_len, _ = x.shape
    m = bsz * s_len
    bias_a, bias_c = tables
    cos, sin = rope
    tq, tk, ahead = tiles["tq_attn"], tiles["tk_attn"], tiles["ahead"]
    x2 = x.reshape(m, D_MODEL)
    u2 = _in_proj(x2, w["g_pre"], w["w_in"], tiles["tm_in"], tiles["tn_in"])
    u3 = u2.reshape(bsz, s_len, U_WIDTH)
    qd, kd, vtc, vtd, vta, qn, kn = _attn_prep(u2, cos, sin, w["qnorm_d"], w["knorm_d"], bsz, s_len, tk)
    ya = _win_attn(u3, vta, bias_a, w["sink"], tiles["tq_win"], tiles["ahead_win"])
    yb = _pool(u3, w["pool_w"], w["pool_scale"], tiles["tq_pool"])
    lambda_init = 0.8 - 0.6 * math.exp(-0.3 * layer_idx)
    span = math.log2(s_len)

    seg_max = lambda t: jnp.sqrt(jnp.max(t.reshape(bsz, -1, 8, C_HEADS, 2, HEAD_DIM)[:, :, 0, :, :, 0], axis=1))
    q_max, k_max = seg_max(qn), seg_max(kn) * BOUND_SLACK
    b_hi, b_lo = jnp.max(bias_c, axis=(1, 2, 3)), jnp.min(bias_c, axis=(1, 2, 3))
    stat_c = jnp.concatenate([k_max, jnp.broadcast_to(b_hi[None, :, None], (bsz, C_HEADS, 1)),
                              jnp.zeros((bsz, C_HEADS, 5), F32)], axis=2)
    stat_c = jnp.broadcast_to(stat_c[..., None], (bsz, C_HEADS, 8, LANES))
    fits_c = 2.0 * jnp.max(q_max * k_max) + jnp.max(b_hi - b_lo) + span <= MAX_EXP2_SPAN
    attend_c = lambda bounded: _diff_attn(u3, vtc, bias_c, stat_c, w["lam"], w["diff_subln"], lambda_init,
                                          tq, tk, tiles["nb_c"], ahead, bounded)
    yc = lax.cond(fits_c, lambda: attend_c(True), lambda: attend_c(False))

    bound_d = (HEAD_DIM * SCALE * LOG2E * BOUND_SLACK
               * jnp.max(jnp.abs(w["qnorm_d"])) * jnp.max(jnp.abs(w["knorm_d"])))
    fits_d = 2.0 * bound_d + span <= MAX_EXP2_SPAN
    attend_d = lambda bounded: _grid_attn(qd.reshape(bsz, s_len, -1), kd.reshape(bsz, s_len, -1), vtd, u3,
                                          jnp.full((8, LANES), bound_d, F32), tq, tk, tiles["nb_d"], ahead, bounded)
    yd = lax.cond(fits_d, lambda: attend_d(True), lambda: attend_d(False))
    flat = lambda y: y.reshape(m, 4 * LANES)
    out = _out_stage(x2, flat(ya), flat(yb), flat(yc), flat(yd), pe.reshape(m, PLE_DIM),
                     w["w_o"], w["g_post"], w["w_pe"], w["w_pg"], tiles["tm_out"])
    return out.reshape(bsz, s_len, D_MODEL)


def _trunk(x, p, rel_bias, layers, tiles=None):
    s_len = x.shape[1]
    tiles = tiles or _tiles(s_len)
    tables = _bias_tables(rel_bias, tiles["tq_attn"], tiles["tk_attn"])
    rope = _axial_rope(s_len)
    for i, w in enumerate(layers):
        x = _layer(x, p[i], i, tables, rope, w, tiles)
    return x


def _prep_layers(w_in, w_o, g_pre, g_post, sink_a, pool_w, pool_scale, lam_q1, lam_k1, lam_q2, lam_k2,
                 diff_subln, qnorm_d, knorm_d, w_pe, w_pg):
    layers = []
    for i in range(w_in.shape[0]):
        layers.append(dict(
            w_in=_prep_w_in(w_in[i]), w_o=w_o[i].astype(BF16), g_pre=g_pre[i], g_post=g_post[i],
            sink=jnp.repeat(sink_a[i].astype(F32) * LOG2E, A_BLOCK).reshape(2, 1, 4 * A_BLOCK),
            pool_w=pool_w[i], pool_scale=pool_scale[i],
            lam=jnp.stack([lam_q1[i], lam_k1[i], lam_q2[i], lam_k2[i]]).astype(F32),
            diff_subln=diff_subln[i], qnorm_d=qnorm_d[i], knorm_d=knorm_d[i],
            w_pe=w_pe[i].astype(BF16), w_pg=w_pg[i].astype(BF16)))
    return layers


def kernel(x_prompt, x_sample, p_prompt, p_sample, w_in, w_o, g_pre, g_post, sink_a, pool_w, pool_scale,
           lam_q1, lam_k1, lam_q2, lam_k2, diff_subln, qnorm_d, knorm_d, rel_bias, w_pe, w_pg):
    layers = _prep_layers(w_in, w_o, g_pre, g_post, sink_a, pool_w, pool_scale, lam_q1, lam_k1, lam_q2,
                          lam_k2, diff_subln, qnorm_d, knorm_d, w_pe, w_pg)
    y_prompt = _trunk(x_prompt, p_prompt, rel_bias, layers)
    y_sample = _trunk(x_sample, p_sample, rel_bias, layers)
    return (y_prompt, y_sample)
```

```python
import functools
import math

import jax
import jax.numpy as jnp
from jax import lax
from jax.experimental import pallas as pl
from jax.experimental.pallas import tpu as pltpu

F32 = jnp.float32
BF16 = jnp.bfloat16

D_MODEL = 2048
HEAD_DIM = 64
LANES = 128
PLE_DIM = 256
EPS = 1e-6
NEG_INF = -1e30
SCALE = HEAD_DIM ** -0.5
LOG2E = math.log2(math.e)
VT_ONES = 16
KV_ROWS = HEAD_DIM + VT_ONES
BOUND_SLACK = 1.02
MAX_EXP2_SPAN = 100.0

A_HEADS = 8
A_WINDOW = 128
A_BLOCK = 128
B_GROUPS = 4
B_POOL_SIZES = (2, 4, 8, 16)
B_HALO = 64
C_HEADS = 4
REL_BUCKETS = 32
REL_MAX_DIST = 128
ROPE_THETA = 10000.0
GRID_W = 64

IN_SIZES = (512, 128, 128, 512, 512, 512, 512, 512, 512, 512, 512, 128, 128, 512)
MIX_WIDTH = 2048

U_WIDTH = 5632
COL_AQ, COL_AG, COL_BX, COL_BG = 0, 4, 8, 12
COL_CQ, COL_CK, COL_CV, COL_CG = 16, 20, 24, 28
COL_DQ, COL_DG = 32, 36
COL_AK, COL_AV, COL_DK, COL_DV = 40, 41, 42, 43

VMEM_LIMIT = 56 * 1024 * 1024


def _params(sem, vmem=VMEM_LIMIT):
    return pltpu.CompilerParams(dimension_semantics=sem, vmem_limit_bytes=vmem)


def _dot(a, b):
    return jnp.dot(a, b, preferred_element_type=F32)


def _dot_nt(a, b):
    return lax.dot_general(a, b, (((1,), (1,)), ((), ())), preferred_element_type=F32)


def _silu(x):
    return x / (1.0 + jnp.exp(-x))


def _lane_is_low(width=LANES):
    return (lax.broadcasted_iota(jnp.int32, (1, width), 1) % LANES) < HEAD_DIM


def _in_proj_kernel(x_ref, g_ref, w_ref, o_ref, h_ref):
    @pl.when(pl.program_id(1) == 0)
    def _():
        x = x_ref[...]
        ms = jnp.mean(x * x, axis=-1, keepdims=True)
        h_ref[...] = (x * lax.rsqrt(ms + EPS) * g_ref[...]).astype(BF16)

    o_ref[...] = _dot(h_ref[...], w_ref[...]).astype(o_ref.dtype)


def _in_proj(x2, g_pre, w_in_p, tm, tn):
    m = x2.shape[0]
    return pl.pallas_call(
        _in_proj_kernel,
        grid=(m // tm, U_WIDTH // tn),
        in_specs=[
            pl.BlockSpec((tm, D_MODEL), lambda i, j: (i, 0)),
            pl.BlockSpec((1, D_MODEL), lambda i, j: (0, 0)),
            pl.BlockSpec((D_MODEL, tn), lambda i, j: (0, j)),
        ],
        out_specs=pl.BlockSpec((tm, tn), lambda i, j: (i, j)),
        out_shape=jax.ShapeDtypeStruct((m, U_WIDTH), BF16),
        scratch_shapes=[pltpu.VMEM((tm, D_MODEL), BF16)],
        compiler_params=_params(("parallel", "arbitrary")),
        name="in_proj",
    )(x2, g_pre.reshape(1, D_MODEL), w_in_p)


def _group_queries(x, group, low):
    xr = pltpu.roll(x, HEAD_DIM, 1)
    zero = jnp.zeros_like(x)
    if group == 0:
        return [jnp.where(low, x, zero), jnp.where(low, xr, zero)]
    return [jnp.where(low, zero, xr), jnp.where(low, zero, x)]


def _win_attn_kernel(q_ref, kp_ref, kc_ref, kn_ref, vp_ref, vc_ref, vn_ref, g_ref, bias_ref,
                     sink_ref, o_ref, *, nq, ahead):
    i = pl.program_id(1)
    n = pl.num_programs(1)
    low = _lane_is_low()
    kext = jnp.concatenate([kp_ref[0], kc_ref[0], kn_ref[0]], axis=0)
    vts = [vp_ref[0, 0]] + [vc_ref[0, b] for b in range(nq)] + [vn_ref[0, 0]]
    units = [(qb, grp) for qb in range(nq) for grp in range(2)]

    def qk(u):
        qb, grp = units[u]
        rows = slice(qb * A_BLOCK, (qb + 1) * A_BLOCK)
        heads = []
        for pair in (2 * grp, 2 * grp + 1):
            heads += _group_queries(q_ref[0, rows, pair * LANES:(pair + 1) * LANES], grp, low)
        return _dot_nt(kext[qb * A_BLOCK:(qb + 3) * A_BLOCK], jnp.concatenate(heads, axis=0))

    def softmax(u, st):
        qb, grp = units[u]
        variant = 1
        if qb == 0:
            variant = jnp.where(i == 0, 0, variant)
        if qb == nq - 1:
            variant = jnp.where(i == n - 1, 2, variant)
        sb = _same_bits(st) + bias_ref[variant, grp]
        m = jnp.maximum(jnp.max(sb, axis=0, keepdims=True), sink_ref[grp])
        return m, jnp.exp2(sb - m).astype(BF16)

    def pv(u, probs):
        qb, grp = units[u]
        m, pt = probs
        vt = jnp.concatenate([vts[qb + d][grp] for d in range(3)], axis=1)
        acc = _dot(vt, pt)
        ot = acc[:HEAD_DIM] / (acc[HEAD_DIM:HEAD_DIM + 1] + jnp.exp2(sink_ref[grp] - m))
        rows = slice(qb * A_BLOCK, (qb + 1) * A_BLOCK)
        for pair in range(2):
            yt = jnp.concatenate([ot[:, (2 * pair) * A_BLOCK:(2 * pair + 1) * A_BLOCK],
                                  ot[:, (2 * pair + 1) * A_BLOCK:(2 * pair + 2) * A_BLOCK]], axis=0)
            cols = slice((2 * grp + pair) * LANES, (2 * grp + pair + 1) * LANES)
            gate = g_ref[0, rows, cols].astype(F32)
            o_ref[0, rows, cols] = (yt.T * _silu(gate)).astype(o_ref.dtype)

    _flash_units(len(units), ahead, qk, softmax, pv)


def _win_attn(u3, vta, bias_a, sink_a, tq, ahead):
    bsz, s_len, _ = u3.shape
    nq = tq // A_BLOCK
    nblk = s_len // A_BLOCK
    wide = (1, tq, 4 * LANES)
    const = lambda shape: pl.BlockSpec(shape, lambda bi, i: (0,) * len(shape), pipeline_mode=pl.Buffered(1))
    return pl.pallas_call(
        functools.partial(_win_attn_kernel, nq=nq, ahead=ahead),
        grid=(bsz, s_len // tq),
        in_specs=[
            pl.BlockSpec(wide, lambda bi, i: (bi, i, COL_AQ // 4)),
            pl.BlockSpec((1, A_BLOCK, LANES), lambda bi, i: (bi, jnp.maximum(i * nq - 1, 0), COL_AK)),
            pl.BlockSpec((1, tq, LANES), lambda bi, i: (bi, i, COL_AK)),
            pl.BlockSpec((1, A_BLOCK, LANES), lambda bi, i: (bi, jnp.minimum((i + 1) * nq, nblk - 1), COL_AK)),
            pl.BlockSpec((1, 1, 2, KV_ROWS, A_BLOCK), lambda bi, i: (bi, jnp.maximum(i * nq - 1, 0), 0, 0, 0)),
            pl.BlockSpec((1, nq, 2, KV_ROWS, A_BLOCK), lambda bi, i: (bi, i, 0, 0, 0)),
            pl.BlockSpec((1, 1, 2, KV_ROWS, A_BLOCK),
                         lambda bi, i: (bi, jnp.minimum((i + 1) * nq, nblk - 1), 0, 0, 0)),
            pl.BlockSpec(wide, lambda bi, i: (bi, i, COL_AG // 4)),
            const(bias_a.shape),
            const(sink_a.shape),
        ],
        out_specs=pl.BlockSpec(wide, lambda bi, i: (bi, i, 0)),
        out_shape=jax.ShapeDtypeStruct((bsz, s_len, 4 * LANES), BF16),
        compiler_params=_params(("parallel", "parallel")),
        name="win_attn",
    )(u3, u3, u3, u3, vta, vta, vta, u3, bias_a, sink_a)


def _pool_kernel(xp_ref, xc_ref, xn_ref, g_ref, band_ref, w_ref, sc_ref, o_ref, *, s_len, tq):
    i = pl.program_id(1)
    n = pl.num_programs(1)
    xp = xp_ref[0]
    xn = xn_ref[0]
    xp = jnp.where(i > 0, xp, jnp.zeros_like(xp))
    xn = jnp.where(i < n - 1, xn, jnp.zeros_like(xn))
    xc = xc_ref[0]
    xext = jnp.concatenate([xp, xc, xn], axis=0)
    t = i * tq + lax.broadcasted_iota(jnp.int32, (tq, 1), 0)
    for gi, size in enumerate(B_POOL_SIZES):
        lo = size // 2
        hi = size - lo - 1
        lanes = slice(gi * LANES, (gi + 1) * LANES)
        win_sum = _dot(band_ref[gi], xext[:, lanes])
        start = jnp.maximum(t - lo, 0)
        end = jnp.minimum(t + hi, s_len - 1) + 1
        pooled = win_sum / (end - start).astype(F32) - xc[:, lanes].astype(F32)
        y = _dot(pooled.astype(BF16), w_ref[gi]) * sc_ref[:, lanes]
        gate = g_ref[0, :, lanes].astype(F32)
        o_ref[0, :, lanes] = (y * _silu(gate)).astype(o_ref.dtype)


def _pool_bands(tq):
    r = jnp.arange(tq)[:, None]
    c = jnp.arange(tq + 2 * B_HALO)[None, :] - B_HALO
    bands = []
    for size in B_POOL_SIZES:
        lo = size // 2
        hi = size - lo - 1
        bands.append(((c - r >= -lo) & (c - r <= hi)).astype(BF16))
    return jnp.stack(bands)


def _pool(u3, pool_w, pool_scale, tq):
    bsz, s_len, _ = u3.shape
    nt = s_len // tq
    per = tq // B_HALO
    nh = s_len // B_HALO
    wide = (1, tq, 4 * LANES)
    halo = (1, B_HALO, 4 * LANES)
    return pl.pallas_call(
        functools.partial(_pool_kernel, s_len=s_len, tq=tq),
        grid=(bsz, nt),
        in_specs=[
            pl.BlockSpec(halo, lambda bi, i: (bi, jnp.maximum(i * per - 1, 0), COL_BX // 4)),
            pl.BlockSpec(wide, lambda bi, i: (bi, i, COL_BX // 4)),
            pl.BlockSpec(halo, lambda bi, i: (bi, jnp.minimum((i + 1) * per, nh - 1), COL_BX // 4)),
            pl.BlockSpec(wide, lambda bi, i: (bi, i, COL_BG // 4)),
            pl.BlockSpec((B_GROUPS, tq, tq + 2 * B_HALO), lambda bi, i: (0, 0, 0)),
            pl.BlockSpec((B_GROUPS, LANES, LANES), lambda bi, i: (0, 0, 0)),
            pl.BlockSpec((1, 4 * LANES), lambda bi, i: (0, 0)),
        ],
        out_specs=pl.BlockSpec(wide, lambda bi, i: (bi, i, 0)),
        out_shape=jax.ShapeDtypeStruct((bsz, s_len, 4 * LANES), BF16),
        compiler_params=_params(("parallel", "parallel")),
        name="pool",
    )(u3, u3, u3, u3, _pool_bands(tq), pool_w.astype(BF16), pool_scale.reshape(1, 4 * LANES))


def _softmax_t(st, m):
    m_new = jnp.maximum(m, jnp.max(st, axis=0, keepdims=True))
    return m_new, jnp.exp2(m - m_new), jnp.exp2(st - m_new).astype(BF16)


def _flash_units(n_units, ahead, qk, softmax, pv):
    scores = {u: qk(u) for u in range(min(ahead, n_units))}
    for u in range(n_units):
        probs = softmax(u, scores.pop(u))
        if u + ahead < n_units:
            scores[u + ahead] = qk(u + ahead)
        pv(u, probs)


def _same_bits(x):
    return lax.bitcast_convert_type(lax.bitcast_convert_type(x, jnp.int32), F32)


def _transposed(x):
    return x.astype(F32).T.astype(BF16)


def _masked_halves(q, low):
    zero = jnp.zeros_like(q)
    return jnp.where(low, q, zero), jnp.where(low, zero, q)


def _diff_attn_kernel(q_ref, k_ref, vt_ref, g_ref, bias_ref, stat_ref, lam_ref, sub_ref, o_ref,
                      *, tq, tk, nb, n_iter, ahead, lambda_init, bounded):
    i = pl.program_id(2)
    ratio = tq // tk
    n_tab = bias_ref.shape[1]
    rows = vt_ref.shape[-2]
    low = _lane_is_low()
    q = q_ref[0]
    qs = _masked_halves(q, low)
    bounds = None
    if bounded:
        qf = q.astype(F32)
        halves = jnp.concatenate([jnp.where(low, 1.0, 0.0), jnp.where(low, 0.0, 1.0)] +
                                 [jnp.zeros((1, LANES), F32)] * 6, axis=0).astype(BF16)
        norm2 = _dot_nt(halves, (qf * qf).astype(BF16))
        bounds = [jnp.sqrt(norm2[h:h + 1]) * stat_ref[0, 0, h:h + 1, 0:1] + stat_ref[0, 0, 2:3, 0:1]
                  for h in range(2)]

    def step(j, carry):
        state = list(carry)
        blocks = [nb * j + b for b in range(nb)]
        ks = [k_ref[0, pl.ds(pl.multiple_of(blk * tk, tk), tk), :] for blk in blocks]
        vts = [vt_ref[0, 0, blk] for blk in blocks]
        tabs = [jnp.clip(blk - ratio * i + 2, 0, n_tab - 1) for blk in blocks]

        def qk(u):
            return _dot_nt(ks[u // 2], qs[u % 2])

        def softmax(u, st):
            sb = _same_bits(st) + bias_ref[0, tabs[u // 2]]
            if bounded:
                return None, None, jnp.exp2(sb - bounds[u % 2]).astype(BF16)
            return _softmax_t(sb, state[u % 2][0])

        def pv(u, probs):
            m_new, alpha, pt = probs
            acc = state[u % 2][1]
            state[u % 2] = (m_new, (acc if bounded else alpha * acc) + _dot(vts[u // 2], pt))

        _flash_units(2 * nb, ahead, qk, softmax, pv)
        return tuple(state)

    m0 = None if bounded else jnp.full((1, tq), NEG_INF, F32)
    init = tuple((m0, jnp.zeros((rows, tq), F32)) for _ in range(2))
    (_, a1), (_, a2) = lax.fori_loop(0, n_iter, step, init)

    dot1 = jnp.sum(lam_ref[0:1, :] * lam_ref[1:2, :], axis=-1, keepdims=True)
    dot2 = jnp.sum(lam_ref[2:3, :] * lam_ref[3:4, :], axis=-1, keepdims=True)
    lam = jnp.exp(dot1) - jnp.exp(dot2) + lambda_init
    yt = a1[:LANES] / a1[LANES:LANES + 1] - lam * (a2[:LANES] / a2[LANES:LANES + 1])
    y = yt.T
    ms = jnp.mean(y * y, axis=-1, keepdims=True)
    y = y * lax.rsqrt(ms + EPS) * sub_ref[...] * (1.0 - lambda_init)
    o_ref[0] = (y * _silu(g_ref[0].astype(F32))).astype(o_ref.dtype)


def _diff_attn(u3, vtc, bias_c, stat_c, lam_vecs, subln, lambda_init, tq, tk, nb, ahead, bounded):
    bsz, s_len, _ = u3.shape
    nk = s_len // tk
    rows = vtc.shape[-2]
    n_tab = bias_c.shape[1]
    return pl.pallas_call(
        functools.partial(_diff_attn_kernel, tq=tq, tk=tk, nb=nb, n_iter=nk // nb, ahead=ahead,
                          lambda_init=lambda_init, bounded=bounded),
        grid=(bsz, C_HEADS, s_len // tq),
        in_specs=[
            pl.BlockSpec((1, tq, LANES), lambda bi, h, i: (bi, i, COL_CQ + h)),
            pl.BlockSpec((1, s_len, LANES), lambda bi, h, i: (bi, 0, COL_CK + h)),
            pl.BlockSpec((1, 1, nk, rows, tk), lambda bi, h, i: (bi, h, 0, 0, 0)),
            pl.BlockSpec((1, tq, LANES), lambda bi, h, i: (bi, i, COL_CG + h)),
            pl.BlockSpec((1, n_tab, tk, tq), lambda bi, h, i: (h, 0, 0, 0)),
            pl.BlockSpec((1, 1, 8, LANES), lambda bi, h, i: (bi, h, 0, 0)),
            pl.BlockSpec((4, HEAD_DIM), lambda bi, h, i: (0, 0)),
            pl.BlockSpec((1, LANES), lambda bi, h, i: (0, 0)),
        ],
        out_specs=pl.BlockSpec((1, tq, LANES), lambda bi, h, i: (bi, i, h)),
        out_shape=jax.ShapeDtypeStruct((bsz, s_len, 4 * LANES), BF16),
        compiler_params=_params(("parallel", "parallel", "parallel")),
        name="diff_attn_bounded" if bounded else "diff_attn",
    )(u3, u3, vtc, u3, bias_c, stat_c, lam_vecs, subln.reshape(1, LANES))


def _grid_attn_kernel(q_ref, k_ref, vt_ref, g_ref, bound_ref, o_ref, *, tq, tk, nb, n_iter, ahead, bounded):
    grp = pl.program_id(1)
    lane = lax.broadcasted_iota(jnp.int32, (1, LANES), 1)
    keep = (lane >= grp * HEAD_DIM) & (lane < (grp + 1) * HEAD_DIM)
    rows = vt_ref.shape[-2]
    bound = bound_ref[0:1, 0:1]
    qs = []
    for pair in range(2):
        x = q_ref[0, :, pair * LANES:(pair + 1) * LANES]
        xr = pltpu.roll(x, HEAD_DIM, 1)
        zero = jnp.zeros_like(x)
        qs.append(_transposed(jnp.where(keep, jnp.where(grp == 0, x, xr), zero)))
        qs.append(_transposed(jnp.where(keep, jnp.where(grp == 0, xr, x), zero)))

    def step(j, carry):
        state = list(carry)
        blocks = [nb * j + b for b in range(nb)]
        ks = [k_ref[0, pl.ds(pl.multiple_of(blk * tk, tk), tk), :] for blk in blocks]
        vts = [vt_ref[0, 0, blk] for blk in blocks]

        def qk(u):
            return _dot(ks[u // 4], qs[u % 4])

        def softmax(u, st):
            if bounded:
                return None, None, jnp.exp2(st - bound).astype(BF16)
            return _softmax_t(st, state[u % 4][0])

        def pv(u, probs):
            m_new, alpha, pt = probs
            acc = state[u % 4][1]
            state[u % 4] = (m_new, (acc if bounded else alpha * acc) + _dot(vts[u // 4], pt))

        _flash_units(4 * nb, ahead, qk, softmax, pv)
        return tuple(state)

    m0 = None if bounded else jnp.full((1, tq), NEG_INF, F32)
    init = tuple((m0, jnp.zeros((rows, tq), F32)) for _ in range(4))
    state = lax.fori_loop(0, n_iter, step, init)
    for pair in range(2):
        a0, a1 = state[2 * pair][1], state[2 * pair + 1][1]
        yt = jnp.concatenate([a0[:HEAD_DIM] / a0[HEAD_DIM:HEAD_DIM + 1],
                              a1[:HEAD_DIM] / a1[HEAD_DIM:HEAD_DIM + 1]], axis=0)
        gate = g_ref[0, :, pair * LANES:(pair + 1) * LANES].astype(F32)
        o_ref[0, :, pair * LANES:(pair + 1) * LANES] = (yt.T * _silu(gate)).astype(o_ref.dtype)


def _grid_attn(qd3, kd3, vtd, u3, bound_d, tq, tk, nb, ahead, bounded):
    bsz, s_len, _ = qd3.shape
    nk = s_len // tk
    rows = vtd.shape[-2]
    return pl.pallas_call(
        functools.partial(_grid_attn_kernel, tq=tq, tk=tk, nb=nb, n_iter=nk // nb, ahead=ahead,
                          bounded=bounded),
        grid=(bsz, 2, s_len // tq),
        in_specs=[
            pl.BlockSpec((1, tq, 2 * LANES), lambda bi, h, i: (bi, i, h)),
            pl.BlockSpec((1, s_len, LANES), lambda bi, h, i: (bi, 0, 0)),
            pl.BlockSpec((1, 1, nk, rows, tk), lambda bi, h, i: (bi, h, 0, 0, 0)),
            pl.BlockSpec((1, tq, 2 * LANES), lambda bi, h, i: (bi, i, COL_DG // 2 + h)),
            pl.BlockSpec((8, LANES), lambda bi, h, i: (0, 0)),
        ],
        out_specs=pl.BlockSpec((1, tq, 2 * LANES), lambda bi, h, i: (bi, i, h)),
        out_shape=jax.ShapeDtypeStruct((bsz, s_len, 4 * LANES), BF16),
        compiler_params=_params(("parallel", "parallel", "parallel")),
        name="grid_attn_bounded" if bounded else "grid_attn",
    )(qd3, kd3, vtd, u3, bound_d)


def _norm_rope(x, gain, cos, sin, seg):
    w = x.shape[-1]
    x2 = x * x
    hi = x2.astype(BF16)
    lo = (x2 - hi.astype(F32)).astype(BF16)
    ss = _dot(hi, seg) + _dot(lo, seg)
    y = x * lax.rsqrt(ss * (1.0 / HEAD_DIM) + EPS) * gain
    quarter = HEAD_DIM // 4
    ahead = pltpu.roll(y, w - quarter, 1)
    behind = pltpu.roll(y, quarter, 1)
    first = (lax.broadcasted_iota(jnp.int32, (1, w), 1) % (2 * quarter)) < quarter
    rot = jnp.where(first, -ahead, behind)
    reps = w // LANES
    cos_w = jnp.concatenate([cos] * reps, axis=1)
    sin_w = jnp.concatenate([sin] * reps, axis=1)
    return y * cos_w + rot * sin_w


def _attn_prep_kernel(q_ref, k_ref, vc_ref, vd_ref, va_ref, qc_ref, kc_ref, cos_ref, sin_ref, gq_ref, gk_ref,
                      seg_ref, qo_ref, ko_ref, vtc_ref, vtd_ref, vta_ref, qn_ref, kn_ref):
    cos = cos_ref[...]
    sin = sin_ref[...]
    seg = seg_ref[...]
    q = _norm_rope(q_ref[...].astype(F32), gq_ref[...], cos, sin, seg)
    qo_ref[...] = (q * (SCALE * LOG2E)).astype(qo_ref.dtype)
    k = _norm_rope(k_ref[...].astype(F32), gk_ref[...], cos, sin, seg[:LANES, :LANES])
    ko_ref[...] = k.astype(ko_ref.dtype)
    tm = q.shape[0]
    ones = jnp.ones((VT_ONES, tm), BF16)
    vct = vc_ref[...].astype(F32).T
    for h in range(C_HEADS):
        vtc_ref[0, h, 0] = jnp.concatenate([vct[h * LANES:(h + 1) * LANES].astype(BF16), ones], axis=0)
    vdt = vd_ref[...].astype(F32).T.astype(BF16)
    vat = va_ref[...].astype(F32).T.astype(BF16)
    for h in range(2):
        vtd_ref[0, h, 0] = jnp.concatenate([vdt[h * HEAD_DIM:(h + 1) * HEAD_DIM], ones], axis=0)
        for kb in range(tm // A_BLOCK):
            cols = slice(kb * A_BLOCK, (kb + 1) * A_BLOCK)
            vta_ref[0, kb, h] = jnp.concatenate([vat[h * HEAD_DIM:(h + 1) * HEAD_DIM, cols], ones[:, cols]], axis=0)
    for src, dst in ((qc_ref, qn_ref), (kc_ref, kn_ref)):
        x = src[...].astype(F32)
        seg_sums = _dot((x * x).astype(BF16), seg)
        dst[0] = jnp.broadcast_to(jnp.max(seg_sums, axis=0, keepdims=True), (8, 4 * LANES))


def _attn_prep(u2, cos, sin, qnorm, knorm, bsz, s_len, tm):
    m = u2.shape[0]
    per_seq = s_len // tm
    per_tile = tm // A_BLOCK
    lane = jnp.arange(4 * LANES)
    seg = (lane[:, None] // HEAD_DIM == lane[None, :] // HEAD_DIM).astype(BF16)
    rows_c = LANES + VT_ONES
    return pl.pallas_call(
        _attn_prep_kernel,
        grid=(m // tm,),
        in_specs=[
            pl.BlockSpec((tm, 4 * LANES), lambda i: (i, COL_DQ // 4)),
            pl.BlockSpec((tm, LANES), lambda i: (i, COL_DK)),
            pl.BlockSpec((tm, 4 * LANES), lambda i: (i, COL_CV // 4)),
            pl.BlockSpec((tm, LANES), lambda i: (i, COL_DV)),
            pl.BlockSpec((tm, LANES), lambda i: (i, COL_AV)),
            pl.BlockSpec((tm, 4 * LANES), lambda i: (i, COL_CQ // 4)),
            pl.BlockSpec((tm, 4 * LANES), lambda i: (i, COL_CK // 4)),
            pl.BlockSpec((tm, LANES), lambda i: (i % per_seq, 0)),
            pl.BlockSpec((tm, LANES), lambda i: (i % per_seq, 0)),
            pl.BlockSpec((1, 4 * LANES), lambda i: (0, 0)),
            pl.BlockSpec((1, LANES), lambda i: (0, 0)),
            pl.BlockSpec((4 * LANES, 4 * LANES), lambda i: (0, 0)),
        ],
        out_specs=[
            pl.BlockSpec((tm, 4 * LANES), lambda i: (i, 0)),
            pl.BlockSpec((tm, LANES), lambda i: (i, 0)),
            pl.BlockSpec((1, C_HEADS, 1, rows_c, tm), lambda i: (i // per_seq, 0, i % per_seq, 0, 0)),
            pl.BlockSpec((1, 2, 1, KV_ROWS, tm), lambda i: (i // per_seq, 0, i % per_seq, 0, 0)),
            pl.BlockSpec((1, per_tile, 2, KV_ROWS, A_BLOCK), lambda i: (i // per_seq, i % per_seq, 0, 0, 0)),
            pl.BlockSpec((1, 8, 4 * LANES), lambda i: (i, 0, 0)),
            pl.BlockSpec((1, 8, 4 * LANES), lambda i: (i, 0, 0)),
        ],
        out_shape=[jax.ShapeDtypeStruct((m, 4 * LANES), BF16),
                   jax.ShapeDtypeStruct((m, LANES), BF16),
                   jax.ShapeDtypeStruct((bsz, C_HEADS, per_seq, rows_c, tm), BF16),
                   jax.ShapeDtypeStruct((bsz, 2, per_seq, KV_ROWS, tm), BF16),
                   jax.ShapeDtypeStruct((bsz, s_len // A_BLOCK, 2, KV_ROWS, A_BLOCK), BF16),
                   jax.ShapeDtypeStruct((m // tm, 8, 4 * LANES), F32),
                   jax.ShapeDtypeStruct((m // tm, 8, 4 * LANES), F32)],
        compiler_params=_params(("parallel",)),
        name="attn_prep",
    )(u2, u2, u2, u2, u2, u2, u2, cos, sin, jnp.tile(qnorm, 8).reshape(1, 4 * LANES),
      jnp.tile(knorm, 2).reshape(1, LANES), seg)


def _out_kernel(x_ref, ya_ref, yb_ref, yc_ref, yd_ref, pe_ref, wo_ref, gp_ref, wpe_ref, wpg_ref, o_ref):
    width = 4 * LANES
    z = _dot(ya_ref[...], wo_ref[0 * width:1 * width, :])
    z += _dot(yb_ref[...], wo_ref[1 * width:2 * width, :])
    z += _dot(yc_ref[...], wo_ref[2 * width:3 * width, :])
    z += _dot(yd_ref[...], wo_ref[3 * width:4 * width, :])
    ms = jnp.mean(z * z, axis=-1, keepdims=True)
    x = x_ref[...] + z * lax.rsqrt(ms + EPS) * gp_ref[...]
    logits = _dot(x.astype(BF16), wpg_ref[...])
    gate = 1.0 / (1.0 + jnp.exp(-logits))
    pe = _dot(pe_ref[...].astype(BF16), wpe_ref[...])
    o_ref[...] = x + pe * gate


def _out_stage(x2, ya, yb, yc, yd, pe2, w_o, g_post, w_pe, w_pg, tm):
    m = x2.shape[0]
    row = lambda w: pl.BlockSpec((tm, w), lambda i: (i, 0))
    const = lambda r, c: pl.BlockSpec((r, c), lambda i: (0, 0), pipeline_mode=pl.Buffered(1))
    return pl.pallas_call(
        _out_kernel,
        grid=(m // tm,),
        in_specs=[row(D_MODEL), row(4 * LANES), row(4 * LANES), row(4 * LANES), row(4 * LANES),
                  row(PLE_DIM), const(MIX_WIDTH, D_MODEL), const(1, D_MODEL),
                  const(PLE_DIM, D_MODEL), const(D_MODEL, D_MODEL)],
        out_specs=row(D_MODEL),
        out_shape=jax.ShapeDtypeStruct((m, D_MODEL), F32),
        compiler_params=_params(("parallel",)),
        name="out_stage",
    )(x2, ya, yb, yc, yd, pe2, w_o, g_post.reshape(1, D_MODEL), w_pe, w_pg)


def _prep_w_in(w):
    parts, start = [], 0
    for n in IN_SIZES:
        parts.append(w[:, start:start + n])
        start += n
    aq, ak, av, ag, bx, bg, cq, ck, cv, cg, dq, dk, dv, dg = parts

    log2_scale = SCALE * LOG2E
    cols = [aq * log2_scale, ag, bx, bg, cq * log2_scale, ck, cv, cg, dq, dg, ak, av, dk, dv]
    return jnp.concatenate(cols, axis=1).astype(BF16)


def _rel_bucket(rel):
    nb = REL_BUCKETS // 2
    max_exact = nb // 2
    ret = jnp.where(rel > 0, nb, 0)
    n = jnp.abs(rel)
    nf = jnp.maximum(n, 1).astype(F32)
    large = max_exact + (jnp.log(nf / max_exact) / math.log(REL_MAX_DIST / max_exact)
                         * (nb - max_exact)).astype(jnp.int32)
    large = jnp.minimum(large, nb - 1)
    return ret + jnp.where(n < max_exact, n, large)


def _bucket_values(tab, rel):
    bucket = _rel_bucket(rel)[None]
    out = jnp.zeros((tab.shape[1],) + rel.shape, F32)
    for b in range(REL_BUCKETS):
        out = jnp.where(bucket == b, tab[b].reshape((-1,) + (1,) * rel.ndim), out)
    return out


def _bias_tables(rel_bias, tq, tk):
    tab = rel_bias.astype(F32)
    ck = jnp.arange(3 * A_BLOCK)[:, None]
    rq = jnp.arange(A_BLOCK)[None, :]
    rel_a = ck - A_BLOCK - rq
    vals = _bucket_values(tab[:, :A_HEADS], rel_a) * LOG2E
    in_band = jnp.abs(rel_a) <= A_WINDOW
    variants = []
    for edge in (ck >= A_BLOCK, ck >= 0, ck < 2 * A_BLOCK):
        masked = jnp.where(in_band & edge, vals, NEG_INF)
        variants.append(jnp.transpose(masked.reshape(2, 4, 3 * A_BLOCK, A_BLOCK), (0, 2, 1, 3))
                        .reshape(2, 3 * A_BLOCK, 4 * A_BLOCK))
    bias_a = jnp.stack(variants)
    assert tk >= REL_MAX_DIST
    n_tab = tq // tk + 4
    d = jnp.arange(n_tab)[:, None, None]
    c = jnp.arange(tk)[None, :, None]
    r = jnp.arange(tq)[None, None, :]
    bias_c = _bucket_values(tab[:, A_HEADS:], c + (d - 2) * tk - r) * LOG2E
    return bias_a, bias_c


def _axial_rope(s_len):
    rows = s_len // GRID_W
    t_row = jnp.repeat(jnp.arange(rows), GRID_W).astype(F32)
    t_col = jnp.tile(jnp.arange(GRID_W), rows).astype(F32)
    half = HEAD_DIM // 2
    inv = ROPE_THETA ** (-jnp.arange(0, half, 2, dtype=F32) / half)
    ang_r = t_row[:, None] * inv[None, :]
    ang_c = t_col[:, None] * inv[None, :]
    ang = jnp.concatenate([ang_r, ang_r, ang_c, ang_c] * 2, axis=-1)
    return jnp.cos(ang), jnp.sin(ang)


def _tiles(s_len):
    return dict(tm_in=min(512, s_len), tn_in=U_WIDTH // 2, tq_pool=min(256, s_len), tm_out=min(256, s_len),
                tq_attn=512, tk_attn=256, nb_c=16, nb_d=16, ahead=3, tq_win=512, ahead_win=1)


def _layer(x, pe, layer_idx, tables, rope, w, tiles):
    bsz, s_len, _ = x.shape
    m = bsz * s_len
    bias_a, bias_c = tables
    cos, sin = rope
    tq, tk, ahead = tiles["tq_attn"], tiles["tk_attn"], tiles["ahead"]
    x2 = x.reshape(m, D_MODEL)
    u2 = _in_proj(x2, w["g_pre"], w["w_in"], tiles["tm_in"], tiles["tn_in"])
    u3 = u2.reshape(bsz, s_len, U_WIDTH)
    qd, kd, vtc, vtd, vta, qn, kn = _attn_prep(u2, cos, sin, w["qnorm_d"], w["knorm_d"], bsz, s_len, tk)
    ya = _win_attn(u3, vta, bias_a, w["sink"], tiles["tq_win"], tiles["ahead_win"])
    yb = _pool(u3, w["pool_w"], w["pool_scale"], tiles["tq_pool"])
    lambda_init = 0.8 - 0.6 * math.exp(-0.3 * layer_idx)
    span = math.log2(s_len)

    seg_max = lambda t: jnp.sqrt(jnp.max(t.reshape(bsz, -1, 8, C_HEADS, 2, HEAD_DIM)[:, :, 0, :, :, 0], axis=1))
    q_max, k_max = seg_max(qn), seg_max(kn) * BOUND_SLACK
    b_hi, b_lo = jnp.max(bias_c, axis=(1, 2, 3)), jnp.min(bias_c, axis=(1, 2, 3))
    stat_c = jnp.concatenate([k_max, jnp.broadcast_to(b_hi[None, :, None], (bsz, C_HEADS, 1)),
                              jnp.zeros((bsz, C_HEADS, 5), F32)], axis=2)
    stat_c = jnp.broadcast_to(stat_c[..., None], (bsz, C_HEADS, 8, LANES))
    fits_c = 2.0 * jnp.max(q_max * k_max) + jnp.max(b_hi - b_lo) + span <= MAX_EXP2_SPAN
    attend_c = lambda bounded: _diff_attn(u3, vtc, bias_c, stat_c, w["lam"], w["diff_subln"], lambda_init,
                                          tq, tk, tiles["nb_c"], ahead, bounded)
    yc = lax.cond(fits_c, lambda: attend_c(True), lambda: attend_c(False))

    bound_d = (HEAD_DIM * SCALE * LOG2E * BOUND_SLACK
               * jnp.max(jnp.abs(w["qnorm_d"])) * jnp.max(jnp.abs(w["knorm_d"])))
    fits_d = 2.0 * bound_d + span <= MAX_EXP2_SPAN
    attend_d = lambda bounded: _grid_attn(qd.reshape(bsz, s_len, -1), kd.reshape(bsz, s_len, -1), vtd, u3,
                                          jnp.full((8, LANES), bound_d, F32), tq, tk, tiles["nb_d"], ahead, bounded)
    yd = lax.cond(fits_d, lambda: attend_d(True), lambda: attend_d(False))
    flat = lambda y: y.reshape(m, 4 * LANES)
    out = _out_stage(x2, flat(ya), flat(yb), flat(yc), flat(yd), pe.reshape(m, PLE_DIM),
                     w["w_o"], w["g_post"], w["w_pe"], w["w_pg"], tiles["tm_out"])
    return out.reshape(bsz, s_len, D_MODEL)


def _trunk(x, p, rel_bias, layers, tiles=None):
    s_len = x.shape[1]
    tiles = tiles or _tiles(s_len)
    tables = _bias_tables(rel_bias, tiles["tq_attn"], tiles["tk_attn"])
    rope = _axial_rope(s_len)
    for i, w in enumerate(layers):
        x = _layer(x, p[i], i, tables, rope, w, tiles)
    return x


def _prep_layers(w_in, w_o, g_pre, g_post, sink_a, pool_w, pool_scale, lam_q1, lam_k1, lam_q2, lam_k2,
                 diff_subln, qnorm_d, knorm_d, w_pe, w_pg):
    layers = []
    for i in range(w_in.shape[0]):
        layers.append(dict(
            w_in=_prep_w_in(w_in[i]), w_o=w_o[i].astype(BF16), g_pre=g_pre[i], g_post=g_post[i],
            sink=jnp.repeat(sink_a[i].astype(F32) * LOG2E, A_BLOCK).reshape(2, 1, 4 * A_BLOCK),
            pool_w=pool_w[i], pool_scale=pool_scale[i],
            lam=jnp.stack([lam_q1[i], lam_k1[i], lam_q2[i], lam_k2[i]]).astype(F32),
            diff_subln=diff_subln[i], qnorm_d=qnorm_d[i], knorm_d=knorm_d[i],
            w_pe=w_pe[i].astype(BF16), w_pg=w_pg[i].astype(BF16)))
    return layers


def kernel(x_prompt, x_sample, p_prompt, p_sample, w_in, w_o, g_pre, g_post, sink_a, pool_w, pool_scale,
           lam_q1, lam_k1, lam_q2, lam_k2, diff_subln, qnorm_d, knorm_d, rel_bias, w_pe, w_pg):
    layers = _prep_layers(w_in, w_o, g_pre, g_post, sink_a, pool_w, pool_scale, lam_q1, lam_k1, lam_q2,
                          lam_k2, diff_subln, qnorm_d, knorm_d, w_pe, w_pg)
    y_prompt = _trunk(x_prompt, p_prompt, rel_bias, layers)
    y_sample = _trunk(x_sample, p_sample, rel_bias, layers)
    return (y_prompt, y_sample)
```

```python
import functools
import math

import jax
import jax.numpy as jnp
from jax import lax
from jax.experimental import pallas as pl
from jax.experimental.pallas import tpu as pltpu

F32 = jnp.float32
BF16 = jnp.bfloat16

D_MODEL = 2048
HEAD_DIM = 64
LANES = 128
PLE_DIM = 256
EPS = 1e-6
NEG_INF = -1e30
SCALE = HEAD_DIM ** -0.5
LOG2E = math.log2(math.e)
VT_ONES = 16
KV_ROWS = HEAD_DIM + VT_ONES
BOUND_SLACK = 1.02
MAX_EXP2_SPAN = 100.0

A_HEADS = 8
A_WINDOW = 128
A_BLOCK = 128
B_GROUPS = 4
B_POOL_SIZES = (2, 4, 8, 16)
B_HALO = 64
C_HEADS = 4
REL_BUCKETS = 32
REL_MAX_DIST = 128
ROPE_THETA = 10000.0
GRID_W = 64

IN_SIZES = (512, 128, 128, 512, 512, 512, 512, 512, 512, 512, 512, 128, 128, 512)
MIX_WIDTH = 2048

U_WIDTH = 5632
COL_AQ, COL_AG, COL_BX, COL_BG = 0, 4, 8, 12
COL_CQ, COL_CK, COL_CV, COL_CG = 16, 20, 24, 28
COL_DQ, COL_DG = 32, 36
COL_AK, COL_AV, COL_DK, COL_DV = 40, 41, 42, 43

VMEM_LIMIT = 56 * 1024 * 1024


def _params(sem, vmem=VMEM_LIMIT):
    return pltpu.CompilerParams(dimension_semantics=sem, vmem_limit_bytes=vmem)


def _dot(a, b):
    return jnp.dot(a, b, preferred_element_type=F32)


def _dot_nt(a, b):
    return lax.dot_general(a, b, (((1,), (1,)), ((), ())), preferred_element_type=F32)


def _silu(x):
    return x / (1.0 + jnp.exp(-x))


def _lane_is_low(width=LANES):
    return (lax.broadcasted_iota(jnp.int32, (1, width), 1) % LANES) < HEAD_DIM


def _in_proj_kernel(x_ref, g_ref, w_ref, o_ref, h_ref):
    @pl.when(pl.program_id(1) == 0)
    def _():
        x = x_ref[...]
        ms = jnp.mean(x * x, axis=-1, keepdims=True)
        h_ref[...] = (x * lax.rsqrt(ms + EPS) * g_ref[...]).astype(BF16)

    o_ref[...] = _dot(h_ref[...], w_ref[...]).astype(o_ref.dtype)


def _in_proj(x2, g_pre, w_in_p, tm, tn):
    m = x2.shape[0]
    return pl.pallas_call(
        _in_proj_kernel,
        grid=(m // tm, U_WIDTH // tn),
        in_specs=[
            pl.BlockSpec((tm, D_MODEL), lambda i, j: (i, 0)),
            pl.BlockSpec((1, D_MODEL), lambda i, j: (0, 0)),
            pl.BlockSpec((D_MODEL, tn), lambda i, j: (0, j)),
        ],
        out_specs=pl.BlockSpec((tm, tn), lambda i, j: (i, j)),
        out_shape=jax.ShapeDtypeStruct((m, U_WIDTH), BF16),
        scratch_shapes=[pltpu.VMEM((tm, D_MODEL), BF16)],
        compiler_params=_params(("parallel", "arbitrary")),
        name="in_proj",
    )(x2, g_pre.reshape(1, D_MODEL), w_in_p)


def _group_queries(x, group, low):
    xr = pltpu.roll(x, HEAD_DIM, 1)
    zero = jnp.zeros_like(x)
    if group == 0:
        return [jnp.where(low, x, zero), jnp.where(low, xr, zero)]
    return [jnp.where(low, zero, xr), jnp.where(low, zero, x)]


def _win_attn_kernel(q_ref, kp_ref, kc_ref, kn_ref, vp_ref, vc_ref, vn_ref, g_ref, bias_ref,
                     sink_ref, o_ref, *, nq, ahead):
    i = pl.program_id(1)
    n = pl.num_programs(1)
    low = _lane_is_low()
    kext = jnp.concatenate([kp_ref[0], kc_ref[0], kn_ref[0]], axis=0)
    vts = [vp_ref[0, 0]] + [vc_ref[0, b] for b in range(nq)] + [vn_ref[0, 0]]
    units = [(qb, grp) for qb in range(nq) for grp in range(2)]

    def qk(u):
        qb, grp = units[u]
        rows = slice(qb * A_BLOCK, (qb + 1) * A_BLOCK)
        heads = []
        for pair in (2 * grp, 2 * grp + 1):
            heads += _group_queries(q_ref[0, rows, pair * LANES:(pair + 1) * LANES], grp, low)
        return _dot_nt(kext[qb * A_BLOCK:(qb + 3) * A_BLOCK], jnp.concatenate(heads, axis=0))

    def softmax(u, st):
        qb, grp = units[u]
        variant = 1
        if qb == 0:
            variant = jnp.where(i == 0, 0, variant)
        if qb == nq - 1:
            variant = jnp.where(i == n - 1, 2, variant)
        sb = _same_bits(st) + bias_ref[variant, grp]
        m = jnp.maximum(jnp.max(sb, axis=0, keepdims=True), sink_ref[grp])
        return m, jnp.exp2(sb - m).astype(BF16)

    def pv(u, probs):
        qb, grp = units[u]
        m, pt = probs
        vt = jnp.concatenate([vts[qb + d][grp] for d in range(3)], axis=1)
        acc = _dot(vt, pt)
        ot = acc[:HEAD_DIM] / (acc[HEAD_DIM:HEAD_DIM + 1] + jnp.exp2(sink_ref[grp] - m))
        rows = slice(qb * A_BLOCK, (qb + 1) * A_BLOCK)
        for pair in range(2):
            yt = jnp.concatenate([ot[:, (2 * pair) * A_BLOCK:(2 * pair + 1) * A_BLOCK],
                                  ot[:, (2 * pair + 1) * A_BLOCK:(2 * pair + 2) * A_BLOCK]], axis=0)
            cols = slice((2 * grp + pair) * LANES, (2 * grp + pair + 1) * LANES)
            gate = g_ref[0, rows, cols].astype(F32)
            o_ref[0, rows, cols] = (yt.T * _silu(gate)).astype(o_ref.dtype)

    _flash_units(len(units), ahead, qk, softmax, pv)


def _win_attn(u3, vta, bias_a, sink_a, tq, ahead):
    bsz, s_len, _ = u3.shape
    nq = tq // A_BLOCK
    nblk = s_len // A_BLOCK
    wide = (1, tq, 4 * LANES)
    const = lambda shape: pl.BlockSpec(shape, lambda bi, i: (0,) * len(shape), pipeline_mode=pl.Buffered(1))
    return pl.pallas_call(
        functools.partial(_win_attn_kernel, nq=nq, ahead=ahead),
        grid=(bsz, s_len // tq),
        in_specs=[
            pl.BlockSpec(wide, lambda bi, i: (bi, i, COL_AQ // 4)),
            pl.BlockSpec((1, A_BLOCK, LANES), lambda bi, i: (bi, jnp.maximum(i * nq - 1, 0), COL_AK)),
            pl.BlockSpec((1, tq, LANES), lambda bi, i: (bi, i, COL_AK)),
            pl.BlockSpec((1, A_BLOCK, LANES), lambda bi, i: (bi, jnp.minimum((i + 1) * nq, nblk - 1), COL_AK)),
            pl.BlockSpec((1, 1, 2, KV_ROWS, A_BLOCK), lambda bi, i: (bi, jnp.maximum(i * nq - 1, 0), 0, 0, 0)),
            pl.BlockSpec((1, nq, 2, KV_ROWS, A_BLOCK), lambda bi, i: (bi, i, 0, 0, 0)),
            pl.BlockSpec((1, 1, 2, KV_ROWS, A_BLOCK),
                         lambda bi, i: (bi, jnp.minimum((i + 1) * nq, nblk - 1), 0, 0, 0)),
            pl.BlockSpec(wide, lambda bi, i: (bi, i, COL_AG // 4)),
            const(bias_a.shape),
            const(sink_a.shape),
        ],
        out_specs=pl.BlockSpec(wide, lambda bi, i: (bi, i, 0)),
        out_shape=jax.ShapeDtypeStruct((bsz, s_len, 4 * LANES), BF16),
        compiler_params=_params(("parallel", "parallel")),
        name="win_attn",
    )(u3, u3, u3, u3, vta, vta, vta, u3, bias_a, sink_a)


def _pool_kernel(xp_ref, xc_ref, xn_ref, g_ref, band_ref, w_ref, sc_ref, o_ref, *, s_len, tq):
    i = pl.program_id(1)
    n = pl.num_programs(1)
    xp = xp_ref[0]
    xn = xn_ref[0]
    xp = jnp.where(i > 0, xp, jnp.zeros_like(xp))
    xn = jnp.where(i < n - 1, xn, jnp.zeros_like(xn))
    xc = xc_ref[0]
    xext = jnp.concatenate([xp, xc, xn], axis=0)
    t = i * tq + lax.broadcasted_iota(jnp.int32, (tq, 1), 0)
    for gi, size in enumerate(B_POOL_SIZES):
        lo = size // 2
        hi = size - lo - 1
        lanes = slice(gi * LANES, (gi + 1) * LANES)
        win_sum = _dot(band_ref[gi], xext[:, lanes])
        start = jnp.maximum(t - lo, 0)
        end = jnp.minimum(t + hi, s_len - 1) + 1
        pooled = win_sum / (end - start).astype(F32) - xc[:, lanes].astype(F32)
        y = _dot(pooled.astype(BF16), w_ref[gi]) * sc_ref[:, lanes]
        gate = g_ref[0, :, lanes].astype(F32)
        o_ref[0, :, lanes] = (y * _silu(gate)).astype(o_ref.dtype)


def _pool_bands(tq):
    r = jnp.arange(tq)[:, None]
    c = jnp.arange(tq + 2 * B_HALO)[None, :] - B_HALO
    bands = []
    for size in B_POOL_SIZES:
        lo = size // 2
        hi = size - lo - 1
        bands.append(((c - r >= -lo) & (c - r <= hi)).astype(BF16))
    return jnp.stack(bands)


def _pool(u3, pool_w, pool_scale, tq):
    bsz, s_len, _ = u3.shape
    nt = s_len // tq
    per = tq // B_HALO
    nh = s_len // B_HALO
    wide = (1, tq, 4 * LANES)
    halo = (1, B_HALO, 4 * LANES)
    return pl.pallas_call(
        functools.partial(_pool_kernel, s_len=s_len, tq=tq),
        grid=(bsz, nt),
        in_specs=[
            pl.BlockSpec(halo, lambda bi, i: (bi, jnp.maximum(i * per - 1, 0), COL_BX // 4)),
            pl.BlockSpec(wide, lambda bi, i: (bi, i, COL_BX // 4)),
            pl.BlockSpec(halo, lambda bi, i: (bi, jnp.minimum((i + 1) * per, nh - 1), COL_BX // 4)),
            pl.BlockSpec(wide, lambda bi, i: (bi, i, COL_BG // 4)),
            pl.BlockSpec((B_GROUPS, tq, tq + 2 * B_HALO), lambda bi, i: (0, 0, 0)),
            pl.BlockSpec((B_GROUPS, LANES, LANES), lambda bi, i: (0, 0, 0)),
            pl.BlockSpec((1, 4 * LANES), lambda bi, i: (0, 0)),
        ],
        out_specs=pl.BlockSpec(wide, lambda bi, i: (bi, i, 0)),
        out_shape=jax.ShapeDtypeStruct((bsz, s_len, 4 * LANES), BF16),
        compiler_params=_params(("parallel", "parallel")),
        name="pool",
    )(u3, u3, u3, u3, _pool_bands(tq), pool_w.astype(BF16), pool_scale.reshape(1, 4 * LANES))


def _softmax_t(st, m):
    m_new = jnp.maximum(m, jnp.max(st, axis=0, keepdims=True))
    return m_new, jnp.exp2(m - m_new), jnp.exp2(st - m_new).astype(BF16)


def _flash_units(n_units, ahead, qk, softmax, pv):
    scores = {u: qk(u) for u in range(min(ahead, n_units))}
    for u in range(n_units):
        probs = softmax(u, scores.pop(u))
        if u + ahead < n_units:
            scores[u + ahead] = qk(u + ahead)
        pv(u, probs)


def _same_bits(x):
    return lax.bitcast_convert_type(lax.bitcast_convert_type(x, jnp.int32), F32)


def _transposed(x):
    return x.astype(F32).T.astype(BF16)


def _masked_halves(q, low):
    zero = jnp.zeros_like(q)
    return jnp.where(low, q, zero), jnp.where(low, zero, q)


def _diff_attn_kernel(q_ref, k_ref, vt_ref, g_ref, bias_ref, stat_ref, lam_ref, sub_ref, o_ref,
                      *, tq, tk, nb, n_iter, ahead, lambda_init, bounded):
    i = pl.program_id(2)
    ratio = tq // tk
    n_tab = bias_ref.shape[1]
    rows = vt_ref.shape[-2]
    low = _lane_is_low()
    q = q_ref[0]
    qs = [_transposed(half) for half in _masked_halves(q, low)]
    bounds = None
    if bounded:
        qf = q.astype(F32)
        halves = jnp.concatenate([jnp.where(low, 1.0, 0.0), jnp.where(low, 0.0, 1.0)] +
                                 [jnp.zeros((1, LANES), F32)] * 6, axis=0).astype(BF16)
        norm2 = _dot_nt(halves, (qf * qf).astype(BF16))
        bounds = [jnp.sqrt(norm2[h:h + 1]) * stat_ref[0, 0, h:h + 1, 0:1] + stat_ref[0, 0, 2:3, 0:1]
                  for h in range(2)]

    def step(j, carry):
        state = list(carry)
        blocks = [nb * j + b for b in range(nb)]
        ks = [k_ref[0, pl.ds(pl.multiple_of(blk * tk, tk), tk), :] for blk in blocks]
        vts = [vt_ref[0, 0, blk] for blk in blocks]
        tabs = [jnp.clip(blk - ratio * i + 2, 0, n_tab - 1) for blk in blocks]

        def qk(u):
            return _dot(ks[u // 2], qs[u % 2])

        def softmax(u, st):
            sb = _same_bits(st) + bias_ref[0, tabs[u // 2]]
            if bounded:
                return None, None, jnp.exp2(sb - bounds[u % 2]).astype(BF16)
            return _softmax_t(sb, state[u % 2][0])

        def pv(u, probs):
            m_new, alpha, pt = probs
            acc = state[u % 2][1]
            state[u % 2] = (m_new, (acc if bounded else alpha * acc) + _dot(vts[u // 2], pt))

        _flash_units(2 * nb, ahead, qk, softmax, pv)
        return tuple(state)

    m0 = None if bounded else jnp.full((1, tq), NEG_INF, F32)
    init = tuple((m0, jnp.zeros((rows, tq), F32)) for _ in range(2))
    (_, a1), (_, a2) = lax.fori_loop(0, n_iter, step, init)

    dot1 = jnp.sum(lam_ref[0:1, :] * lam_ref[1:2, :], axis=-1, keepdims=True)
    dot2 = jnp.sum(lam_ref[2:3, :] * lam_ref[3:4, :], axis=-1, keepdims=True)
    lam = jnp.exp(dot1) - jnp.exp(dot2) + lambda_init
    yt = a1[:LANES] / a1[LANES:LANES + 1] - lam * (a2[:LANES] / a2[LANES:LANES + 1])
    y = yt.T
    ms = jnp.mean(y * y, axis=-1, keepdims=True)
    y = y * lax.rsqrt(ms + EPS) * sub_ref[...] * (1.0 - lambda_init)
    o_ref[0] = (y * _silu(g_ref[0].astype(F32))).astype(o_ref.dtype)


def _diff_attn(u3, vtc, bias_c, stat_c, lam_vecs, subln, lambda_init, tq, tk, nb, ahead, bounded):
    bsz, s_len, _ = u3.shape
    nk = s_len // tk
    rows = vtc.shape[-2]
    n_tab = bias_c.shape[1]
    return pl.pallas_call(
        functools.partial(_diff_attn_kernel, tq=tq, tk=tk, nb=nb, n_iter=nk // nb, ahead=ahead,
                          lambda_init=lambda_init, bounded=bounded),
        grid=(bsz, C_HEADS, s_len // tq),
        in_specs=[
            pl.BlockSpec((1, tq, LANES), lambda bi, h, i: (bi, i, COL_CQ + h)),
            pl.BlockSpec((1, s_len, LANES), lambda bi, h, i: (bi, 0, COL_CK + h)),
            pl.BlockSpec((1, 1, nk, rows, tk), lambda bi, h, i: (bi, h, 0, 0, 0)),
            pl.BlockSpec((1, tq, LANES), lambda bi, h, i: (bi, i, COL_CG + h)),
            pl.BlockSpec((1, n_tab, tk, tq), lambda bi, h, i: (h, 0, 0, 0)),
            pl.BlockSpec((1, 1, 8, LANES), lambda bi, h, i: (bi, h, 0, 0)),
            pl.BlockSpec((4, HEAD_DIM), lambda bi, h, i: (0, 0)),
            pl.BlockSpec((1, LANES), lambda bi, h, i: (0, 0)),
        ],
        out_specs=pl.BlockSpec((1, tq, LANES), lambda bi, h, i: (bi, i, h)),
        out_shape=jax.ShapeDtypeStruct((bsz, s_len, 4 * LANES), BF16),
        compiler_params=_params(("parallel", "parallel", "parallel")),
        name="diff_attn_bounded" if bounded else "diff_attn",
    )(u3, u3, vtc, u3, bias_c, stat_c, lam_vecs, subln.reshape(1, LANES))


def _grid_attn_kernel(q_ref, k_ref, vt_ref, g_ref, bound_ref, o_ref, *, tq, tk, nb, n_iter, ahead, bounded):
    grp = pl.program_id(1)
    lane = lax.broadcasted_iota(jnp.int32, (1, LANES), 1)
    keep = (lane >= grp * HEAD_DIM) & (lane < (grp + 1) * HEAD_DIM)
    rows = vt_ref.shape[-2]
    bound = bound_ref[0:1, 0:1]
    qs = []
    for pair in range(2):
        x = q_ref[0, :, pair * LANES:(pair + 1) * LANES]
        xr = pltpu.roll(x, HEAD_DIM, 1)
        zero = jnp.zeros_like(x)
        qs.append(_transposed(jnp.where(keep, jnp.where(grp == 0, x, xr), zero)))
        qs.append(_transposed(jnp.where(keep, jnp.where(grp == 0, xr, x), zero)))

    def step(j, carry):
        state = list(carry)
        blocks = [nb * j + b for b in range(nb)]
        ks = [k_ref[0, pl.ds(pl.multiple_of(blk * tk, tk), tk), :] for blk in blocks]
        vts = [vt_ref[0, 0, blk] for blk in blocks]

        def qk(u):
            return _dot(ks[u // 4], qs[u % 4])

        def softmax(u, st):
            if bounded:
                return None, None, jnp.exp2(st - bound).astype(BF16)
            return _softmax_t(st, state[u % 4][0])

        def pv(u, probs):
            m_new, alpha, pt = probs
            acc = state[u % 4][1]
            state[u % 4] = (m_new, (acc if bounded else alpha * acc) + _dot(vts[u // 4], pt))

        _flash_units(4 * nb, ahead, qk, softmax, pv)
        return tuple(state)

    m0 = None if bounded else jnp.full((1, tq), NEG_INF, F32)
    init = tuple((m0, jnp.zeros((rows, tq), F32)) for _ in range(4))
    state = lax.fori_loop(0, n_iter, step, init)
    for pair in range(2):
        a0, a1 = state[2 * pair][1], state[2 * pair + 1][1]
        yt = jnp.concatenate([a0[:HEAD_DIM] / a0[HEAD_DIM:HEAD_DIM + 1],
                              a1[:HEAD_DIM] / a1[HEAD_DIM:HEAD_DIM + 1]], axis=0)
        gate = g_ref[0, :, pair * LANES:(pair + 1) * LANES].astype(F32)
        o_ref[0, :, pair * LANES:(pair + 1) * LANES] = (yt.T * _silu(gate)).astype(o_ref.dtype)


def _grid_attn(qd3, kd3, vtd, u3, bound_d, tq, tk, nb, ahead, bounded):
    bsz, s_len, _ = qd3.shape
    nk = s_len // tk
    rows = vtd.shape[-2]
    return pl.pallas_call(
        functools.partial(_grid_attn_kernel, tq=tq, tk=tk, nb=nb, n_iter=nk // nb, ahead=ahead,
                          bounded=bounded),
        grid=(bsz, 2, s_len // tq),
        in_specs=[
            pl.BlockSpec((1, tq, 2 * LANES), lambda bi, h, i: (bi, i, h)),
            pl.BlockSpec((1, s_len, LANES), lambda bi, h, i: (bi, 0, 0)),
            pl.BlockSpec((1, 1, nk, rows, tk), lambda bi, h, i: (bi, h, 0, 0, 0)),
            pl.BlockSpec((1, tq, 2 * LANES), lambda bi, h, i: (bi, i, COL_DG // 2 + h)),
            pl.BlockSpec((8, LANES), lambda bi, h, i: (0, 0)),
        ],
        out_specs=pl.BlockSpec((1, tq, 2 * LANES), lambda bi, h, i: (bi, i, h)),
        out_shape=jax.ShapeDtypeStruct((bsz, s_len, 4 * LANES), BF16),
        compiler_params=_params(("parallel", "parallel", "parallel")),
        name="grid_attn_bounded" if bounded else "grid_attn",
    )(qd3, kd3, vtd, u3, bound_d)


def _norm_rope(x, gain, cos, sin, seg):
    w = x.shape[-1]
    x2 = x * x
    hi = x2.astype(BF16)
    lo = (x2 - hi.astype(F32)).astype(BF16)
    ss = _dot(hi, seg) + _dot(lo, seg)
    y = x * lax.rsqrt(ss * (1.0 / HEAD_DIM) + EPS) * gain
    quarter = HEAD_DIM // 4
    ahead = pltpu.roll(y, w - quarter, 1)
    behind = pltpu.roll(y, quarter, 1)
    first = (lax.broadcasted_iota(jnp.int32, (1, w), 1) % (2 * quarter)) < quarter
    rot = jnp.where(first, -ahead, behind)
    reps = w // LANES
    cos_w = jnp.concatenate([cos] * reps, axis=1)
    sin_w = jnp.concatenate([sin] * reps, axis=1)
    return y * cos_w + rot * sin_w


def _attn_prep_kernel(q_ref, k_ref, vc_ref, vd_ref, va_ref, qc_ref, kc_ref, cos_ref, sin_ref, gq_ref, gk_ref,
                      seg_ref, qo_ref, ko_ref, vtc_ref, vtd_ref, vta_ref, qn_ref, kn_ref):
    cos = cos_ref[...]
    sin = sin_ref[...]
    seg = seg_ref[...]
    q = _norm_rope(q_ref[...].astype(F32), gq_ref[...], cos, sin, seg)
    qo_ref[...] = (q * (SCALE * LOG2E)).astype(qo_ref.dtype)
    k = _norm_rope(k_ref[...].astype(F32), gk_ref[...], cos, sin, seg[:LANES, :LANES])
    ko_ref[...] = k.astype(ko_ref.dtype)
    tm = q.shape[0]
    ones = jnp.ones((VT_ONES, tm), BF16)
    vct = vc_ref[...].astype(F32).T
    for h in range(C_HEADS):
        vtc_ref[0, h, 0] = jnp.concatenate([vct[h * LANES:(h + 1) * LANES].astype(BF16), ones], axis=0)
    vdt = vd_ref[...].astype(F32).T.astype(BF16)
    vat = va_ref[...].astype(F32).T.astype(BF16)
    for h in range(2):
        vtd_ref[0, h, 0] = jnp.concatenate([vdt[h * HEAD_DIM:(h + 1) * HEAD_DIM], ones], axis=0)
        for kb in range(tm // A_BLOCK):
            cols = slice(kb * A_BLOCK, (kb + 1) * A_BLOCK)
            vta_ref[0, kb, h] = jnp.concatenate([vat[h * HEAD_DIM:(h + 1) * HEAD_DIM, cols], ones[:, cols]], axis=0)
    for src, dst in ((qc_ref, qn_ref), (kc_ref, kn_ref)):
        x = src[...].astype(F32)
        seg_sums = _dot((x * x).astype(BF16), seg)
        dst[0] = jnp.broadcast_to(jnp.max(seg_sums, axis=0, keepdims=True), (8, 4 * LANES))


def _attn_prep(u2, cos, sin, qnorm, knorm, bsz, s_len, tm):
    m = u2.shape[0]
    per_seq = s_len // tm
    per_tile = tm // A_BLOCK
    lane = jnp.arange(4 * LANES)
    seg = (lane[:, None] // HEAD_DIM == lane[None, :] // HEAD_DIM).astype(BF16)
    rows_c = LANES + VT_ONES
    return pl.pallas_call(
        _attn_prep_kernel,
        grid=(m // tm,),
        in_specs=[
            pl.BlockSpec((tm, 4 * LANES), lambda i: (i, COL_DQ // 4)),
            pl.BlockSpec((tm, LANES), lambda i: (i, COL_DK)),
            pl.BlockSpec((tm, 4 * LANES), lambda i: (i, COL_CV // 4)),
            pl.BlockSpec((tm, LANES), lambda i: (i, COL_DV)),
            pl.BlockSpec((tm, LANES), lambda i: (i, COL_AV)),
            pl.BlockSpec((tm, 4 * LANES), lambda i: (i, COL_CQ // 4)),
            pl.BlockSpec((tm, 4 * LANES), lambda i: (i, COL_CK // 4)),
            pl.BlockSpec((tm, LANES), lambda i: (i % per_seq, 0)),
            pl.BlockSpec((tm, LANES), lambda i: (i % per_seq, 0)),
            pl.BlockSpec((1, 4 * LANES), lambda i: (0, 0)),
            pl.BlockSpec((1, LANES), lambda i: (0, 0)),
            pl.BlockSpec((4 * LANES, 4 * LANES), lambda i: (0, 0)),
        ],
        out_specs=[
            pl.BlockSpec((tm, 4 * LANES), lambda i: (i, 0)),
            pl.BlockSpec((tm, LANES), lambda i: (i, 0)),
            pl.BlockSpec((1, C_HEADS, 1, rows_c, tm), lambda i: (i // per_seq, 0, i % per_seq, 0, 0)),
            pl.BlockSpec((1, 2, 1, KV_ROWS, tm), lambda i: (i // per_seq, 0, i % per_seq, 0, 0)),
            pl.BlockSpec((1, per_tile, 2, KV_ROWS, A_BLOCK), lambda i: (i // per_seq, i % per_seq, 0, 0, 0)),
            pl.BlockSpec((1, 8, 4 * LANES), lambda i: (i, 0, 0)),
            pl.BlockSpec((1, 8, 4 * LANES), lambda i: (i, 0, 0)),
        ],
        out_shape=[jax.ShapeDtypeStruct((m, 4 * LANES), BF16),
                   jax.ShapeDtypeStruct((m, LANES), BF16),
                   jax.ShapeDtypeStruct((bsz, C_HEADS, per_seq, rows_c, tm), BF16),
                   jax.ShapeDtypeStruct((bsz, 2, per_seq, KV_ROWS, tm), BF16),
                   jax.ShapeDtypeStruct((bsz, s_len // A_BLOCK, 2, KV_ROWS, A_BLOCK), BF16),
                   jax.ShapeDtypeStruct((m // tm, 8, 4 * LANES), F32),
                   jax.ShapeDtypeStruct((m // tm, 8, 4 * LANES), F32)],
        compiler_params=_params(("parallel",)),
        name="attn_prep",
    )(u2, u2, u2, u2, u2, u2, u2, cos, sin, jnp.tile(qnorm, 8).reshape(1, 4 * LANES),
      jnp.tile(knorm, 2).reshape(1, LANES), seg)


def _out_kernel(x_ref, ya_ref, yb_ref, yc_ref, yd_ref, pe_ref, wo_ref, gp_ref, wpe_ref, wpg_ref, o_ref):
    width = 4 * LANES
    z = _dot(ya_ref[...], wo_ref[0 * width:1 * width, :])
    z += _dot(yb_ref[...], wo_ref[1 * width:2 * width, :])
    z += _dot(yc_ref[...], wo_ref[2 * width:3 * width, :])
    z += _dot(yd_ref[...], wo_ref[3 * width:4 * width, :])
    ms = jnp.mean(z * z, axis=-1, keepdims=True)
    x = x_ref[...] + z * lax.rsqrt(ms + EPS) * gp_ref[...]
    logits = _dot(x.astype(BF16), wpg_ref[...])
    gate = 1.0 / (1.0 + jnp.exp(-logits))
    pe = _dot(pe_ref[...].astype(BF16), wpe_ref[...])
    o_ref[...] = x + pe * gate


def _out_stage(x2, ya, yb, yc, yd, pe2, w_o, g_post, w_pe, w_pg, tm):
    m = x2.shape[0]
    row = lambda w: pl.BlockSpec((tm, w), lambda i: (i, 0))
    const = lambda r, c: pl.BlockSpec((r, c), lambda i: (0, 0), pipeline_mode=pl.Buffered(1))
    return pl.pallas_call(
        _out_kernel,
        grid=(m // tm,),
        in_specs=[row(D_MODEL), row(4 * LANES), row(4 * LANES), row(4 * LANES), row(4 * LANES),
                  row(PLE_DIM), const(MIX_WIDTH, D_MODEL), const(1, D_MODEL),
                  const(PLE_DIM, D_MODEL), const(D_MODEL, D_MODEL)],
        out_specs=row(D_MODEL),
        out_shape=jax.ShapeDtypeStruct((m, D_MODEL), F32),
        compiler_params=_params(("parallel",)),
        name="out_stage",
    )(x2, ya, yb, yc, yd, pe2, w_o, g_post.reshape(1, D_MODEL), w_pe, w_pg)


def _prep_w_in(w):
    parts, start = [], 0
    for n in IN_SIZES:
        parts.append(w[:, start:start + n])
        start += n
    aq, ak, av, ag, bx, bg, cq, ck, cv, cg, dq, dk, dv, dg = parts

    log2_scale = SCALE * LOG2E
    cols = [aq * log2_scale, ag, bx, bg, cq * log2_scale, ck, cv, cg, dq, dg, ak, av, dk, dv]
    return jnp.concatenate(cols, axis=1).astype(BF16)


def _rel_bucket(rel):
    nb = REL_BUCKETS // 2
    max_exact = nb // 2
    ret = jnp.where(rel > 0, nb, 0)
    n = jnp.abs(rel)
    nf = jnp.maximum(n, 1).astype(F32)
    large = max_exact + (jnp.log(nf / max_exact) / math.log(REL_MAX_DIST / max_exact)
                         * (nb - max_exact)).astype(jnp.int32)
    large = jnp.minimum(large, nb - 1)
    return ret + jnp.where(n < max_exact, n, large)


def _bucket_values(tab, rel):
    bucket = _rel_bucket(rel)[None]
    out = jnp.zeros((tab.shape[1],) + rel.shape, F32)
    for b in range(REL_BUCKETS):
        out = jnp.where(bucket == b, tab[b].reshape((-1,) + (1,) * rel.ndim), out)
    return out


def _bias_tables(rel_bias, tq, tk):
    tab = rel_bias.astype(F32)
    ck = jnp.arange(3 * A_BLOCK)[:, None]
    rq = jnp.arange(A_BLOCK)[None, :]
    rel_a = ck - A_BLOCK - rq
    vals = _bucket_values(tab[:, :A_HEADS], rel_a) * LOG2E
    in_band = jnp.abs(rel_a) <= A_WINDOW
    variants = []
    for edge in (ck >= A_BLOCK, ck >= 0, ck < 2 * A_BLOCK):
        masked = jnp.where(in_band & edge, vals, NEG_INF)
        variants.append(jnp.transpose(masked.reshape(2, 4, 3 * A_BLOCK, A_BLOCK), (0, 2, 1, 3))
                        .reshape(2, 3 * A_BLOCK, 4 * A_BLOCK))
    bias_a = jnp.stack(variants)
    assert tk >= REL_MAX_DIST
    n_tab = tq // tk + 4
    d = jnp.arange(n_tab)[:, None, None]
    c = jnp.arange(tk)[None, :, None]
    r = jnp.arange(tq)[None, None, :]
    bias_c = _bucket_values(tab[:, A_HEADS:], c + (d - 2) * tk - r) * LOG2E
    return bias_a, bias_c


def _axial_rope(s_len):
    rows = s_len // GRID_W
    t_row = jnp.repeat(jnp.arange(rows), GRID_W).astype(F32)
    t_col = jnp.tile(jnp.arange(GRID_W), rows).astype(F32)
    half = HEAD_DIM // 2
    inv = ROPE_THETA ** (-jnp.arange(0, half, 2, dtype=F32) / half)
    ang_r = t_row[:, None] * inv[None, :]
    ang_c = t_col[:, None] * inv[None, :]
    ang = jnp.concatenate([ang_r, ang_r, ang_c, ang_c] * 2, axis=-1)
    return jnp.cos(ang), jnp.sin(ang)


def _tiles(s_len):
    return dict(tm_in=min(512, s_len), tn_in=U_WIDTH // 2, tq_pool=min(256, s_len), tm_out=min(256, s_len),
                tq_attn=512, tk_attn=256, nb_c=16, nb_d=16, ahead_c=2, ahead_d=3, tq_win=512, ahead_win=1)


def _layer(x, pe, layer_idx, tables, rope, w, tiles):
    bsz, s_len, _ = x.shape
    m = bsz * s_len
    bias_a, bias_c = tables
    cos, sin = rope
    tq, tk = tiles["tq_attn"], tiles["tk_attn"]
    x2 = x.reshape(m, D_MODEL)
    u2 = _in_proj(x2, w["g_pre"], w["w_in"], tiles["tm_in"], tiles["tn_in"])
    u3 = u2.reshape(bsz, s_len, U_WIDTH)
    qd, kd, vtc, vtd, vta, qn, kn = _attn_prep(u2, cos, sin, w["qnorm_d"], w["knorm_d"], bsz, s_len, tk)
    ya = _win_attn(u3, vta, bias_a, w["sink"], tiles["tq_win"], tiles["ahead_win"])
    yb = _pool(u3, w["pool_w"], w["pool_scale"], tiles["tq_pool"])
    lambda_init = 0.8 - 0.6 * math.exp(-0.3 * layer_idx)
    span = math.log2(s_len)

    seg_max = lambda t: jnp.sqrt(jnp.max(t.reshape(bsz, -1, 8, C_HEADS, 2, HEAD_DIM)[:, :, 0, :, :, 0], axis=1))
    q_max, k_max = seg_max(qn), seg_max(kn) * BOUND_SLACK
    b_hi, b_lo = jnp.max(bias_c, axis=(1, 2, 3)), jnp.min(bias_c, axis=(1, 2, 3))
    stat_c = jnp.concatenate([k_max, jnp.broadcast_to(b_hi[None, :, None], (bsz, C_HEADS, 1)),
                              jnp.zeros((bsz, C_HEADS, 5), F32)], axis=2)
    stat_c = jnp.broadcast_to(stat_c[..., None], (bsz, C_HEADS, 8, LANES))
    fits_c = 2.0 * jnp.max(q_max * k_max) + jnp.max(b_hi - b_lo) + span <= MAX_EXP2_SPAN
    attend_c = lambda bounded: _diff_attn(u3, vtc, bias_c, stat_c, w["lam"], w["diff_subln"], lambda_init,
                                          tq, tk, tiles["nb_c"], tiles["ahead_c"], bounded)
    yc = lax.cond(fits_c, lambda: attend_c(True), lambda: attend_c(False))

    bound_d = (HEAD_DIM * SCALE * LOG2E * BOUND_SLACK
               * jnp.max(jnp.abs(w["qnorm_d"])) * jnp.max(jnp.abs(w["knorm_d"])))
    fits_d = 2.0 * bound_d + span <= MAX_EXP2_SPAN
    attend_d = lambda bounded: _grid_attn(qd.reshape(bsz, s_len, -1), kd.reshape(bsz, s_len, -1), vtd, u3,
                                          jnp.full((8, LANES), bound_d, F32), tq, tk, tiles["nb_d"],
                                          tiles["ahead_d"], bounded)
    yd = lax.cond(fits_d, lambda: attend_d(True), lambda: attend_d(False))
    flat = lambda y: y.reshape(m, 4 * LANES)
    out = _out_stage(x2, flat(ya), flat(yb), flat(yc), flat(yd), pe.reshape(m, PLE_DIM),
                     w["w_o"], w["g_post"], w["w_pe"], w["w_pg"], tiles["tm_out"])
    return out.reshape(bsz, s_len, D_MODEL)


def _trunk(x, p, rel_bias, layers, tiles=None):
    s_len = x.shape[1]
    tiles = tiles or _tiles(s_len)
    tables = _bias_tables(rel_bias, tiles["tq_attn"], tiles["tk_attn"])
    rope = _axial_rope(s_len)
    for i, w in enumerate(layers):
        x = _layer(x, p[i], i, tables, rope, w, tiles)
    return x


def _prep_layers(w_in, w_o, g_pre, g_post, sink_a, pool_w, pool_scale, lam_q1, lam_k1, lam_q2, lam_k2,
                 diff_subln, qnorm_d, knorm_d, w_pe, w_pg):
    layers = []
    for i in range(w_in.shape[0]):
        layers.append(dict(
            w_in=_prep_w_in(w_in[i]), w_o=w_o[i].astype(BF16), g_pre=g_pre[i], g_post=g_post[i],
            sink=jnp.repeat(sink_a[i].astype(F32) * LOG2E, A_BLOCK).reshape(2, 1, 4 * A_BLOCK),
            pool_w=pool_w[i], pool_scale=pool_scale[i],
            lam=jnp.stack([lam_q1[i], lam_k1[i], lam_q2[i], lam_k2[i]]).astype(F32),
            diff_subln=diff_subln[i], qnorm_d=qnorm_d[i], knorm_d=knorm_d[i],
            w_pe=w_pe[i].astype(BF16), w_pg=w_pg[i].astype(BF16)))
    return layers


def kernel(x_prompt, x_sample, p_prompt, p_sample, w_in, w_o, g_pre, g_post, sink_a, pool_w, pool_scale,
           lam_q1, lam_k1, lam_q2, lam_k2, diff_subln, qnorm_d, knorm_d, rel_bias, w_pe, w_pg):
    layers = _prep_layers(w_in, w_o, g_pre, g_post, sink_a, pool_w, pool_scale, lam_q1, lam_k1, lam_q2,
                          lam_k2, diff_subln, qnorm_d, knorm_d, w_pe, w_pg)
    y_prompt = _trunk(x_prompt, p_prompt, rel_bias, layers)
    y_sample = _trunk(x_sample, p_sample, rel_bias, layers)
    return (y_prompt, y_sample)
```

```python
import functools
import math

import jax
import jax.numpy as jnp
from jax import lax
from jax.experimental import pallas as pl
from jax.experimental.pallas import tpu as pltpu

F32 = jnp.float32
BF16 = jnp.bfloat16

D_MODEL = 2048
HEAD_DIM = 64
LANES = 128
PLE_DIM = 256
EPS = 1e-6
NEG_INF = -1e30
SCALE = HEAD_DIM ** -0.5
LOG2E = math.log2(math.e)
VT_ONES = 16
KV_ROWS = HEAD_DIM + VT_ONES
BOUND_SLACK = 1.02
MAX_EXP2_SPAN = 100.0

A_HEADS = 8
A_WINDOW = 128
A_BLOCK = 128
B_GROUPS = 4
B_POOL_SIZES = (2, 4, 8, 16)
B_HALO = 64
C_HEADS = 4
REL_BUCKETS = 32
REL_MAX_DIST = 128
ROPE_THETA = 10000.0
GRID_W = 64

IN_SIZES = (512, 128, 128, 512, 512, 512, 512, 512, 512, 512, 512, 128, 128, 512)
MIX_WIDTH = 2048

U_WIDTH = 5632
COL_AQ, COL_AG, COL_BX, COL_BG = 0, 4, 8, 12
COL_CQ, COL_CK, COL_CV, COL_CG = 16, 20, 24, 28
COL_DQ, COL_DG = 32, 36
COL_AK, COL_AV, COL_DK, COL_DV = 40, 41, 42, 43

VMEM_LIMIT = 56 * 1024 * 1024


def _params(sem, vmem=VMEM_LIMIT):
    return pltpu.CompilerParams(dimension_semantics=sem, vmem_limit_bytes=vmem)


def _dot(a, b):
    return jnp.dot(a, b, preferred_element_type=F32)


def _dot_nt(a, b):
    return lax.dot_general(a, b, (((1,), (1,)), ((), ())), preferred_element_type=F32)


def _silu(x):
    return x / (1.0 + jnp.exp(-x))


def _lane_is_low(width=LANES):
    return (lax.broadcasted_iota(jnp.int32, (1, width), 1) % LANES) < HEAD_DIM


def _in_proj_kernel(x_ref, g_ref, w_ref, o_ref, h_ref):
    @pl.when(pl.program_id(1) == 0)
    def _():
        x = x_ref[...]
        ms = jnp.mean(x * x, axis=-1, keepdims=True)
        h_ref[...] = (x * lax.rsqrt(ms + EPS) * g_ref[...]).astype(BF16)

    o_ref[...] = _dot(h_ref[...], w_ref[...]).astype(o_ref.dtype)


def _in_proj(x2, g_pre, w_in_p, tm, tn):
    m = x2.shape[0]
    return pl.pallas_call(
        _in_proj_kernel,
        grid=(m // tm, U_WIDTH // tn),
        in_specs=[
            pl.BlockSpec((tm, D_MODEL), lambda i, j: (i, 0)),
            pl.BlockSpec((1, D_MODEL), lambda i, j: (0, 0)),
            pl.BlockSpec((D_MODEL, tn), lambda i, j: (0, j)),
        ],
        out_specs=pl.BlockSpec((tm, tn), lambda i, j: (i, j)),
        out_shape=jax.ShapeDtypeStruct((m, U_WIDTH), BF16),
        scratch_shapes=[pltpu.VMEM((tm, D_MODEL), BF16)],
        compiler_params=_params(("parallel", "arbitrary")),
        name="in_proj",
    )(x2, g_pre.reshape(1, D_MODEL), w_in_p)


def _group_queries(x, group, low):
    xr = pltpu.roll(x, HEAD_DIM, 1)
    zero = jnp.zeros_like(x)
    if group == 0:
        return [jnp.where(low, x, zero), jnp.where(low, xr, zero)]
    return [jnp.where(low, zero, xr), jnp.where(low, zero, x)]


def _win_attn_kernel(q_ref, kp_ref, kc_ref, kn_ref, vp_ref, vc_ref, vn_ref, g_ref, bias_ref,
                     sink_ref, o_ref, *, nq, ahead):
    i = pl.program_id(1)
    n = pl.num_programs(1)
    low = _lane_is_low()
    kext = jnp.concatenate([kp_ref[0], kc_ref[0], kn_ref[0]], axis=0)
    vts = [vp_ref[0, 0]] + [vc_ref[0, b] for b in range(nq)] + [vn_ref[0, 0]]
    units = [(qb, grp) for qb in range(nq) for grp in range(2)]

    def qk(u):
        qb, grp = units[u]
        rows = slice(qb * A_BLOCK, (qb + 1) * A_BLOCK)
        heads = []
        for pair in (2 * grp, 2 * grp + 1):
            heads += _group_queries(q_ref[0, rows, pair * LANES:(pair + 1) * LANES], grp, low)
        return _dot_nt(kext[qb * A_BLOCK:(qb + 3) * A_BLOCK], jnp.concatenate(heads, axis=0))

    def softmax(u, st):
        qb, grp = units[u]
        variant = 1
        if qb == 0:
            variant = jnp.where(i == 0, 0, variant)
        if qb == nq - 1:
            variant = jnp.where(i == n - 1, 2, variant)
        sb = _same_bits(st) + bias_ref[variant, grp]
        m = jnp.maximum(jnp.max(sb, axis=0, keepdims=True), sink_ref[grp])
        return m, jnp.exp2(sb - m).astype(BF16)

    def pv(u, probs):
        qb, grp = units[u]
        m, pt = probs
        vt = jnp.concatenate([vts[qb + d][grp] for d in range(3)], axis=1)
        acc = _dot(vt, pt)
        ot = acc[:HEAD_DIM] / (acc[HEAD_DIM:HEAD_DIM + 1] + jnp.exp2(sink_ref[grp] - m))
        rows = slice(qb * A_BLOCK, (qb + 1) * A_BLOCK)
        for pair in range(2):
            yt = jnp.concatenate([ot[:, (2 * pair) * A_BLOCK:(2 * pair + 1) * A_BLOCK],
                                  ot[:, (2 * pair + 1) * A_BLOCK:(2 * pair + 2) * A_BLOCK]], axis=0)
            cols = slice((2 * grp + pair) * LANES, (2 * grp + pair + 1) * LANES)
            gate = g_ref[0, rows, cols].astype(F32)
            o_ref[0, rows, cols] = (yt.T * _silu(gate)).astype(o_ref.dtype)

    _flash_units(len(units), ahead, qk, softmax, pv)


def _win_attn(u3, vta, bias_a, sink_a, tq, ahead):
    bsz, s_len, _ = u3.shape
    nq = tq // A_BLOCK
    nblk = s_len // A_BLOCK
    wide = (1, tq, 4 * LANES)
    const = lambda shape: pl.BlockSpec(shape, lambda bi, i: (0,) * len(shape), pipeline_mode=pl.Buffered(1))
    return pl.pallas_call(
        functools.partial(_win_attn_kernel, nq=nq, ahead=ahead),
        grid=(bsz, s_len // tq),
        in_specs=[
            pl.BlockSpec(wide, lambda bi, i: (bi, i, COL_AQ // 4)),
            pl.BlockSpec((1, A_BLOCK, LANES), lambda bi, i: (bi, jnp.maximum(i * nq - 1, 0), COL_AK)),
            pl.BlockSpec((1, tq, LANES), lambda bi, i: (bi, i, COL_AK)),
            pl.BlockSpec((1, A_BLOCK, LANES), lambda bi, i: (bi, jnp.minimum((i + 1) * nq, nblk - 1), COL_AK)),
            pl.BlockSpec((1, 1, 2, KV_ROWS, A_BLOCK), lambda bi, i: (bi, jnp.maximum(i * nq - 1, 0), 0, 0, 0)),
            pl.BlockSpec((1, nq, 2, KV_ROWS, A_BLOCK), lambda bi, i: (bi, i, 0, 0, 0)),
            pl.BlockSpec((1, 1, 2, KV_ROWS, A_BLOCK),
                         lambda bi, i: (bi, jnp.minimum((i + 1) * nq, nblk - 1), 0, 0, 0)),
            pl.BlockSpec(wide, lambda bi, i: (bi, i, COL_AG // 4)),
            const(bias_a.shape),
            const(sink_a.shape),
        ],
        out_specs=pl.BlockSpec(wide, lambda bi, i: (bi, i, 0)),
        out_shape=jax.ShapeDtypeStruct((bsz, s_len, 4 * LANES), BF16),
        compiler_params=_params(("parallel", "parallel")),
        name="win_attn",
    )(u3, u3, u3, u3, vta, vta, vta, u3, bias_a, sink_a)


def _pool_kernel(xp_ref, xc_ref, xn_ref, g_ref, band_ref, w_ref, sc_ref, o_ref, *, s_len, tq):
    i = pl.program_id(1)
    n = pl.num_programs(1)
    xp = xp_ref[0]
    xn = xn_ref[0]
    xp = jnp.where(i > 0, xp, jnp.zeros_like(xp))
    xn = jnp.where(i < n - 1, xn, jnp.zeros_like(xn))
    xc = xc_ref[0]
    xext = jnp.concatenate([xp, xc, xn], axis=0)
    t = i * tq + lax.broadcasted_iota(jnp.int32, (tq, 1), 0)
    for gi, size in enumerate(B_POOL_SIZES):
        lo = size // 2
        hi = size - lo - 1
        lanes = slice(gi * LANES, (gi + 1) * LANES)
        win_sum = _dot(band_ref[gi], xext[:, lanes])
        start = jnp.maximum(t - lo, 0)
        end = jnp.minimum(t + hi, s_len - 1) + 1
        pooled = win_sum / (end - start).astype(F32) - xc[:, lanes].astype(F32)
        y = _dot(pooled.astype(BF16), w_ref[gi]) * sc_ref[:, lanes]
        gate = g_ref[0, :, lanes].astype(F32)
        o_ref[0, :, lanes] = (y * _silu(gate)).astype(o_ref.dtype)


def _pool_bands(tq):
    r = jnp.arange(tq)[:, None]
    c = jnp.arange(tq + 2 * B_HALO)[None, :] - B_HALO
    bands = []
    for size in B_POOL_SIZES:
        lo = size // 2
        hi = size - lo - 1
        bands.append(((c - r >= -lo) & (c - r <= hi)).astype(BF16))
    return jnp.stack(bands)


def _pool(u3, pool_w, pool_scale, tq):
    bsz, s_len, _ = u3.shape
    nt = s_len // tq
    per = tq // B_HALO
    nh = s_len // B_HALO
    wide = (1, tq, 4 * LANES)
    halo = (1, B_HALO, 4 * LANES)
    return pl.pallas_call(
        functools.partial(_pool_kernel, s_len=s_len, tq=tq),
        grid=(bsz, nt),
        in_specs=[
            pl.BlockSpec(halo, lambda bi, i: (bi, jnp.maximum(i * per - 1, 0), COL_BX // 4)),
            pl.BlockSpec(wide, lambda bi, i: (bi, i, COL_BX // 4)),
            pl.BlockSpec(halo, lambda bi, i: (bi, jnp.minimum((i + 1) * per, nh - 1), COL_BX // 4)),
            pl.BlockSpec(wide, lambda bi, i: (bi, i, COL_BG // 4)),
            pl.BlockSpec((B_GROUPS, tq, tq + 2 * B_HALO), lambda bi, i: (0, 0, 0)),
            pl.BlockSpec((B_GROUPS, LANES, LANES), lambda bi, i: (0, 0, 0)),
            pl.BlockSpec((1, 4 * LANES), lambda bi, i: (0, 0)),
        ],
        out_specs=pl.BlockSpec(wide, lambda bi, i: (bi, i, 0)),
        out_shape=jax.ShapeDtypeStruct((bsz, s_len, 4 * LANES), BF16),
        compiler_params=_params(("parallel", "parallel")),
        name="pool",
    )(u3, u3, u3, u3, _pool_bands(tq), pool_w.astype(BF16), pool_scale.reshape(1, 4 * LANES))


def _softmax_t(st, m):
    m_new = jnp.maximum(m, jnp.max(st, axis=0, keepdims=True))
    return m_new, jnp.exp2(m - m_new), jnp.exp2(st - m_new).astype(BF16)


def _flash_units(n_units, ahead, qk, softmax, pv):
    scores = {u: qk(u) for u in range(min(ahead, n_units))}
    for u in range(n_units):
        probs = softmax(u, scores.pop(u))
        if u + ahead < n_units:
            scores[u + ahead] = qk(u + ahead)
        pv(u, probs)


def _same_bits(x):
    return lax.bitcast_convert_type(lax.bitcast_convert_type(x, jnp.int32), F32)


def _transposed(x):
    return x.astype(F32).T.astype(BF16)


def _masked_halves(q, low):
    zero = jnp.zeros_like(q)
    return jnp.where(low, q, zero), jnp.where(low, zero, q)


def _diff_attn_kernel(q_ref, k_ref, vt_ref, g_ref, bias_ref, stat_ref, lam_ref, sub_ref, o_ref,
                      *, tq, tk, nb, n_iter, ahead, lambda_init, bounded, n_sub):
    for sub in range(n_sub):
        _diff_attn_tile(q_ref, k_ref, vt_ref, g_ref, bias_ref, stat_ref, lam_ref, sub_ref, o_ref, sub,
                        pl.program_id(2) * n_sub + sub, tq=tq, tk=tk, nb=nb, n_iter=n_iter, ahead=ahead,
                        lambda_init=lambda_init, bounded=bounded)


def _diff_attn_tile(q_ref, k_ref, vt_ref, g_ref, bias_ref, stat_ref, lam_ref, sub_ref, o_ref, sub, i,
                    *, tq, tk, nb, n_iter, ahead, lambda_init, bounded):
    rows_q = slice(sub * tq, (sub + 1) * tq)
    ratio = tq // tk
    n_tab = bias_ref.shape[1]
    rows = vt_ref.shape[-2]
    low = _lane_is_low()
    q = q_ref[0, rows_q]
    qs = _masked_halves(q, low)
    bounds = None
    if bounded:
        qf = q.astype(F32)
        halves = jnp.concatenate([jnp.where(low, 1.0, 0.0), jnp.where(low, 0.0, 1.0)] +
                                 [jnp.zeros((1, LANES), F32)] * 6, axis=0).astype(BF16)
        norm2 = _dot_nt(halves, (qf * qf).astype(BF16))
        bounds = [jnp.sqrt(norm2[h:h + 1]) * stat_ref[0, 0, h:h + 1, 0:1] + stat_ref[0, 0, 2:3, 0:1]
                  for h in range(2)]

    def step(j, carry):
        state = list(carry)
        blocks = [nb * j + b for b in range(nb)]
        ks = [k_ref[0, pl.ds(pl.multiple_of(blk * tk, tk), tk), :] for blk in blocks]
        vts = [vt_ref[0, 0, blk] for blk in blocks]
        tabs = [jnp.clip(blk - ratio * i + 2, 0, n_tab - 1) for blk in blocks]

        def qk(u):
            return _dot_nt(ks[u // 2], qs[u % 2])

        def softmax(u, st):
            sb = _same_bits(st) + bias_ref[0, tabs[u // 2]]
            if bounded:
                return None, None, jnp.exp2(sb - bounds[u % 2]).astype(BF16)
            return _softmax_t(sb, state[u % 2][0])

        def pv(u, probs):
            m_new, alpha, pt = probs
            acc = state[u % 2][1]
            state[u % 2] = (m_new, (acc if bounded else alpha * acc) + _dot(vts[u // 2], pt))

        _flash_units(2 * nb, ahead, qk, softmax, pv)
        return tuple(state)

    m0 = None if bounded else jnp.full((1, tq), NEG_INF, F32)
    init = tuple((m0, jnp.zeros((rows, tq), F32)) for _ in range(2))
    (_, a1), (_, a2) = lax.fori_loop(0, n_iter, step, init)

    dot1 = jnp.sum(lam_ref[0:1, :] * lam_ref[1:2, :], axis=-1, keepdims=True)
    dot2 = jnp.sum(lam_ref[2:3, :] * lam_ref[3:4, :], axis=-1, keepdims=True)
    lam = jnp.exp(dot1) - jnp.exp(dot2) + lambda_init
    yt = a1[:LANES] / a1[LANES:LANES + 1] - lam * (a2[:LANES] / a2[LANES:LANES + 1])
    y = yt.T
    ms = jnp.mean(y * y, axis=-1, keepdims=True)
    y = y * lax.rsqrt(ms + EPS) * sub_ref[...] * (1.0 - lambda_init)
    o_ref[0, rows_q] = (y * _silu(g_ref[0, rows_q].astype(F32))).astype(o_ref.dtype)


def _diff_attn(u3, vtc, bias_c, stat_c, lam_vecs, subln, lambda_init, tq, tk, nb, ahead, bounded):
    bsz, s_len, _ = u3.shape
    nk = s_len // tk
    rows = vtc.shape[-2]
    n_tab = bias_c.shape[1]
    n_sub = 2 if s_len % (2 * tq) == 0 else 1
    return pl.pallas_call(
        functools.partial(_diff_attn_kernel, tq=tq, tk=tk, nb=nb, n_iter=nk // nb, ahead=ahead,
                          lambda_init=lambda_init, bounded=bounded, n_sub=n_sub),
        grid=(bsz, C_HEADS, s_len // (tq * n_sub)),
        in_specs=[
            pl.BlockSpec((1, tq * n_sub, LANES), lambda bi, h, i: (bi, i, COL_CQ + h)),
            pl.BlockSpec((1, s_len, LANES), lambda bi, h, i: (bi, 0, COL_CK + h)),
            pl.BlockSpec((1, 1, nk, rows, tk), lambda bi, h, i: (bi, h, 0, 0, 0)),
            pl.BlockSpec((1, tq * n_sub, LANES), lambda bi, h, i: (bi, i, COL_CG + h)),
            pl.BlockSpec((1, n_tab, tk, tq), lambda bi, h, i: (h, 0, 0, 0)),
            pl.BlockSpec((1, 1, 8, LANES), lambda bi, h, i: (bi, h, 0, 0)),
            pl.BlockSpec((4, HEAD_DIM), lambda bi, h, i: (0, 0)),
            pl.BlockSpec((1, LANES), lambda bi, h, i: (0, 0)),
        ],
        out_specs=pl.BlockSpec((1, tq * n_sub, LANES), lambda bi, h, i: (bi, i, h)),
        out_shape=jax.ShapeDtypeStruct((bsz, s_len, 4 * LANES), BF16),
        compiler_params=_params(("parallel", "parallel", "parallel")),
        name="diff_attn_bounded" if bounded else "diff_attn",
    )(u3, u3, vtc, u3, bias_c, stat_c, lam_vecs, subln.reshape(1, LANES))


def _grid_attn_kernel(q_ref, k_ref, vt_ref, g_ref, bound_ref, o_ref, *, n_sub, **tile_args):
    for sub in range(n_sub):
        _grid_attn_tile(q_ref, k_ref, vt_ref, g_ref, bound_ref, o_ref, sub, **tile_args)


def _grid_attn_tile(q_ref, k_ref, vt_ref, g_ref, bound_ref, o_ref, sub, *, tq, tk, nb, n_iter, ahead, bounded):
    rows_q = slice(sub * tq, (sub + 1) * tq)
    grp = pl.program_id(1)
    lane = lax.broadcasted_iota(jnp.int32, (1, LANES), 1)
    keep = (lane >= grp * HEAD_DIM) & (lane < (grp + 1) * HEAD_DIM)
    rows = vt_ref.shape[-2]
    bound = bound_ref[0:1, 0:1]
    qs = []
    for pair in range(2):
        x = q_ref[0, rows_q, pair * LANES:(pair + 1) * LANES]
        xr = pltpu.roll(x, HEAD_DIM, 1)
        zero = jnp.zeros_like(x)
        qs.append(_transposed(jnp.where(keep, jnp.where(grp == 0, x, xr), zero)))
        qs.append(_transposed(jnp.where(keep, jnp.where(grp == 0, xr, x), zero)))

    def step(j, carry):
        state = list(carry)
        blocks = [nb * j + b for b in range(nb)]
        ks = [k_ref[0, pl.ds(pl.multiple_of(blk * tk, tk), tk), :] for blk in blocks]
        vts = [vt_ref[0, 0, blk] for blk in blocks]

        def qk(u):
            return _dot(ks[u // 4], qs[u % 4])

        def softmax(u, st):
            if bounded:
                return None, None, jnp.exp2(st - bound).astype(BF16)
            return _softmax_t(st, state[u % 4][0])

        def pv(u, probs):
            m_new, alpha, pt = probs
            acc = state[u % 4][1]
            state[u % 4] = (m_new, (acc if bounded else alpha * acc) + _dot(vts[u // 4], pt))

        _flash_units(4 * nb, ahead, qk, softmax, pv)
        return tuple(state)

    m0 = None if bounded else jnp.full((1, tq), NEG_INF, F32)
    init = tuple((m0, jnp.zeros((rows, tq), F32)) for _ in range(4))
    state = lax.fori_loop(0, n_iter, step, init)
    for pair in range(2):
        a0, a1 = state[2 * pair][1], state[2 * pair + 1][1]
        yt = jnp.concatenate([a0[:HEAD_DIM] / a0[HEAD_DIM:HEAD_DIM + 1],
                              a1[:HEAD_DIM] / a1[HEAD_DIM:HEAD_DIM + 1]], axis=0)
        gate = g_ref[0, rows_q, pair * LANES:(pair + 1) * LANES].astype(F32)
        o_ref[0, rows_q, pair * LANES:(pair + 1) * LANES] = (yt.T * _silu(gate)).astype(o_ref.dtype)


def _grid_attn(qd3, kd3, vtd, u3, bound_d, tq, tk, nb, ahead, bounded):
    bsz, s_len, _ = qd3.shape
    nk = s_len // tk
    rows = vtd.shape[-2]
    n_sub = 2 if nk == nb and s_len % (2 * tq) == 0 else 1
    return pl.pallas_call(
        functools.partial(_grid_attn_kernel, tq=tq, tk=tk, nb=nb, n_iter=nk // nb, ahead=ahead,
                          bounded=bounded, n_sub=n_sub),
        grid=(bsz, 2, s_len // (tq * n_sub)),
        in_specs=[
            pl.BlockSpec((1, tq * n_sub, 2 * LANES), lambda bi, h, i: (bi, i, h)),
            pl.BlockSpec((1, s_len, LANES), lambda bi, h, i: (bi, 0, 0)),
            pl.BlockSpec((1, 1, nk, rows, tk), lambda bi, h, i: (bi, h, 0, 0, 0)),
            pl.BlockSpec((1, tq * n_sub, 2 * LANES), lambda bi, h, i: (bi, i, COL_DG // 2 + h)),
            pl.BlockSpec((8, LANES), lambda bi, h, i: (0, 0)),
        ],
        out_specs=pl.BlockSpec((1, tq * n_sub, 2 * LANES), lambda bi, h, i: (bi, i, h)),
        out_shape=jax.ShapeDtypeStruct((bsz, s_len, 4 * LANES), BF16),
        compiler_params=_params(("parallel", "parallel", "parallel")),
        name="grid_attn_bounded" if bounded else "grid_attn",
    )(qd3, kd3, vtd, u3, bound_d)


def _norm_rope(x, gain, cos, sin, seg):
    w = x.shape[-1]
    x2 = x * x
    hi = x2.astype(BF16)
    lo = (x2 - hi.astype(F32)).astype(BF16)
    ss = _dot(hi, seg) + _dot(lo, seg)
    y = x * lax.rsqrt(ss * (1.0 / HEAD_DIM) + EPS) * gain
    quarter = HEAD_DIM // 4
    ahead = pltpu.roll(y, w - quarter, 1)
    behind = pltpu.roll(y, quarter, 1)
    first = (lax.broadcasted_iota(jnp.int32, (1, w), 1) % (2 * quarter)) < quarter
    rot = jnp.where(first, -ahead, behind)
    reps = w // LANES
    cos_w = jnp.concatenate([cos] * reps, axis=1)
    sin_w = jnp.concatenate([sin] * reps, axis=1)
    return y * cos_w + rot * sin_w


def _attn_prep_kernel(q_ref, k_ref, vc_ref, vd_ref, va_ref, qc_ref, kc_ref, cos_ref, sin_ref, gq_ref, gk_ref,
                      seg_ref, qo_ref, ko_ref, vtc_ref, vtd_ref, vta_ref, qn_ref, kn_ref):
    cos = cos_ref[...]
    sin = sin_ref[...]
    seg = seg_ref[...]
    q = _norm_rope(q_ref[...].astype(F32), gq_ref[...], cos, sin, seg)
    qo_ref[...] = (q * (SCALE * LOG2E)).astype(qo_ref.dtype)
    k = _norm_rope(k_ref[...].astype(F32), gk_ref[...], cos, sin, seg[:LANES, :LANES])
    ko_ref[...] = k.astype(ko_ref.dtype)
    tm = q.shape[0]
    ones = jnp.ones((VT_ONES, tm), BF16)
    vct = vc_ref[...].astype(F32).T
    for h in range(C_HEADS):
        vtc_ref[0, h, 0] = jnp.concatenate([vct[h * LANES:(h + 1) * LANES].astype(BF16), ones], axis=0)
    vdt = vd_ref[...].astype(F32).T.astype(BF16)
    vat = va_ref[...].astype(F32).T.astype(BF16)
    for h in range(2):
        vtd_ref[0, h, 0] = jnp.concatenate([vdt[h * HEAD_DIM:(h + 1) * HEAD_DIM], ones], axis=0)
        for kb in range(tm // A_BLOCK):
            cols = slice(kb * A_BLOCK, (kb + 1) * A_BLOCK)
            vta_ref[0, kb, h] = jnp.concatenate([vat[h * HEAD_DIM:(h + 1) * HEAD_DIM, cols], ones[:, cols]], axis=0)
    for src, dst in ((qc_ref, qn_ref), (kc_ref, kn_ref)):
        x = src[...].astype(F32)
        seg_sums = _dot((x * x).astype(BF16), seg)
        dst[0] = jnp.broadcast_to(jnp.max(seg_sums, axis=0, keepdims=True), (8, 4 * LANES))


def _attn_prep(u2, cos, sin, qnorm, knorm, bsz, s_len, tm):
    m = u2.shape[0]
    per_seq = s_len // tm
    per_tile = tm // A_BLOCK
    lane = jnp.arange(4 * LANES)
    seg = (lane[:, None] // HEAD_DIM == lane[None, :] // HEAD_DIM).astype(BF16)
    rows_c = LANES + VT_ONES
    return pl.pallas_call(
        _attn_prep_kernel,
        grid=(m // tm,),
        in_specs=[
            pl.BlockSpec((tm, 4 * LANES), lambda i: (i, COL_DQ // 4)),
            pl.BlockSpec((tm, LANES), lambda i: (i, COL_DK)),
            pl.BlockSpec((tm, 4 * LANES), lambda i: (i, COL_CV // 4)),
            pl.BlockSpec((tm, LANES), lambda i: (i, COL_DV)),
            pl.BlockSpec((tm, LANES), lambda i: (i, COL_AV)),
            pl.BlockSpec((tm, 4 * LANES), lambda i: (i, COL_CQ // 4)),
            pl.BlockSpec((tm, 4 * LANES), lambda i: (i, COL_CK // 4)),
            pl.BlockSpec((tm, LANES), lambda i: (i % per_seq, 0)),
            pl.BlockSpec((tm, LANES), lambda i: (i % per_seq, 0)),
            pl.BlockSpec((1, 4 * LANES), lambda i: (0, 0)),
            pl.BlockSpec((1, LANES), lambda i: (0, 0)),
            pl.BlockSpec((4 * LANES, 4 * LANES), lambda i: (0, 0)),
        ],
        out_specs=[
            pl.BlockSpec((tm, 4 * LANES), lambda i: (i, 0)),
            pl.BlockSpec((tm, LANES), lambda i: (i, 0)),
            pl.BlockSpec((1, C_HEADS, 1, rows_c, tm), lambda i: (i // per_seq, 0, i % per_seq, 0, 0)),
            pl.BlockSpec((1, 2, 1, KV_ROWS, tm), lambda i: (i // per_seq, 0, i % per_seq, 0, 0)),
            pl.BlockSpec((1, per_tile, 2, KV_ROWS, A_BLOCK), lambda i: (i // per_seq, i % per_seq, 0, 0, 0)),
            pl.BlockSpec((1, 8, 4 * LANES), lambda i: (i, 0, 0)),
            pl.BlockSpec((1, 8, 4 * LANES), lambda i: (i, 0, 0)),
        ],
        out_shape=[jax.ShapeDtypeStruct((m, 4 * LANES), BF16),
                   jax.ShapeDtypeStruct((m, LANES), BF16),
                   jax.ShapeDtypeStruct((bsz, C_HEADS, per_seq, rows_c, tm), BF16),
                   jax.ShapeDtypeStruct((bsz, 2, per_seq, KV_ROWS, tm), BF16),
                   jax.ShapeDtypeStruct((bsz, s_len // A_BLOCK, 2, KV_ROWS, A_BLOCK), BF16),
                   jax.ShapeDtypeStruct((m // tm, 8, 4 * LANES), F32),
                   jax.ShapeDtypeStruct((m // tm, 8, 4 * LANES), F32)],
        compiler_params=_params(("parallel",)),
        name="attn_prep",
    )(u2, u2, u2, u2, u2, u2, u2, cos, sin, jnp.tile(qnorm, 8).reshape(1, 4 * LANES),
      jnp.tile(knorm, 2).reshape(1, LANES), seg)


def _out_kernel(x_ref, ya_ref, yb_ref, yc_ref, yd_ref, pe_ref, wo_ref, gp_ref, wpe_ref, wpg_ref, o_ref):
    width = 4 * LANES
    z = _dot(ya_ref[...], wo_ref[0 * width:1 * width, :])
    z += _dot(yb_ref[...], wo_ref[1 * width:2 * width, :])
    z += _dot(yc_ref[...], wo_ref[2 * width:3 * width, :])
    z += _dot(yd_ref[...], wo_ref[3 * width:4 * width, :])
    ms = jnp.mean(z * z, axis=-1, keepdims=True)
    x = x_ref[...] + z * lax.rsqrt(ms + EPS) * gp_ref[...]
    logits = _dot(x.astype(BF16), wpg_ref[...])
    gate = 1.0 / (1.0 + jnp.exp(-logits))
    pe = _dot(pe_ref[...].astype(BF16), wpe_ref[...])
    o_ref[...] = x + pe * gate


def _out_stage(x2, ya, yb, yc, yd, pe2, w_o, g_post, w_pe, w_pg, tm):
    m = x2.shape[0]
    row = lambda w: pl.BlockSpec((tm, w), lambda i: (i, 0))
    const = lambda r, c: pl.BlockSpec((r, c), lambda i: (0, 0), pipeline_mode=pl.Buffered(1))
    return pl.pallas_call(
        _out_kernel,
        grid=(m // tm,),
        in_specs=[row(D_MODEL), row(4 * LANES), row(4 * LANES), row(4 * LANES), row(4 * LANES),
                  row(PLE_DIM), const(MIX_WIDTH, D_MODEL), const(1, D_MODEL),
                  const(PLE_DIM, D_MODEL), const(D_MODEL, D_MODEL)],
        out_specs=row(D_MODEL),
        out_shape=jax.ShapeDtypeStruct((m, D_MODEL), F32),
        compiler_params=_params(("parallel",)),
        name="out_stage",
    )(x2, ya, yb, yc, yd, pe2, w_o, g_post.reshape(1, D_MODEL), w_pe, w_pg)


def _prep_w_in(w):
    parts, start = [], 0
    for n in IN_SIZES:
        parts.append(w[:, start:start + n])
        start += n
    aq, ak, av, ag, bx, bg, cq, ck, cv, cg, dq, dk, dv, dg = parts

    log2_scale = SCALE * LOG2E
    cols = [aq * log2_scale, ag, bx, bg, cq * log2_scale, ck, cv, cg, dq, dg, ak, av, dk, dv]
    return jnp.concatenate(cols, axis=1).astype(BF16)


def _rel_bucket(rel):
    nb = REL_BUCKETS // 2
    max_exact = nb // 2
    ret = jnp.where(rel > 0, nb, 0)
    n = jnp.abs(rel)
    nf = jnp.maximum(n, 1).astype(F32)
    large = max_exact + (jnp.log(nf / max_exact) / math.log(REL_MAX_DIST / max_exact)
                         * (nb - max_exact)).astype(jnp.int32)
    large = jnp.minimum(large, nb - 1)
    return ret + jnp.where(n < max_exact, n, large)


def _bucket_values(tab, rel):
    bucket = _rel_bucket(rel)[None]
    out = jnp.zeros((tab.shape[1],) + rel.shape, F32)
    for b in range(REL_BUCKETS):
        out = jnp.where(bucket == b, tab[b].reshape((-1,) + (1,) * rel.ndim), out)
    return out


def _bias_tables(rel_bias, tq, tk):
    tab = rel_bias.astype(F32)
    ck = jnp.arange(3 * A_BLOCK)[:, None]
    rq = jnp.arange(A_BLOCK)[None, :]
    rel_a = ck - A_BLOCK - rq
    vals = _bucket_values(tab[:, :A_HEADS], rel_a) * LOG2E
    in_band = jnp.abs(rel_a) <= A_WINDOW
    variants = []
    for edge in (ck >= A_BLOCK, ck >= 0, ck < 2 * A_BLOCK):
        masked = jnp.where(in_band & edge, vals, NEG_INF)
        variants.append(jnp.transpose(masked.reshape(2, 4, 3 * A_BLOCK, A_BLOCK), (0, 2, 1, 3))
                        .reshape(2, 3 * A_BLOCK, 4 * A_BLOCK))
    bias_a = jnp.stack(variants)
    assert tk >= REL_MAX_DIST
    n_tab = tq // tk + 4
    d = jnp.arange(n_tab)[:, None, None]
    c = jnp.arange(tk)[None, :, None]
    r = jnp.arange(tq)[None, None, :]
    bias_c = _bucket_values(tab[:, A_HEADS:], c + (d - 2) * tk - r) * LOG2E
    return bias_a, bias_c


def _axial_rope(s_len):
    rows = s_len // GRID_W
    t_row = jnp.repeat(jnp.arange(rows), GRID_W).astype(F32)
    t_col = jnp.tile(jnp.arange(GRID_W), rows).astype(F32)
    half = HEAD_DIM // 2
    inv = ROPE_THETA ** (-jnp.arange(0, half, 2, dtype=F32) / half)
    ang_r = t_row[:, None] * inv[None, :]
    ang_c = t_col[:, None] * inv[None, :]
    ang = jnp.concatenate([ang_r, ang_r, ang_c, ang_c] * 2, axis=-1)
    return jnp.cos(ang), jnp.sin(ang)


def _tiles(s_len):
    return dict(tm_in=min(512, s_len), tn_in=U_WIDTH // 2, tq_pool=min(256, s_len), tm_out=min(256, s_len),
                tq_attn=512, tk_attn=256, nb_c=16, nb_d=16, ahead_c=3, ahead_d=3, tq_win=512, ahead_win=1)


def _layer(x, pe, layer_idx, tables, rope, w, tiles):
    bsz, s_len, _ = x.shape
    m = bsz * s_len
    bias_a, bias_c = tables
    cos, sin = rope
    tq, tk = tiles["tq_attn"], tiles["tk_attn"]
    x2 = x.reshape(m, D_MODEL)
    u2 = _in_proj(x2, w["g_pre"], w["w_in"], tiles["tm_in"], tiles["tn_in"])
    u3 = u2.reshape(bsz, s_len, U_WIDTH)
    qd, kd, vtc, vtd, vta, qn, kn = _attn_prep(u2, cos, sin, w["qnorm_d"], w["knorm_d"], bsz, s_len, tk)
    ya = _win_attn(u3, vta, bias_a, w["sink"], tiles["tq_win"], tiles["ahead_win"])
    yb = _pool(u3, w["pool_w"], w["pool_scale"], tiles["tq_pool"])
    lambda_init = 0.8 - 0.6 * math.exp(-0.3 * layer_idx)
    span = math.log2(s_len)

    seg_max = lambda t: jnp.sqrt(jnp.max(t.reshape(bsz, -1, 8, C_HEADS, 2, HEAD_DIM)[:, :, 0, :, :, 0], axis=1))
    q_max, k_max = seg_max(qn), seg_max(kn) * BOUND_SLACK
    b_hi, b_lo = jnp.max(bias_c, axis=(1, 2, 3)), jnp.min(bias_c, axis=(1, 2, 3))
    stat_c = jnp.concatenate([k_max, jnp.broadcast_to(b_hi[None, :, None], (bsz, C_HEADS, 1)),
                              jnp.zeros((bsz, C_HEADS, 5), F32)], axis=2)
    stat_c = jnp.broadcast_to(stat_c[..., None], (bsz, C_HEADS, 8, LANES))
    fits_c = 2.0 * jnp.max(q_max * k_max) + jnp.max(b_hi - b_lo) + span <= MAX_EXP2_SPAN
    attend_c = lambda bounded: _diff_attn(u3, vtc, bias_c, stat_c, w["lam"], w["diff_subln"], lambda_init,
                                          tq, tk, tiles["nb_c"], tiles["ahead_c"], bounded)
    yc = lax.cond(fits_c, lambda: attend_c(True), lambda: attend_c(False))

    bound_d = (HEAD_DIM * SCALE * LOG2E * BOUND_SLACK
               * jnp.max(jnp.abs(w["qnorm_d"])) * jnp.max(jnp.abs(w["knorm_d"])))
    fits_d = 2.0 * bound_d + span <= MAX_EXP2_SPAN
    attend_d = lambda bounded: _grid_attn(qd.reshape(bsz, s_len, -1), kd.reshape(bsz, s_len, -1), vtd, u3,
                                          jnp.full((8, LANES), bound_d, F32), tq, tk, tiles["nb_d"],
                                          tiles["ahead_d"], bounded)
    yd = lax.cond(fits_d, lambda: attend_d(True), lambda: attend_d(False))
    flat = lambda y: y.reshape(m, 4 * LANES)
    out = _out_stage(x2, flat(ya), flat(yb), flat(yc), flat(yd), pe.reshape(m, PLE_DIM),
                     w["w_o"], w["g_post"], w["w_pe"], w["w_pg"], tiles["tm_out"])
    return out.reshape(bsz, s_len, D_MODEL)


def _trunk(x, p, rel_bias, layers, tiles=None):
    s_len = x.shape[1]
    tiles = tiles or _tiles(s_len)
    tables = _bias_tables(rel_bias, tiles["tq_attn"], tiles["tk_attn"])
    rope = _axial_rope(s_len)
    for i, w in enumerate(layers):
        x = _layer(x, p[i], i, tables, rope, w, tiles)
    return x


def _prep_layers(w_in, w_o, g_pre, g_post, sink_a, pool_w, pool_scale, lam_q1, lam_k1, lam_q2, lam_k2,
                 diff_subln, qnorm_d, knorm_d, w_pe, w_pg):
    layers = []
    for i in range(w_in.shape[0]):
        layers.append(dict(
            w_in=_prep_w_in(w_in[i]), w_o=w_o[i].astype(BF16), g_pre=g_pre[i], g_post=g_post[i],
            sink=jnp.repeat(sink_a[i].astype(F32) * LOG2E, A_BLOCK).reshape(2, 1, 4 * A_BLOCK),
            pool_w=pool_w[i], pool_scale=pool_scale[i],
            lam=jnp.stack([lam_q1[i], lam_k1[i], lam_q2[i], lam_k2[i]]).astype(F32),
            diff_subln=diff_subln[i], qnorm_d=qnorm_d[i], knorm_d=knorm_d[i],
            w_pe=w_pe[i].astype(BF16), w_pg=w_pg[i].astype(BF16)))
    return layers


def kernel(x_prompt, x_sample, p_prompt, p_sample, w_in, w_o, g_pre, g_post, sink_a, pool_w, pool_scale,
           lam_q1, lam_k1, lam_q2, lam_k2, diff_subln, qnorm_d, knorm_d, rel_bias, w_pe, w_pg):
    layers = _prep_layers(w_in, w_o, g_pre, g_post, sink_a, pool_w, pool_scale, lam_q1, lam_k1, lam_q2,
                          lam_k2, diff_subln, qnorm_d, knorm_d, w_pe, w_pg)
    y_prompt = _trunk(x_prompt, p_prompt, rel_bias, layers)
    y_sample = _trunk(x_sample, p_sample, rel_bias, layers)
    return (y_prompt, y_sample)
```

```python
import functools
import math

import jax
import jax.numpy as jnp
from jax import lax
from jax.experimental import pallas as pl
from jax.experimental.pallas import tpu as pltpu

F32 = jnp.float32
BF16 = jnp.bfloat16

D_MODEL = 2048
HEAD_DIM = 64
LANES = 128
PLE_DIM = 256
EPS = 1e-6
NEG_INF = -1e30
SCALE = HEAD_DIM ** -0.5
LOG2E = math.log2(math.e)
VT_ONES = 16
KV_ROWS = HEAD_DIM + VT_ONES
BOUND_SLACK = 1.02
MAX_EXP2_SPAN = 100.0

A_HEADS = 8
A_WINDOW = 128
A_BLOCK = 128
B_GROUPS = 4
B_POOL_SIZES = (2, 4, 8, 16)
B_HALO = 64
C_HEADS = 4
REL_BUCKETS = 32
REL_MAX_DIST = 128
ROPE_THETA = 10000.0
GRID_W = 64

IN_SIZES = (512, 128, 128, 512, 512, 512, 512, 512, 512, 512, 512, 128, 128, 512)
MIX_WIDTH = 2048

U_WIDTH = 5632
COL_AQ, COL_AG, COL_BX, COL_BG = 0, 4, 8, 12
COL_CQ, COL_CK, COL_CV, COL_CG = 16, 20, 24, 28
COL_DQ, COL_DG = 32, 36
COL_AK, COL_AV, COL_DK, COL_DV = 40, 41, 42, 43

VMEM_LIMIT = 56 * 1024 * 1024


def _params(sem, vmem=VMEM_LIMIT):
    return pltpu.CompilerParams(dimension_semantics=sem, vmem_limit_bytes=vmem)


def _dot(a, b):
    return jnp.dot(a, b, preferred_element_type=F32)


def _dot_nt(a, b):
    return lax.dot_general(a, b, (((1,), (1,)), ((), ())), preferred_element_type=F32)


def _silu(x):
    return x / (1.0 + jnp.exp(-x))


def _lane_is_low(width=LANES):
    return (lax.broadcasted_iota(jnp.int32, (1, width), 1) % LANES) < HEAD_DIM


def _in_proj_kernel(x_ref, g_ref, w_ref, o_ref, h_ref):
    @pl.when(pl.program_id(1) == 0)
    def _():
        x = x_ref[...]
        ms = jnp.mean(x * x, axis=-1, keepdims=True)
        h_ref[...] = (x * lax.rsqrt(ms + EPS) * g_ref[...]).astype(BF16)

    o_ref[...] = _dot(h_ref[...], w_ref[...]).astype(o_ref.dtype)


def _in_proj(x2, g_pre, w_in_p, tm, tn):
    m = x2.shape[0]
    return pl.pallas_call(
        _in_proj_kernel,
        grid=(m // tm, U_WIDTH // tn),
        in_specs=[
            pl.BlockSpec((tm, D_MODEL), lambda i, j: (i, 0)),
            pl.BlockSpec((1, D_MODEL), lambda i, j: (0, 0)),
            pl.BlockSpec((D_MODEL, tn), lambda i, j: (0, j)),
        ],
        out_specs=pl.BlockSpec((tm, tn), lambda i, j: (i, j)),
        out_shape=jax.ShapeDtypeStruct((m, U_WIDTH), BF16),
        scratch_shapes=[pltpu.VMEM((tm, D_MODEL), BF16)],
        compiler_params=_params(("parallel", "arbitrary")),
        name="in_proj",
    )(x2, g_pre.reshape(1, D_MODEL), w_in_p)


def _group_queries(x, group, low):
    xr = pltpu.roll(x, HEAD_DIM, 1)
    zero = jnp.zeros_like(x)
    if group == 0:
        return [jnp.where(low, x, zero), jnp.where(low, xr, zero)]
    return [jnp.where(low, zero, xr), jnp.where(low, zero, x)]


def _win_attn_kernel(q_ref, kp_ref, kc_ref, kn_ref, vp_ref, vc_ref, vn_ref, g_ref, bias_ref,
                     sink_ref, o_ref, *, nq, ahead):
    i = pl.program_id(1)
    n = pl.num_programs(1)
    low = _lane_is_low()
    kext = jnp.concatenate([kp_ref[0], kc_ref[0], kn_ref[0]], axis=0)
    vts = [vp_ref[0, 0]] + [vc_ref[0, b] for b in range(nq)] + [vn_ref[0, 0]]
    units = [(qb, grp) for qb in range(nq) for grp in range(2)]

    def qk(u):
        qb, grp = units[u]
        rows = slice(qb * A_BLOCK, (qb + 1) * A_BLOCK)
        heads = []
        for pair in (2 * grp, 2 * grp + 1):
            heads += _group_queries(q_ref[0, rows, pair * LANES:(pair + 1) * LANES], grp, low)
        return _dot_nt(kext[qb * A_BLOCK:(qb + 3) * A_BLOCK], jnp.concatenate(heads, axis=0))

    def softmax(u, st):
        qb, grp = units[u]
        variant = 1
        if qb == 0:
            variant = jnp.where(i == 0, 0, variant)
        if qb == nq - 1:
            variant = jnp.where(i == n - 1, 2, variant)
        sb = _same_bits(st) + bias_ref[variant, grp]
        m = jnp.maximum(jnp.max(sb, axis=0, keepdims=True), sink_ref[grp])
        return m, jnp.exp2(sb - m).astype(BF16)

    def pv(u, probs):
        qb, grp = units[u]
        m, pt = probs
        vt = jnp.concatenate([vts[qb + d][grp] for d in range(3)], axis=1)
        acc = _dot(vt, pt)
        ot = acc[:HEAD_DIM] / (acc[HEAD_DIM:HEAD_DIM + 1] + jnp.exp2(sink_ref[grp] - m))
        rows = slice(qb * A_BLOCK, (qb + 1) * A_BLOCK)
        for pair in range(2):
            yt = jnp.concatenate([ot[:, (2 * pair) * A_BLOCK:(2 * pair + 1) * A_BLOCK],
                                  ot[:, (2 * pair + 1) * A_BLOCK:(2 * pair + 2) * A_BLOCK]], axis=0)
            cols = slice((2 * grp + pair) * LANES, (2 * grp + pair + 1) * LANES)
            gate = g_ref[0, rows, cols].astype(F32)
            o_ref[0, rows, cols] = (yt.T * _silu(gate)).astype(o_ref.dtype)

    _flash_units(len(units), ahead, qk, softmax, pv)


def _win_attn(u3, vta, bias_a, sink_a, tq, ahead):
    bsz, s_len, _ = u3.shape
    nq = tq // A_BLOCK
    nblk = s_len // A_BLOCK
    wide = (1, tq, 4 * LANES)
    const = lambda shape: pl.BlockSpec(shape, lambda bi, i: (0,) * len(shape), pipeline_mode=pl.Buffered(1))
    return pl.pallas_call(
        functools.partial(_win_attn_kernel, nq=nq, ahead=ahead),
        grid=(bsz, s_len // tq),
        in_specs=[
            pl.BlockSpec(wide, lambda bi, i: (bi, i, COL_AQ // 4)),
            pl.BlockSpec((1, A_BLOCK, LANES), lambda bi, i: (bi, jnp.maximum(i * nq - 1, 0), COL_AK)),
            pl.BlockSpec((1, tq, LANES), lambda bi, i: (bi, i, COL_AK)),
            pl.BlockSpec((1, A_BLOCK, LANES), lambda bi, i: (bi, jnp.minimum((i + 1) * nq, nblk - 1), COL_AK)),
            pl.BlockSpec((1, 1, 2, KV_ROWS, A_BLOCK), lambda bi, i: (bi, jnp.maximum(i * nq - 1, 0), 0, 0, 0)),
            pl.BlockSpec((1, nq, 2, KV_ROWS, A_BLOCK), lambda bi, i: (bi, i, 0, 0, 0)),
            pl.BlockSpec((1, 1, 2, KV_ROWS, A_BLOCK),
                         lambda bi, i: (bi, jnp.minimum((i + 1) * nq, nblk - 1), 0, 0, 0)),
            pl.BlockSpec(wide, lambda bi, i: (bi, i, COL_AG // 4)),
            const(bias_a.shape),
            const(sink_a.shape),
        ],
        out_specs=pl.BlockSpec(wide, lambda bi, i: (bi, i, 0)),
        out_shape=jax.ShapeDtypeStruct((bsz, s_len, 4 * LANES), BF16),
        compiler_params=_params(("parallel", "parallel")),
        name="win_attn",
    )(u3, u3, u3, u3, vta, vta, vta, u3, bias_a, sink_a)


def _pool_kernel(xp_ref, xc_ref, xn_ref, g_ref, band_ref, w_ref, sc_ref, o_ref, *, s_len, tq):
    i = pl.program_id(1)
    n = pl.num_programs(1)
    xp = xp_ref[0]
    xn = xn_ref[0]
    xp = jnp.where(i > 0, xp, jnp.zeros_like(xp))
    xn = jnp.where(i < n - 1, xn, jnp.zeros_like(xn))
    xc = xc_ref[0]
    xext = jnp.concatenate([xp, xc, xn], axis=0)
    t = i * tq + lax.broadcasted_iota(jnp.int32, (tq, 1), 0)
    for gi, size in enumerate(B_POOL_SIZES):
        lo = size // 2
        hi = size - lo - 1
        lanes = slice(gi * LANES, (gi + 1) * LANES)
        win_sum = _dot(band_ref[gi], xext[:, lanes])
        start = jnp.maximum(t - lo, 0)
        end = jnp.minimum(t + hi, s_len - 1) + 1
        pooled = win_sum / (end - start).astype(F32) - xc[:, lanes].astype(F32)
        y = _dot(pooled.astype(BF16), w_ref[gi]) * sc_ref[:, lanes]
        gate = g_ref[0, :, lanes].astype(F32)
        o_ref[0, :, lanes] = (y * _silu(gate)).astype(o_ref.dtype)


def _pool_bands(tq):
    r = jnp.arange(tq)[:, None]
    c = jnp.arange(tq + 2 * B_HALO)[None, :] - B_HALO
    bands = []
    for size in B_POOL_SIZES:
        lo = size // 2
        hi = size - lo - 1
        bands.append(((c - r >= -lo) & (c - r <= hi)).astype(BF16))
    return jnp.stack(bands)


def _pool(u3, pool_w, pool_scale, tq):
    bsz, s_len, _ = u3.shape
    nt = s_len // tq
    per = tq // B_HALO
    nh = s_len // B_HALO
    wide = (1, tq, 4 * LANES)
    halo = (1, B_HALO, 4 * LANES)
    return pl.pallas_call(
        functools.partial(_pool_kernel, s_len=s_len, tq=tq),
        grid=(bsz, nt),
        in_specs=[
            pl.BlockSpec(halo, lambda bi, i: (bi, jnp.maximum(i * per - 1, 0), COL_BX // 4)),
            pl.BlockSpec(wide, lambda bi, i: (bi, i, COL_BX // 4)),
            pl.BlockSpec(halo, lambda bi, i: (bi, jnp.minimum((i + 1) * per, nh - 1), COL_BX // 4)),
            pl.BlockSpec(wide, lambda bi, i: (bi, i, COL_BG // 4)),
            pl.BlockSpec((B_GROUPS, tq, tq + 2 * B_HALO), lambda bi, i: (0, 0, 0)),
            pl.BlockSpec((B_GROUPS, LANES, LANES), lambda bi, i: (0, 0, 0)),
            pl.BlockSpec((1, 4 * LANES), lambda bi, i: (0, 0)),
        ],
        out_specs=pl.BlockSpec(wide, lambda bi, i: (bi, i, 0)),
        out_shape=jax.ShapeDtypeStruct((bsz, s_len, 4 * LANES), BF16),
        compiler_params=_params(("parallel", "parallel")),
        name="pool",
    )(u3, u3, u3, u3, _pool_bands(tq), pool_w.astype(BF16), pool_scale.reshape(1, 4 * LANES))


def _softmax_t(st, m):
    m_new = jnp.maximum(m, jnp.max(st, axis=0, keepdims=True))
    return m_new, jnp.exp2(m - m_new), jnp.exp2(st - m_new).astype(BF16)


def _flash_units(n_units, ahead, qk, softmax, pv):
    scores = {u: qk(u) for u in range(min(ahead, n_units))}
    for u in range(n_units):
        probs = softmax(u, scores.pop(u))
        if u + ahead < n_units:
            scores[u + ahead] = qk(u + ahead)
        pv(u, probs)


def _same_bits(x):
    return lax.bitcast_convert_type(lax.bitcast_convert_type(x, jnp.int32), F32)


def _transposed(x):
    return x.astype(F32).T.astype(BF16)


def _masked_halves(q, low):
    zero = jnp.zeros_like(q)
    return jnp.where(low, q, zero), jnp.where(low, zero, q)


def _diff_attn_kernel(q_ref, k_ref, vt_ref, g_ref, bias_ref, stat_ref, lam_ref, sub_ref, o_ref,
                      *, tq, tk, nb, n_iter, ahead, lambda_init, bounded, n_sub):
    for sub in range(n_sub):
        _diff_attn_tile(q_ref, k_ref, vt_ref, g_ref, bias_ref, stat_ref, lam_ref, sub_ref, o_ref, sub,
                        pl.program_id(2) * n_sub + sub, tq=tq, tk=tk, nb=nb, n_iter=n_iter, ahead=ahead,
                        lambda_init=lambda_init, bounded=bounded)


def _diff_attn_tile(q_ref, k_ref, vt_ref, g_ref, bias_ref, stat_ref, lam_ref, sub_ref, o_ref, sub, i,
                    *, tq, tk, nb, n_iter, ahead, lambda_init, bounded):
    rows_q = slice(sub * tq, (sub + 1) * tq)
    ratio = tq // tk
    n_tab = bias_ref.shape[1]
    rows = vt_ref.shape[-2]
    low = _lane_is_low()
    q = q_ref[0, rows_q]
    qs = _masked_halves(q, low)
    bounds = None
    if bounded:
        qf = q.astype(F32)
        halves = jnp.concatenate([jnp.where(low, 1.0, 0.0), jnp.where(low, 0.0, 1.0)] +
                                 [jnp.zeros((1, LANES), F32)] * 6, axis=0).astype(BF16)
        norm2 = _dot_nt(halves, (qf * qf).astype(BF16))
        bounds = [jnp.sqrt(norm2[h:h + 1]) * stat_ref[0, 0, h:h + 1, 0:1] + stat_ref[0, 0, 2:3, 0:1]
                  for h in range(2)]

    def step(j, carry):
        state = list(carry)
        blocks = [nb * j + b for b in range(nb)]
        ks = [k_ref[0, pl.ds(pl.multiple_of(blk * tk, tk), tk), :] for blk in blocks]
        vts = [vt_ref[0, 0, blk] for blk in blocks]
        tabs = [jnp.clip(blk - ratio * i + 2, 0, n_tab - 1) for blk in blocks]

        def qk(u):
            return _dot_nt(ks[u // 2], qs[u % 2])

        def softmax(u, st):
            sb = _same_bits(st) + bias_ref[0, tabs[u // 2]]
            if bounded:
                return None, None, jnp.exp2(sb - bounds[u % 2]).astype(BF16)
            return _softmax_t(sb, state[u % 2][0])

        def pv(u, probs):
            m_new, alpha, pt = probs
            acc = state[u % 2][1]
            state[u % 2] = (m_new, (acc if bounded else alpha * acc) + _dot(vts[u // 2], pt))

        _flash_units(2 * nb, ahead, qk, softmax, pv)
        return tuple(state)

    m0 = None if bounded else jnp.full((1, tq), NEG_INF, F32)
    init = tuple((m0, jnp.zeros((rows, tq), F32)) for _ in range(2))
    (_, a1), (_, a2) = lax.fori_loop(0, n_iter, step, init)

    dot1 = jnp.sum(lam_ref[0:1, :] * lam_ref[1:2, :], axis=-1, keepdims=True)
    dot2 = jnp.sum(lam_ref[2:3, :] * lam_ref[3:4, :], axis=-1, keepdims=True)
    lam = jnp.exp(dot1) - jnp.exp(dot2) + lambda_init
    yt = a1[:LANES] / a1[LANES:LANES + 1] - lam * (a2[:LANES] / a2[LANES:LANES + 1])
    y = yt.T
    ms = jnp.mean(y * y, axis=-1, keepdims=True)
    y = y * lax.rsqrt(ms + EPS) * sub_ref[...] * (1.0 - lambda_init)
    o_ref[0, rows_q] = (y * _silu(g_ref[0, rows_q].astype(F32))).astype(o_ref.dtype)


def _diff_attn(u3, vtc, bias_c, stat_c, lam_vecs, subln, lambda_init, tq, tk, nb, ahead, bounded):
    bsz, s_len, _ = u3.shape
    nk = s_len // tk
    rows = vtc.shape[-2]
    n_tab = bias_c.shape[1]
    n_sub = 2 if s_len % (2 * tq) == 0 else 1
    return pl.pallas_call(
        functools.partial(_diff_attn_kernel, tq=tq, tk=tk, nb=nb, n_iter=nk // nb, ahead=ahead,
                          lambda_init=lambda_init, bounded=bounded, n_sub=n_sub),
        grid=(bsz, C_HEADS, s_len // (tq * n_sub)),
        in_specs=[
            pl.BlockSpec((1, tq * n_sub, LANES), lambda bi, h, i: (bi, i, COL_CQ + h)),
            pl.BlockSpec((1, s_len, LANES), lambda bi, h, i: (bi, 0, COL_CK + h)),
            pl.BlockSpec((1, 1, nk, rows, tk), lambda bi, h, i: (bi, h, 0, 0, 0)),
            pl.BlockSpec((1, tq * n_sub, LANES), lambda bi, h, i: (bi, i, COL_CG + h)),
            pl.BlockSpec((1, n_tab, tk, tq), lambda bi, h, i: (h, 0, 0, 0)),
            pl.BlockSpec((1, 1, 8, LANES), lambda bi, h, i: (bi, h, 0, 0)),
            pl.BlockSpec((4, HEAD_DIM), lambda bi, h, i: (0, 0)),
            pl.BlockSpec((1, LANES), lambda bi, h, i: (0, 0)),
        ],
        out_specs=pl.BlockSpec((1, tq * n_sub, LANES), lambda bi, h, i: (bi, i, h)),
        out_shape=jax.ShapeDtypeStruct((bsz, s_len, 4 * LANES), BF16),
        compiler_params=_params(("parallel", "parallel", "parallel")),
        name="diff_attn_bounded" if bounded else "diff_attn",
    )(u3, u3, vtc, u3, bias_c, stat_c, lam_vecs, subln.reshape(1, LANES))


def _grid_attn_kernel(q_ref, k_ref, vt_ref, g_ref, bound_ref, o_ref, *, n_sub, **tile_args):
    for sub in range(n_sub):
        _grid_attn_tile(q_ref, k_ref, vt_ref, g_ref, bound_ref, o_ref, sub, **tile_args)


def _grid_attn_tile(q_ref, k_ref, vt_ref, g_ref, bound_ref, o_ref, sub, *, tq, tk, nb, n_iter, ahead, bounded):
    rows_q = slice(sub * tq, (sub + 1) * tq)
    grp = pl.program_id(1)
    lane = lax.broadcasted_iota(jnp.int32, (1, LANES), 1)
    keep = (lane >= grp * HEAD_DIM) & (lane < (grp + 1) * HEAD_DIM)
    rows = vt_ref.shape[-2]
    bound = bound_ref[0:1, 0:1]
    qs = []
    for pair in range(2):
        x = q_ref[0, rows_q, pair * LANES:(pair + 1) * LANES]
        xr = pltpu.roll(x, HEAD_DIM, 1)
        zero = jnp.zeros_like(x)
        qs.append(_transposed(jnp.where(keep, jnp.where(grp == 0, x, xr), zero)))
        qs.append(_transposed(jnp.where(keep, jnp.where(grp == 0, xr, x), zero)))

    def step(j, carry):
        state = list(carry)
        blocks = [nb * j + b for b in range(nb)]
        ks = [k_ref[0, pl.ds(pl.multiple_of(blk * tk, tk), tk), :] for blk in blocks]
        vts = [vt_ref[0, 0, blk] for blk in blocks]

        def qk(u):
            return _dot(ks[u // 4], qs[u % 4])

        def softmax(u, st):
            if bounded:
                return None, None, jnp.exp2(st - bound).astype(BF16)
            return _softmax_t(st, state[u % 4][0])

        def pv(u, probs):
            m_new, alpha, pt = probs
            acc = state[u % 4][1]
            state[u % 4] = (m_new, (acc if bounded else alpha * acc) + _dot(vts[u // 4], pt))

        _flash_units(4 * nb, ahead, qk, softmax, pv)
        return tuple(state)

    m0 = None if bounded else jnp.full((1, tq), NEG_INF, F32)
    init = tuple((m0, jnp.zeros((rows, tq), F32)) for _ in range(4))
    state = lax.fori_loop(0, n_iter, step, init)
    for pair in range(2):
        a0, a1 = state[2 * pair][1], state[2 * pair + 1][1]
        yt = jnp.concatenate([a0[:HEAD_DIM] / a0[HEAD_DIM:HEAD_DIM + 1],
                              a1[:HEAD_DIM] / a1[HEAD_DIM:HEAD_DIM + 1]], axis=0)
        gate = g_ref[0, rows_q, pair * LANES:(pair + 1) * LANES].astype(F32)
        o_ref[0, rows_q, pair * LANES:(pair + 1) * LANES] = (yt.T * _silu(gate)).astype(o_ref.dtype)


def _grid_attn(qd3, kd3, vtd, u3, bound_d, tq, tk, nb, ahead, bounded):
    bsz, s_len, _ = qd3.shape
    nk = s_len // tk
    rows = vtd.shape[-2]
    n_sub = 2 if nk == nb and s_len % (2 * tq) == 0 else 1
    return pl.pallas_call(
        functools.partial(_grid_attn_kernel, tq=tq, tk=tk, nb=nb, n_iter=nk // nb, ahead=ahead,
                          bounded=bounded, n_sub=n_sub),
        grid=(bsz, 2, s_len // (tq * n_sub)),
        in_specs=[
            pl.BlockSpec((1, tq * n_sub, 2 * LANES), lambda bi, h, i: (bi, i, h)),
            pl.BlockSpec((1, s_len, LANES), lambda bi, h, i: (bi, 0, 0)),
            pl.BlockSpec((1, 1, nk, rows, tk), lambda bi, h, i: (bi, h, 0, 0, 0)),
            pl.BlockSpec((1, tq * n_sub, 2 * LANES), lambda bi, h, i: (bi, i, COL_DG // 2 + h)),
            pl.BlockSpec((8, LANES), lambda bi, h, i: (0, 0)),
        ],
        out_specs=pl.BlockSpec((1, tq * n_sub, 2 * LANES), lambda bi, h, i: (bi, i, h)),
        out_shape=jax.ShapeDtypeStruct((bsz, s_len, 4 * LANES), BF16),
        compiler_params=_params(("parallel", "parallel", "parallel")),
        name="grid_attn_bounded" if bounded else "grid_attn",
    )(qd3, kd3, vtd, u3, bound_d)


def _norm_rope(x, gain, cos, sin, seg):
    w = x.shape[-1]
    x2 = x * x
    hi = x2.astype(BF16)
    lo = (x2 - hi.astype(F32)).astype(BF16)
    ss = _dot(hi, seg) + _dot(lo, seg)
    y = x * lax.rsqrt(ss * (1.0 / HEAD_DIM) + EPS) * gain
    quarter = HEAD_DIM // 4
    ahead = pltpu.roll(y, w - quarter, 1)
    behind = pltpu.roll(y, quarter, 1)
    first = (lax.broadcasted_iota(jnp.int32, (1, w), 1) % (2 * quarter)) < quarter
    rot = jnp.where(first, -ahead, behind)
    reps = w // LANES
    cos_w = jnp.concatenate([cos] * reps, axis=1)
    sin_w = jnp.concatenate([sin] * reps, axis=1)
    return y * cos_w + rot * sin_w


def _attn_prep_kernel(q_ref, k_ref, vc_ref, vd_ref, va_ref, qc_ref, kc_ref, cos_ref, sin_ref, gq_ref, gk_ref,
                      seg_ref, qo_ref, ko_ref, vtc_ref, vtd_ref, vta_ref, qn_ref, kn_ref):
    cos = cos_ref[...]
    sin = sin_ref[...]
    seg = seg_ref[...]
    q = _norm_rope(q_ref[...].astype(F32), gq_ref[...], cos, sin, seg)
    qo_ref[...] = (q * (SCALE * LOG2E)).astype(qo_ref.dtype)
    k = _norm_rope(k_ref[...].astype(F32), gk_ref[...], cos, sin, seg[:LANES, :LANES])
    ko_ref[...] = k.astype(ko_ref.dtype)
    tm = q.shape[0]
    ones = jnp.ones((VT_ONES, tm), BF16)
    vct = vc_ref[...].astype(F32).T
    for h in range(C_HEADS):
        vtc_ref[0, h, 0] = jnp.concatenate([vct[h * LANES:(h + 1) * LANES].astype(BF16), ones], axis=0)
    vdt = vd_ref[...].astype(F32).T.astype(BF16)
    vat = va_ref[...].astype(F32).T.astype(BF16)
    for h in range(2):
        vtd_ref[0, h, 0] = jnp.concatenate([vdt[h * HEAD_DIM:(h + 1) * HEAD_DIM], ones], axis=0)
        for kb in range(tm // A_BLOCK):
            cols = slice(kb * A_BLOCK, (kb + 1) * A_BLOCK)
            vta_ref[0, kb, h] = jnp.concatenate([vat[h * HEAD_DIM:(h + 1) * HEAD_DIM, cols], ones[:, cols]], axis=0)
    for src, dst in ((qc_ref, qn_ref), (kc_ref, kn_ref)):
        x = src[...].astype(F32)
        seg_sums = _dot((x * x).astype(BF16), seg)
        dst[0] = jnp.broadcast_to(jnp.max(seg_sums, axis=0, keepdims=True), (8, 4 * LANES))


def _attn_prep(u2, cos, sin, qnorm, knorm, bsz, s_len, tm):
    m = u2.shape[0]
    per_seq = s_len // tm
    per_tile = tm // A_BLOCK
    lane = jnp.arange(4 * LANES)
    seg = (lane[:, None] // HEAD_DIM == lane[None, :] // HEAD_DIM).astype(BF16)
    rows_c = LANES + VT_ONES
    return pl.pallas_call(
        _attn_prep_kernel,
        grid=(m // tm,),
        in_specs=[
            pl.BlockSpec((tm, 4 * LANES), lambda i: (i, COL_DQ // 4)),
            pl.BlockSpec((tm, LANES), lambda i: (i, COL_DK)),
            pl.BlockSpec((tm, 4 * LANES), lambda i: (i, COL_CV // 4)),
            pl.BlockSpec((tm, LANES), lambda i: (i, COL_DV)),
            pl.BlockSpec((tm, LANES), lambda i: (i, COL_AV)),
            pl.BlockSpec((tm, 4 * LANES), lambda i: (i, COL_CQ // 4)),
            pl.BlockSpec((tm, 4 * LANES), lambda i: (i, COL_CK // 4)),
            pl.BlockSpec((tm, LANES), lambda i: (i % per_seq, 0)),
            pl.BlockSpec((tm, LANES), lambda i: (i % per_seq, 0)),
            pl.BlockSpec((1, 4 * LANES), lambda i: (0, 0)),
            pl.BlockSpec((1, LANES), lambda i: (0, 0)),
            pl.BlockSpec((4 * LANES, 4 * LANES), lambda i: (0, 0)),
        ],
        out_specs=[
            pl.BlockSpec((tm, 4 * LANES), lambda i: (i, 0)),
            pl.BlockSpec((tm, LANES), lambda i: (i, 0)),
            pl.BlockSpec((1, C_HEADS, 1, rows_c, tm), lambda i: (i // per_seq, 0, i % per_seq, 0, 0)),
            pl.BlockSpec((1, 2, 1, KV_ROWS, tm), lambda i: (i // per_seq, 0, i % per_seq, 0, 0)),
            pl.BlockSpec((1, per_tile, 2, KV_ROWS, A_BLOCK), lambda i: (i // per_seq, i % per_seq, 0, 0, 0)),
            pl.BlockSpec((1, 8, 4 * LANES), lambda i: (i, 0, 0)),
            pl.BlockSpec((1, 8, 4 * LANES), lambda i: (i, 0, 0)),
        ],
        out_shape=[jax.ShapeDtypeStruct((m, 4 * LANES), BF16),
                   jax.ShapeDtypeStruct((m, LANES), BF16),
                   jax.ShapeDtypeStruct((bsz, C_HEADS, per_seq, rows_c, tm), BF16),
                   jax.ShapeDtypeStruct((bsz, 2, per_seq, KV_ROWS, tm), BF16),
                   jax.ShapeDtypeStruct((bsz, s_len // A_BLOCK, 2, KV_ROWS, A_BLOCK), BF16),
                   jax.ShapeDtypeStruct((m // tm, 8, 4 * LANES), F32),
                   jax.ShapeDtypeStruct((m // tm, 8, 4 * LANES), F32)],
        compiler_params=_params(("parallel",)),
        name="attn_prep",
    )(u2, u2, u2, u2, u2, u2, u2, cos, sin, jnp.tile(qnorm, 8).reshape(1, 4 * LANES),
      jnp.tile(knorm, 2).reshape(1, LANES), seg)


def _out_kernel(x_ref, ya_ref, yb_ref, yc_ref, yd_ref, pe_ref, wo_ref, gp_ref, wpe_ref, wpg_ref, o_ref):
    width = 4 * LANES
    z = _dot(ya_ref[...], wo_ref[0 * width:1 * width, :])
    z += _dot(yb_ref[...], wo_ref[1 * width:2 * width, :])
    z += _dot(yc_ref[...], wo_ref[2 * width:3 * width, :])
    z += _dot(yd_ref[...], wo_ref[3 * width:4 * width, :])
    ms = jnp.mean(z * z, axis=-1, keepdims=True)
    x = x_ref[...] + z * lax.rsqrt(ms + EPS) * gp_ref[...]
    logits = _dot(x.astype(BF16), wpg_ref[...])
    gate = 1.0 / (1.0 + jnp.exp(-logits))
    pe = _dot(pe_ref[0].astype(BF16), wpe_ref[...])
    o_ref[...] = x + pe * gate


def _out_stage(x2, ya, yb, yc, yd, p3, layer_idx, w_o, g_post, w_pe, w_pg, tm):
    m = x2.shape[0]
    row = lambda w: pl.BlockSpec((tm, w), lambda i: (i, 0))
    const = lambda r, c: pl.BlockSpec((r, c), lambda i: (0, 0), pipeline_mode=pl.Buffered(1))
    return pl.pallas_call(
        _out_kernel,
        grid=(m // tm,),
        in_specs=[row(D_MODEL), row(4 * LANES), row(4 * LANES), row(4 * LANES), row(4 * LANES),
                  pl.BlockSpec((1, tm, PLE_DIM), lambda i: (layer_idx, i, 0)), const(MIX_WIDTH, D_MODEL),
                  const(1, D_MODEL),
                  const(PLE_DIM, D_MODEL), const(D_MODEL, D_MODEL)],
        out_specs=row(D_MODEL),
        out_shape=jax.ShapeDtypeStruct((m, D_MODEL), F32),
        compiler_params=_params(("parallel",)),
        name="out_stage",
    )(x2, ya, yb, yc, yd, p3, w_o, g_post.reshape(1, D_MODEL), w_pe, w_pg)


def _prep_w_in(w):
    parts, start = [], 0
    for n in IN_SIZES:
        parts.append(w[:, start:start + n])
        start += n
    aq, ak, av, ag, bx, bg, cq, ck, cv, cg, dq, dk, dv, dg = parts

    log2_scale = SCALE * LOG2E
    cols = [aq * log2_scale, ag, bx, bg, cq * log2_scale, ck, cv, cg, dq, dg, ak, av, dk, dv]
    return jnp.concatenate(cols, axis=1).astype(BF16)


def _rel_bucket(rel):
    nb = REL_BUCKETS // 2
    max_exact = nb // 2
    ret = jnp.where(rel > 0, nb, 0)
    n = jnp.abs(rel)
    nf = jnp.maximum(n, 1).astype(F32)
    large = max_exact + (jnp.log(nf / max_exact) / math.log(REL_MAX_DIST / max_exact)
                         * (nb - max_exact)).astype(jnp.int32)
    large = jnp.minimum(large, nb - 1)
    return ret + jnp.where(n < max_exact, n, large)


def _bucket_values(tab, rel):
    bucket = _rel_bucket(rel)[None]
    out = jnp.zeros((tab.shape[1],) + rel.shape, F32)
    for b in range(REL_BUCKETS):
        out = jnp.where(bucket == b, tab[b].reshape((-1,) + (1,) * rel.ndim), out)
    return out


def _bias_tables(rel_bias, tq, tk):
    tab = rel_bias.astype(F32)
    ck = jnp.arange(3 * A_BLOCK)[:, None]
    rq = jnp.arange(A_BLOCK)[None, :]
    rel_a = ck - A_BLOCK - rq
    vals = _bucket_values(tab[:, :A_HEADS], rel_a) * LOG2E
    in_band = jnp.abs(rel_a) <= A_WINDOW
    variants = []
    for edge in (ck >= A_BLOCK, ck >= 0, ck < 2 * A_BLOCK):
        masked = jnp.where(in_band & edge, vals, NEG_INF)
        variants.append(jnp.transpose(masked.reshape(2, 4, 3 * A_BLOCK, A_BLOCK), (0, 2, 1, 3))
                        .reshape(2, 3 * A_BLOCK, 4 * A_BLOCK))
    bias_a = jnp.stack(variants)
    assert tk >= REL_MAX_DIST
    n_tab = tq // tk + 4
    d = jnp.arange(n_tab)[:, None, None]
    c = jnp.arange(tk)[None, :, None]
    r = jnp.arange(tq)[None, None, :]
    bias_c = _bucket_values(tab[:, A_HEADS:], c + (d - 2) * tk - r) * LOG2E
    return bias_a, bias_c


def _axial_rope(s_len):
    rows = s_len // GRID_W
    t_row = jnp.repeat(jnp.arange(rows), GRID_W).astype(F32)
    t_col = jnp.tile(jnp.arange(GRID_W), rows).astype(F32)
    half = HEAD_DIM // 2
    inv = ROPE_THETA ** (-jnp.arange(0, half, 2, dtype=F32) / half)
    ang_r = t_row[:, None] * inv[None, :]
    ang_c = t_col[:, None] * inv[None, :]
    ang = jnp.concatenate([ang_r, ang_r, ang_c, ang_c] * 2, axis=-1)
    return jnp.cos(ang), jnp.sin(ang)


def _tiles(s_len):
    return dict(tm_in=min(512, s_len), tn_in=U_WIDTH // 2, tq_pool=min(256, s_len), tm_out=min(256, s_len),
                tq_attn=512, tk_attn=256, nb_c=16, nb_d=16, ahead_c=3, ahead_d=3, tq_win=1024, ahead_win=1)


def _layer(x, p, layer_idx, tables, rope, w, tiles):
    bsz, s_len, _ = x.shape
    m = bsz * s_len
    bias_a, bias_c = tables
    cos, sin = rope
    tq, tk = tiles["tq_attn"], tiles["tk_attn"]
    x2 = x.reshape(m, D_MODEL)
    u2 = _in_proj(x2, w["g_pre"], w["w_in"], tiles["tm_in"], tiles["tn_in"])
    u3 = u2.reshape(bsz, s_len, U_WIDTH)
    qd, kd, vtc, vtd, vta, qn, kn = _attn_prep(u2, cos, sin, w["qnorm_d"], w["knorm_d"], bsz, s_len, tk)
    ya = _win_attn(u3, vta, bias_a, w["sink"], tiles["tq_win"], tiles["ahead_win"])
    yb = _pool(u3, w["pool_w"], w["pool_scale"], tiles["tq_pool"])
    lambda_init = 0.8 - 0.6 * math.exp(-0.3 * layer_idx)
    span = math.log2(s_len)

    seg_max = lambda t: jnp.sqrt(jnp.max(t.reshape(bsz, -1, 8, C_HEADS, 2, HEAD_DIM)[:, :, 0, :, :, 0], axis=1))
    q_max, k_max = seg_max(qn), seg_max(kn) * BOUND_SLACK
    b_hi, b_lo = jnp.max(bias_c, axis=(1, 2, 3)), jnp.min(bias_c, axis=(1, 2, 3))
    stat_c = jnp.concatenate([k_max, jnp.broadcast_to(b_hi[None, :, None], (bsz, C_HEADS, 1)),
                              jnp.zeros((bsz, C_HEADS, 5), F32)], axis=2)
    stat_c = jnp.broadcast_to(stat_c[..., None], (bsz, C_HEADS, 8, LANES))
    fits_c = 2.0 * jnp.max(q_max * k_max) + jnp.max(b_hi - b_lo) + span <= MAX_EXP2_SPAN
    attend_c = lambda bounded: _diff_attn(u3, vtc, bias_c, stat_c, w["lam"], w["diff_subln"], lambda_init,
                                          tq, tk, tiles["nb_c"], tiles["ahead_c"], bounded)
    yc = lax.cond(fits_c, lambda: attend_c(True), lambda: attend_c(False))

    bound_d = (HEAD_DIM * SCALE * LOG2E * BOUND_SLACK
               * jnp.max(jnp.abs(w["qnorm_d"])) * jnp.max(jnp.abs(w["knorm_d"])))
    fits_d = 2.0 * bound_d + span <= MAX_EXP2_SPAN
    attend_d = lambda bounded: _grid_attn(qd.reshape(bsz, s_len, -1), kd.reshape(bsz, s_len, -1), vtd, u3,
                                          jnp.full((8, LANES), bound_d, F32), tq, tk, tiles["nb_d"],
                                          tiles["ahead_d"], bounded)
    yd = lax.cond(fits_d, lambda: attend_d(True), lambda: attend_d(False))
    flat = lambda y: y.reshape(m, 4 * LANES)
    out = _out_stage(x2, flat(ya), flat(yb), flat(yc), flat(yd), p.reshape(-1, m, PLE_DIM), layer_idx,
                     w["w_o"], w["g_post"], w["w_pe"], w["w_pg"], tiles["tm_out"])
    return out.reshape(bsz, s_len, D_MODEL)


def _trunk(x, p, rel_bias, layers, tiles=None):
    s_len = x.shape[1]
    tiles = tiles or _tiles(s_len)
    tables = _bias_tables(rel_bias, tiles["tq_attn"], tiles["tk_attn"])
    rope = _axial_rope(s_len)
    for i, w in enumerate(layers):
        x = _layer(x, p, i, tables, rope, w, tiles)
    return x


def _prep_layers(w_in, w_o, g_pre, g_post, sink_a, pool_w, pool_scale, lam_q1, lam_k1, lam_q2, lam_k2,
                 diff_subln, qnorm_d, knorm_d, w_pe, w_pg):
    layers = []
    for i in range(w_in.shape[0]):
        layers.append(dict(
            w_in=_prep_w_in(w_in[i]), w_o=w_o[i].astype(BF16), g_pre=g_pre[i], g_post=g_post[i],
            sink=jnp.repeat(sink_a[i].astype(F32) * LOG2E, A_BLOCK).reshape(2, 1, 4 * A_BLOCK),
            pool_w=pool_w[i], pool_scale=pool_scale[i],
            lam=jnp.stack([lam_q1[i], lam_k1[i], lam_q2[i], lam_k2[i]]).astype(F32),
            diff_subln=diff_subln[i], qnorm_d=qnorm_d[i], knorm_d=knorm_d[i],
            w_pe=w_pe[i].astype(BF16), w_pg=w_pg[i].astype(BF16)))
    return layers


def kernel(x_prompt, x_sample, p_prompt, p_sample, w_in, w_o, g_pre, g_post, sink_a, pool_w, pool_scale,
           lam_q1, lam_k1, lam_q2, lam_k2, diff_subln, qnorm_d, knorm_d, rel_bias, w_pe, w_pg):
    layers = _prep_layers(w_in, w_o, g_pre, g_post, sink_a, pool_w, pool_scale, lam_q1, lam_k1, lam_q2,
                          lam_k2, diff_subln, qnorm_d, knorm_d, w_pe, w_pg)
    y_prompt = _trunk(x_prompt, p_prompt, rel_bias, layers)
    y_sample = _trunk(x_sample, p_sample, rel_bias, layers)
    return (y_prompt, y_sample)
```

```python
import functools
import math

import jax
import jax.numpy as jnp
from jax import lax
from jax.experimental import pallas as pl
from jax.experimental.pallas import tpu as pltpu

F32 = jnp.float32
BF16 = jnp.bfloat16

D_MODEL = 2048
HEAD_DIM = 64
LANES = 128
PLE_DIM = 256
EPS = 1e-6
NEG_INF = -1e30
SCALE = HEAD_DIM ** -0.5
LOG2E = math.log2(math.e)
VT_ONES = 16
KV_ROWS = HEAD_DIM + VT_ONES
BOUND_SLACK = 1.02
MAX_EXP2_SPAN = 100.0

A_HEADS = 8
A_WINDOW = 128
A_BLOCK = 128
B_GROUPS = 4
B_POOL_SIZES = (2, 4, 8, 16)
B_HALO = 64
C_HEADS = 4
REL_BUCKETS = 32
REL_MAX_DIST = 128
ROPE_THETA = 10000.0
GRID_W = 64

IN_SIZES = (512, 128, 128, 512, 512, 512, 512, 512, 512, 512, 512, 128, 128, 512)
MIX_WIDTH = 2048

U_WIDTH = 5632
COL_AQ, COL_AG, COL_BX, COL_BG = 0, 4, 8, 12
COL_CQ, COL_CK, COL_CV, COL_CG = 16, 20, 24, 28
COL_DQ, COL_DG = 32, 36
COL_AK, COL_AV, COL_DK, COL_DV = 40, 41, 42, 43

VMEM_LIMIT = 56 * 1024 * 1024


def _params(sem, vmem=VMEM_LIMIT):
    return pltpu.CompilerParams(dimension_semantics=sem, vmem_limit_bytes=vmem)


def _dot(a, b):
    return jnp.dot(a, b, preferred_element_type=F32)


def _dot_nt(a, b):
    return lax.dot_general(a, b, (((1,), (1,)), ((), ())), preferred_element_type=F32)


def _silu(x):
    return x / (1.0 + jnp.exp(-x))


def _lane_is_low(width=LANES):
    return (lax.broadcasted_iota(jnp.int32, (1, width), 1) % LANES) < HEAD_DIM


def _in_proj_kernel(x_ref, g_ref, w_ref, o_ref, h_ref):
    @pl.when(pl.program_id(1) == 0)
    def _():
        x = x_ref[...]
        ms = jnp.mean(x * x, axis=-1, keepdims=True)
        h_ref[...] = (x * lax.rsqrt(ms + EPS) * g_ref[...]).astype(BF16)

    o_ref[...] = _dot(h_ref[...], w_ref[...]).astype(o_ref.dtype)


def _in_proj(x2, g_pre, w_in_p, tm, tn):
    m = x2.shape[0]
    return pl.pallas_call(
        _in_proj_kernel,
        grid=(m // tm, U_WIDTH // tn),
        in_specs=[
            pl.BlockSpec((tm, D_MODEL), lambda i, j: (i, 0)),
            pl.BlockSpec((1, D_MODEL), lambda i, j: (0, 0)),
            pl.BlockSpec((D_MODEL, tn), lambda i, j: (0, j)),
        ],
        out_specs=pl.BlockSpec((tm, tn), lambda i, j: (i, j)),
        out_shape=jax.ShapeDtypeStruct((m, U_WIDTH), BF16),
        scratch_shapes=[pltpu.VMEM((tm, D_MODEL), BF16)],
        compiler_params=_params(("parallel", "arbitrary")),
        name="in_proj",
    )(x2, g_pre.reshape(1, D_MODEL), w_in_p)


def _group_queries(x, group, low):
    xr = pltpu.roll(x, HEAD_DIM, 1)
    zero = jnp.zeros_like(x)
    if group == 0:
        return [jnp.where(low, x, zero), jnp.where(low, xr, zero)]
    return [jnp.where(low, zero, xr), jnp.where(low, zero, x)]


def _win_attn_kernel(q_ref, kp_ref, kc_ref, kn_ref, vp_ref, vc_ref, vn_ref, g_ref, bias_ref,
                     sink_ref, o_ref, *, nq, ahead):
    i = pl.program_id(1)
    n = pl.num_programs(1)
    low = _lane_is_low()
    kext = jnp.concatenate([kp_ref[0], kc_ref[0], kn_ref[0]], axis=0)
    vts = [vp_ref[0, 0]] + [vc_ref[0, b] for b in range(nq)] + [vn_ref[0, 0]]
    units = [(qb, grp) for qb in range(nq) for grp in range(2)]

    def qk(u):
        qb, grp = units[u]
        rows = slice(qb * A_BLOCK, (qb + 1) * A_BLOCK)
        heads = []
        for pair in (2 * grp, 2 * grp + 1):
            heads += _group_queries(q_ref[0, rows, pair * LANES:(pair + 1) * LANES], grp, low)
        return _dot_nt(kext[qb * A_BLOCK:(qb + 3) * A_BLOCK], jnp.concatenate(heads, axis=0))

    def softmax(u, st):
        qb, grp = units[u]
        variant = 1
        if qb == 0:
            variant = jnp.where(i == 0, 0, variant)
        if qb == nq - 1:
            variant = jnp.where(i == n - 1, 2, variant)
        sb = _same_bits(st) + bias_ref[variant, grp]
        m = jnp.maximum(jnp.max(sb, axis=0, keepdims=True), sink_ref[grp])
        return m, jnp.exp2(sb - m).astype(BF16)

    def pv(u, probs):
        qb, grp = units[u]
        m, pt = probs
        vt = jnp.concatenate([vts[qb + d][grp] for d in range(3)], axis=1)
        acc = _dot(vt, pt)
        ot = acc[:HEAD_DIM] / (acc[HEAD_DIM:HEAD_DIM + 1] + jnp.exp2(sink_ref[grp] - m))
        rows = slice(qb * A_BLOCK, (qb + 1) * A_BLOCK)
        for pair in range(2):
            yt = jnp.concatenate([ot[:, (2 * pair) * A_BLOCK:(2 * pair + 1) * A_BLOCK],
                                  ot[:, (2 * pair + 1) * A_BLOCK:(2 * pair + 2) * A_BLOCK]], axis=0)
            cols = slice((2 * grp + pair) * LANES, (2 * grp + pair + 1) * LANES)
            gate = g_ref[0, rows, cols].astype(F32)
            o_ref[0, rows, cols] = (yt.T * _silu(gate)).astype(o_ref.dtype)

    _flash_units(len(units), ahead, qk, softmax, pv)


def _win_attn(u3, vta, bias_a, sink_a, tq, ahead):
    bsz, s_len, _ = u3.shape
    nq = tq // A_BLOCK
    nblk = s_len // A_BLOCK
    wide = (1, tq, 4 * LANES)
    const = lambda shape: pl.BlockSpec(shape, lambda bi, i: (0,) * len(shape), pipeline_mode=pl.Buffered(1))
    return pl.pallas_call(
        functools.partial(_win_attn_kernel, nq=nq, ahead=ahead),
        grid=(bsz, s_len // tq),
        in_specs=[
            pl.BlockSpec(wide, lambda bi, i: (bi, i, COL_AQ // 4)),
            pl.BlockSpec((1, A_BLOCK, LANES), lambda bi, i: (bi, jnp.maximum(i * nq - 1, 0), COL_AK)),
            pl.BlockSpec((1, tq, LANES), lambda bi, i: (bi, i, COL_AK)),
            pl.BlockSpec((1, A_BLOCK, LANES), lambda bi, i: (bi, jnp.minimum((i + 1) * nq, nblk - 1), COL_AK)),
            pl.BlockSpec((1, 1, 2, KV_ROWS, A_BLOCK), lambda bi, i: (bi, jnp.maximum(i * nq - 1, 0), 0, 0, 0)),
            pl.BlockSpec((1, nq, 2, KV_ROWS, A_BLOCK), lambda bi, i: (bi, i, 0, 0, 0)),
            pl.BlockSpec((1, 1, 2, KV_ROWS, A_BLOCK),
                         lambda bi, i: (bi, jnp.minimum((i + 1) * nq, nblk - 1), 0, 0, 0)),
            pl.BlockSpec(wide, lambda bi, i: (bi, i, COL_AG // 4)),
            const(bias_a.shape),
            const(sink_a.shape),
        ],
        out_specs=pl.BlockSpec(wide, lambda bi, i: (bi, i, 0)),
        out_shape=jax.ShapeDtypeStruct((bsz, s_len, 4 * LANES), BF16),
        compiler_params=_params(("parallel", "parallel")),
        name="win_attn",
    )(u3, u3, u3, u3, vta, vta, vta, u3, bias_a, sink_a)


def _pool_kernel(xp_ref, xc_ref, xn_ref, g_ref, band_ref, w_ref, sc_ref, o_ref, *, s_len, tq):
    i = pl.program_id(1)
    n = pl.num_programs(1)
    xp = xp_ref[0]
    xn = xn_ref[0]
    xp = jnp.where(i > 0, xp, jnp.zeros_like(xp))
    xn = jnp.where(i < n - 1, xn, jnp.zeros_like(xn))
    xc = xc_ref[0]
    xext = jnp.concatenate([xp, xc, xn], axis=0)
    t = i * tq + lax.broadcasted_iota(jnp.int32, (tq, 1), 0)
    for gi, size in enumerate(B_POOL_SIZES):
        lo = size // 2
        hi = size - lo - 1
        lanes = slice(gi * LANES, (gi + 1) * LANES)
        win_sum = _dot(band_ref[gi], xext[:, lanes])
        start = jnp.maximum(t - lo, 0)
        end = jnp.minimum(t + hi, s_len - 1) + 1
        pooled = win_sum / (end - start).astype(F32) - xc[:, lanes].astype(F32)
        y = _dot(pooled.astype(BF16), w_ref[gi]) * sc_ref[:, lanes]
        gate = g_ref[0, :, lanes].astype(F32)
        o_ref[0, :, lanes] = (y * _silu(gate)).astype(o_ref.dtype)


def _pool_bands(tq):
    r = jnp.arange(tq)[:, None]
    c = jnp.arange(tq + 2 * B_HALO)[None, :] - B_HALO
    bands = []
    for size in B_POOL_SIZES:
        lo = size // 2
        hi = size - lo - 1
        bands.append(((c - r >= -lo) & (c - r <= hi)).astype(BF16))
    return jnp.stack(bands)


def _pool(u3, pool_w, pool_scale, tq):
    bsz, s_len, _ = u3.shape
    nt = s_len // tq
    per = tq // B_HALO
    nh = s_len // B_HALO
    wide = (1, tq, 4 * LANES)
    halo = (1, B_HALO, 4 * LANES)
    return pl.pallas_call(
        functools.partial(_pool_kernel, s_len=s_len, tq=tq),
        grid=(bsz, nt),
        in_specs=[
            pl.BlockSpec(halo, lambda bi, i: (bi, jnp.maximum(i * per - 1, 0), COL_BX // 4)),
            pl.BlockSpec(wide, lambda bi, i: (bi, i, COL_BX // 4)),
            pl.BlockSpec(halo, lambda bi, i: (bi, jnp.minimum((i + 1) * per, nh - 1), COL_BX // 4)),
            pl.BlockSpec(wide, lambda bi, i: (bi, i, COL_BG // 4)),
            pl.BlockSpec((B_GROUPS, tq, tq + 2 * B_HALO), lambda bi, i: (0, 0, 0)),
            pl.BlockSpec((B_GROUPS, LANES, LANES), lambda bi, i: (0, 0, 0)),
            pl.BlockSpec((1, 4 * LANES), lambda bi, i: (0, 0)),
        ],
        out_specs=pl.BlockSpec(wide, lambda bi, i: (bi, i, 0)),
        out_shape=jax.ShapeDtypeStruct((bsz, s_len, 4 * LANES), BF16),
        compiler_params=_params(("parallel", "parallel")),
        name="pool",
    )(u3, u3, u3, u3, _pool_bands(tq), pool_w.astype(BF16), pool_scale.reshape(1, 4 * LANES))


def _softmax_t(st, m):
    m_new = jnp.maximum(m, jnp.max(st, axis=0, keepdims=True))
    return m_new, jnp.exp2(m - m_new), jnp.exp2(st - m_new).astype(BF16)


def _flash_units(n_units, ahead, qk, softmax, pv):
    scores = {u: qk(u) for u in range(min(ahead, n_units))}
    for u in range(n_units):
        probs = softmax(u, scores.pop(u))
        if u + ahead < n_units:
            scores[u + ahead] = qk(u + ahead)
        pv(u, probs)


def _same_bits(x):
    return lax.bitcast_convert_type(lax.bitcast_convert_type(x, jnp.int32), F32)


def _transposed(x):
    return x.astype(F32).T.astype(BF16)


def _masked_halves(q, low):
    zero = jnp.zeros_like(q)
    return jnp.where(low, q, zero), jnp.where(low, zero, q)


def _diff_attn_kernel(q_ref, k_ref, vt_ref, g_ref, bias_ref, stat_ref, lam_ref, sub_ref, o_ref,
                      *, tq, tk, nb, n_iter, ahead, lambda_init, bounded, n_sub):
    for sub in range(n_sub):
        _diff_attn_tile(q_ref, k_ref, vt_ref, g_ref, bias_ref, stat_ref, lam_ref, sub_ref, o_ref, sub,
                        pl.program_id(2) * n_sub + sub, tq=tq, tk=tk, nb=nb, n_iter=n_iter, ahead=ahead,
                        lambda_init=lambda_init, bounded=bounded)


def _diff_attn_tile(q_ref, k_ref, vt_ref, g_ref, bias_ref, stat_ref, lam_ref, sub_ref, o_ref, sub, i,
                    *, tq, tk, nb, n_iter, ahead, lambda_init, bounded):
    rows_q = slice(sub * tq, (sub + 1) * tq)
    ratio = tq // tk
    n_tab = bias_ref.shape[1]
    rows = vt_ref.shape[-2]
    low = _lane_is_low()
    q = q_ref[0, rows_q]
    qs = _masked_halves(q, low)
    bounds = None
    if bounded:
        qf = q.astype(F32)
        halves = jnp.concatenate([jnp.where(low, 1.0, 0.0), jnp.where(low, 0.0, 1.0)] +
                                 [jnp.zeros((1, LANES), F32)] * 6, axis=0).astype(BF16)
        norm2 = _dot_nt(halves, (qf * qf).astype(BF16))
        bounds = [jnp.sqrt(norm2[h:h + 1]) * stat_ref[0, 0, h:h + 1, 0:1] + stat_ref[0, 0, 2:3, 0:1]
                  for h in range(2)]

    def step(j, carry):
        state = list(carry)
        blocks = [nb * j + b for b in range(nb)]
        ks = [k_ref[0, pl.ds(pl.multiple_of(blk * tk, tk), tk), :] for blk in blocks]
        vts = [vt_ref[0, 0, blk] for blk in blocks]
        tabs = [jnp.clip(blk - ratio * i + 2, 0, n_tab - 1) for blk in blocks]

        def qk(u):
            return _dot_nt(ks[u // 2], qs[u % 2])

        def softmax(u, st):
            sb = _same_bits(st) + bias_ref[0, tabs[u // 2]]
            if bounded:
                return None, None, jnp.exp2(sb - bounds[u % 2]).astype(BF16)
            return _softmax_t(sb, state[u % 2][0])

        def pv(u, probs):
            m_new, alpha, pt = probs
            acc = state[u % 2][1]
            state[u % 2] = (m_new, (acc if bounded else alpha * acc) + _dot(vts[u // 2], pt))

        _flash_units(2 * nb, ahead, qk, softmax, pv)
        return tuple(state)

    m0 = None if bounded else jnp.full((1, tq), NEG_INF, F32)
    init = tuple((m0, jnp.zeros((rows, tq), F32)) for _ in range(2))
    (_, a1), (_, a2) = lax.fori_loop(0, n_iter, step, init)

    dot1 = jnp.sum(lam_ref[0:1, :] * lam_ref[1:2, :], axis=-1, keepdims=True)
    dot2 = jnp.sum(lam_ref[2:3, :] * lam_ref[3:4, :], axis=-1, keepdims=True)
    lam = jnp.exp(dot1) - jnp.exp(dot2) + lambda_init
    yt = a1[:LANES] / a1[LANES:LANES + 1] - lam * (a2[:LANES] / a2[LANES:LANES + 1])
    y = yt.T
    ms = jnp.mean(y * y, axis=-1, keepdims=True)
    y = y * lax.rsqrt(ms + EPS) * sub_ref[...] * (1.0 - lambda_init)
    o_ref[0, rows_q] = (y * _silu(g_ref[0, rows_q].astype(F32))).astype(o_ref.dtype)


def _diff_attn(u3, vtc, bias_c, stat_c, lam_vecs, subln, lambda_init, tq, tk, nb, ahead, bounded):
    bsz, s_len, _ = u3.shape
    nk = s_len // tk
    nb = min(nb, nk)
    rows = vtc.shape[-2]
    n_tab = bias_c.shape[1]
    n_sub = 2 if nk <= 16 and s_len % (2 * tq) == 0 else 1
    return pl.pallas_call(
        functools.partial(_diff_attn_kernel, tq=tq, tk=tk, nb=nb, n_iter=nk // nb, ahead=ahead,
                          lambda_init=lambda_init, bounded=bounded, n_sub=n_sub),
        grid=(bsz, C_HEADS, s_len // (tq * n_sub)),
        in_specs=[
            pl.BlockSpec((1, tq * n_sub, LANES), lambda bi, h, i: (bi, i, COL_CQ + h)),
            pl.BlockSpec((1, s_len, LANES), lambda bi, h, i: (bi, 0, COL_CK + h)),
            pl.BlockSpec((1, 1, nk, rows, tk), lambda bi, h, i: (bi, h, 0, 0, 0)),
            pl.BlockSpec((1, tq * n_sub, LANES), lambda bi, h, i: (bi, i, COL_CG + h)),
            pl.BlockSpec((1, n_tab, tk, tq), lambda bi, h, i: (h, 0, 0, 0)),
            pl.BlockSpec((1, 1, 8, LANES), lambda bi, h, i: (bi, h, 0, 0)),
            pl.BlockSpec((4, HEAD_DIM), lambda bi, h, i: (0, 0)),
            pl.BlockSpec((1, LANES), lambda bi, h, i: (0, 0)),
        ],
        out_specs=pl.BlockSpec((1, tq * n_sub, LANES), lambda bi, h, i: (bi, i, h)),
        out_shape=jax.ShapeDtypeStruct((bsz, s_len, 4 * LANES), BF16),
        compiler_params=_params(("parallel", "parallel", "parallel")),
        name="diff_attn_bounded" if bounded else "diff_attn",
    )(u3, u3, vtc, u3, bias_c, stat_c, lam_vecs, subln.reshape(1, LANES))


def _grid_attn_kernel(q_ref, k_ref, vt_ref, g_ref, bound_ref, o_ref, *, n_sub, **tile_args):
    for sub in range(n_sub):
        _grid_attn_tile(q_ref, k_ref, vt_ref, g_ref, bound_ref, o_ref, sub, **tile_args)


def _grid_attn_tile(q_ref, k_ref, vt_ref, g_ref, bound_ref, o_ref, sub, *, tq, tk, nb, n_iter, ahead, bounded):
    rows_q = slice(sub * tq, (sub + 1) * tq)
    grp = pl.program_id(1)
    lane = lax.broadcasted_iota(jnp.int32, (1, LANES), 1)
    keep = (lane >= grp * HEAD_DIM) & (lane < (grp + 1) * HEAD_DIM)
    rows = vt_ref.shape[-2]
    bound = bound_ref[0:1, 0:1]
    qs = []
    for pair in range(2):
        x = q_ref[0, rows_q, pair * LANES:(pair + 1) * LANES]
        xr = pltpu.roll(x, HEAD_DIM, 1)
        zero = jnp.zeros_like(x)
        qs.append(_transposed(jnp.where(keep, jnp.where(grp == 0, x, xr), zero)))
        qs.append(_transposed(jnp.where(keep, jnp.where(grp == 0, xr, x), zero)))

    def step(j, carry):
        state = list(carry)
        blocks = [nb * j + b for b in range(nb)]
        ks = [k_ref[0, pl.ds(pl.multiple_of(blk * tk, tk), tk), :] for blk in blocks]
        vts = [vt_ref[0, 0, blk] for blk in blocks]

        def qk(u):
            return _dot(ks[u // 4], qs[u % 4])

        def softmax(u, st):
            if bounded:
                return None, None, jnp.exp2(st - bound).astype(BF16)
            return _softmax_t(st, state[u % 4][0])

        def pv(u, probs):
            m_new, alpha, pt = probs
            acc = state[u % 4][1]
            state[u % 4] = (m_new, (acc if bounded else alpha * acc) + _dot(vts[u // 4], pt))

        _flash_units(4 * nb, ahead, qk, softmax, pv)
        return tuple(state)

    m0 = None if bounded else jnp.full((1, tq), NEG_INF, F32)
    init = tuple((m0, jnp.zeros((rows, tq), F32)) for _ in range(4))
    state = lax.fori_loop(0, n_iter, step, init)
    for pair in range(2):
        a0, a1 = state[2 * pair][1], state[2 * pair + 1][1]
        yt = jnp.concatenate([a0[:HEAD_DIM] / a0[HEAD_DIM:HEAD_DIM + 1],
                              a1[:HEAD_DIM] / a1[HEAD_DIM:HEAD_DIM + 1]], axis=0)
        gate = g_ref[0, rows_q, pair * LANES:(pair + 1) * LANES].astype(F32)
        o_ref[0, rows_q, pair * LANES:(pair + 1) * LANES] = (yt.T * _silu(gate)).astype(o_ref.dtype)


def _grid_attn(qd3, kd3, vtd, u3, bound_d, tq, tk, nb, ahead, bounded):
    bsz, s_len, _ = qd3.shape
    nk = s_len // tk
    nb = min(nb, nk)
    rows = vtd.shape[-2]
    n_sub = 2 if nk <= 16 and s_len % (2 * tq) == 0 else 1
    return pl.pallas_call(
        functools.partial(_grid_attn_kernel, tq=tq, tk=tk, nb=nb, n_iter=nk // nb, ahead=ahead,
                          bounded=bounded, n_sub=n_sub),
        grid=(bsz, 2, s_len // (tq * n_sub)),
        in_specs=[
            pl.BlockSpec((1, tq * n_sub, 2 * LANES), lambda bi, h, i: (bi, i, h)),
            pl.BlockSpec((1, s_len, LANES), lambda bi, h, i: (bi, 0, 0)),
            pl.BlockSpec((1, 1, nk, rows, tk), lambda bi, h, i: (bi, h, 0, 0, 0)),
            pl.BlockSpec((1, tq * n_sub, 2 * LANES), lambda bi, h, i: (bi, i, COL_DG // 2 + h)),
            pl.BlockSpec((8, LANES), lambda bi, h, i: (0, 0)),
        ],
        out_specs=pl.BlockSpec((1, tq * n_sub, 2 * LANES), lambda bi, h, i: (bi, i, h)),
        out_shape=jax.ShapeDtypeStruct((bsz, s_len, 4 * LANES), BF16),
        compiler_params=_params(("parallel", "parallel", "parallel")),
        name="grid_attn_bounded" if bounded else "grid_attn",
    )(qd3, kd3, vtd, u3, bound_d)


def _norm_rope(x, gain, cos, sin, seg):
    w = x.shape[-1]
    x2 = x * x
    hi = x2.astype(BF16)
    lo = (x2 - hi.astype(F32)).astype(BF16)
    ss = _dot(hi, seg) + _dot(lo, seg)
    y = x * lax.rsqrt(ss * (1.0 / HEAD_DIM) + EPS) * gain
    quarter = HEAD_DIM // 4
    ahead = pltpu.roll(y, w - quarter, 1)
    behind = pltpu.roll(y, quarter, 1)
    first = (lax.broadcasted_iota(jnp.int32, (1, w), 1) % (2 * quarter)) < quarter
    rot = jnp.where(first, -ahead, behind)
    reps = w // LANES
    cos_w = jnp.concatenate([cos] * reps, axis=1)
    sin_w = jnp.concatenate([sin] * reps, axis=1)
    return y * cos_w + rot * sin_w


def _attn_prep_kernel(q_ref, k_ref, vc_ref, vd_ref, va_ref, qc_ref, kc_ref, cos_ref, sin_ref, gq_ref, gk_ref,
                      seg_ref, qo_ref, ko_ref, vtc_ref, vtd_ref, vta_ref, qn_ref, kn_ref):
    cos = cos_ref[...]
    sin = sin_ref[...]
    seg = seg_ref[...]
    q = _norm_rope(q_ref[...].astype(F32), gq_ref[...], cos, sin, seg)
    qo_ref[...] = (q * (SCALE * LOG2E)).astype(qo_ref.dtype)
    k = _norm_rope(k_ref[...].astype(F32), gk_ref[...], cos, sin, seg[:LANES, :LANES])
    ko_ref[...] = k.astype(ko_ref.dtype)
    tm = q.shape[0]
    ones = jnp.ones((VT_ONES, tm), BF16)
    vct = vc_ref[...].astype(F32).T
    for h in range(C_HEADS):
        vtc_ref[0, h, 0] = jnp.concatenate([vct[h * LANES:(h + 1) * LANES].astype(BF16), ones], axis=0)
    vdt = vd_ref[...].astype(F32).T.astype(BF16)
    vat = va_ref[...].astype(F32).T.astype(BF16)
    for h in range(2):
        vtd_ref[0, h, 0] = jnp.concatenate([vdt[h * HEAD_DIM:(h + 1) * HEAD_DIM], ones], axis=0)
        for kb in range(tm // A_BLOCK):
            cols = slice(kb * A_BLOCK, (kb + 1) * A_BLOCK)
            vta_ref[0, kb, h] = jnp.concatenate([vat[h * HEAD_DIM:(h + 1) * HEAD_DIM, cols], ones[:, cols]], axis=0)
    for src, dst in ((qc_ref, qn_ref), (kc_ref, kn_ref)):
        x = src[...].astype(F32)
        seg_sums = _dot((x * x).astype(BF16), seg)
        dst[0] = jnp.broadcast_to(jnp.max(seg_sums, axis=0, keepdims=True), (8, 4 * LANES))


def _attn_prep(u2, cos, sin, qnorm, knorm, bsz, s_len, tm):
    m = u2.shape[0]
    per_seq = s_len // tm
    per_tile = tm // A_BLOCK
    lane = jnp.arange(4 * LANES)
    seg = (lane[:, None] // HEAD_DIM == lane[None, :] // HEAD_DIM).astype(BF16)
    rows_c = LANES + VT_ONES
    return pl.pallas_call(
        _attn_prep_kernel,
        grid=(m // tm,),
        in_specs=[
            pl.BlockSpec((tm, 4 * LANES), lambda i: (i, COL_DQ // 4)),
            pl.BlockSpec((tm, LANES), lambda i: (i, COL_DK)),
            pl.BlockSpec((tm, 4 * LANES), lambda i: (i, COL_CV // 4)),
            pl.BlockSpec((tm, LANES), lambda i: (i, COL_DV)),
            pl.BlockSpec((tm, LANES), lambda i: (i, COL_AV)),
            pl.BlockSpec((tm, 4 * LANES), lambda i: (i, COL_CQ // 4)),
            pl.BlockSpec((tm, 4 * LANES), lambda i: (i, COL_CK // 4)),
            pl.BlockSpec((tm, LANES), lambda i: (i % per_seq, 0)),
            pl.BlockSpec((tm, LANES), lambda i: (i % per_seq, 0)),
            pl.BlockSpec((1, 4 * LANES), lambda i: (0, 0)),
            pl.BlockSpec((1, LANES), lambda i: (0, 0)),
            pl.BlockSpec((4 * LANES, 4 * LANES), lambda i: (0, 0)),
        ],
        out_specs=[
            pl.BlockSpec((tm, 4 * LANES), lambda i: (i, 0)),
            pl.BlockSpec((tm, LANES), lambda i: (i, 0)),
            pl.BlockSpec((1, C_HEADS, 1, rows_c, tm), lambda i: (i // per_seq, 0, i % per_seq, 0, 0)),
            pl.BlockSpec((1, 2, 1, KV_ROWS, tm), lambda i: (i // per_seq, 0, i % per_seq, 0, 0)),
            pl.BlockSpec((1, per_tile, 2, KV_ROWS, A_BLOCK), lambda i: (i // per_seq, i % per_seq, 0, 0, 0)),
            pl.BlockSpec((1, 8, 4 * LANES), lambda i: (i, 0, 0)),
            pl.BlockSpec((1, 8, 4 * LANES), lambda i: (i, 0, 0)),
        ],
        out_shape=[jax.ShapeDtypeStruct((m, 4 * LANES), BF16),
                   jax.ShapeDtypeStruct((m, LANES), BF16),
                   jax.ShapeDtypeStruct((bsz, C_HEADS, per_seq, rows_c, tm), BF16),
                   jax.ShapeDtypeStruct((bsz, 2, per_seq, KV_ROWS, tm), BF16),
                   jax.ShapeDtypeStruct((bsz, s_len // A_BLOCK, 2, KV_ROWS, A_BLOCK), BF16),
                   jax.ShapeDtypeStruct((m // tm, 8, 4 * LANES), F32),
                   jax.ShapeDtypeStruct((m // tm, 8, 4 * LANES), F32)],
        compiler_params=_params(("parallel",)),
        name="attn_prep",
    )(u2, u2, u2, u2, u2, u2, u2, cos, sin, jnp.tile(qnorm, 8).reshape(1, 4 * LANES),
      jnp.tile(knorm, 2).reshape(1, LANES), seg)


def _out_kernel(x_ref, ya_ref, yb_ref, yc_ref, yd_ref, pe_ref, wo_ref, gp_ref, wpe_ref, wpg_ref, o_ref):
    width = 4 * LANES
    z = _dot(ya_ref[...], wo_ref[0 * width:1 * width, :])
    z += _dot(yb_ref[...], wo_ref[1 * width:2 * width, :])
    z += _dot(yc_ref[...], wo_ref[2 * width:3 * width, :])
    z += _dot(yd_ref[...], wo_ref[3 * width:4 * width, :])
    ms = jnp.mean(z * z, axis=-1, keepdims=True)
    x = x_ref[...] + z * lax.rsqrt(ms + EPS) * gp_ref[...]
    logits = _dot(x.astype(BF16), wpg_ref[...])
    gate = 1.0 / (1.0 + jnp.exp(-logits))
    pe = _dot(pe_ref[0].astype(BF16), wpe_ref[...])
    o_ref[...] = x + pe * gate


def _out_stage(x2, ya, yb, yc, yd, p3, layer_idx, w_o, g_post, w_pe, w_pg, tm):
    m = x2.shape[0]
    row = lambda w: pl.BlockSpec((tm, w), lambda i: (i, 0))
    const = lambda r, c: pl.BlockSpec((r, c), lambda i: (0, 0), pipeline_mode=pl.Buffered(1))
    return pl.pallas_call(
        _out_kernel,
        grid=(m // tm,),
        in_specs=[row(D_MODEL), row(4 * LANES), row(4 * LANES), row(4 * LANES), row(4 * LANES),
                  pl.BlockSpec((1, tm, PLE_DIM), lambda i: (layer_idx, i, 0)), const(MIX_WIDTH, D_MODEL),
                  const(1, D_MODEL),
                  const(PLE_DIM, D_MODEL), const(D_MODEL, D_MODEL)],
        out_specs=row(D_MODEL),
        out_shape=jax.ShapeDtypeStruct((m, D_MODEL), F32),
        compiler_params=_params(("parallel",)),
        name="out_stage",
    )(x2, ya, yb, yc, yd, p3, w_o, g_post.reshape(1, D_MODEL), w_pe, w_pg)


def _prep_w_in(w):
    parts, start = [], 0
    for n in IN_SIZES:
        parts.append(w[:, start:start + n])
        start += n
    aq, ak, av, ag, bx, bg, cq, ck, cv, cg, dq, dk, dv, dg = parts

    log2_scale = SCALE * LOG2E
    cols = [aq * log2_scale, ag, bx, bg, cq * log2_scale, ck, cv, cg, dq, dg, ak, av, dk, dv]
    return jnp.concatenate(cols, axis=1).astype(BF16)


def _rel_bucket(rel):
    nb = REL_BUCKETS // 2
    max_exact = nb // 2
    ret = jnp.where(rel > 0, nb, 0)
    n = jnp.abs(rel)
    nf = jnp.maximum(n, 1).astype(F32)
    large = max_exact + (jnp.log(nf / max_exact) / math.log(REL_MAX_DIST / max_exact)
                         * (nb - max_exact)).astype(jnp.int32)
    large = jnp.minimum(large, nb - 1)
    return ret + jnp.where(n < max_exact, n, large)


def _bucket_values(tab, rel):
    bucket = _rel_bucket(rel)[None]
    out = jnp.zeros((tab.shape[1],) + rel.shape, F32)
    for b in range(REL_BUCKETS):
        out = jnp.where(bucket == b, tab[b].reshape((-1,) + (1,) * rel.ndim), out)
    return out


def _bias_tables(rel_bias, tq, tk):
    tab = rel_bias.astype(F32)
    ck = jnp.arange(3 * A_BLOCK)[:, None]
    rq = jnp.arange(A_BLOCK)[None, :]
    rel_a = ck - A_BLOCK - rq
    vals = _bucket_values(tab[:, :A_HEADS], rel_a) * LOG2E
    in_band = jnp.abs(rel_a) <= A_WINDOW
    variants = []
    for edge in (ck >= A_BLOCK, ck >= 0, ck < 2 * A_BLOCK):
        masked = jnp.where(in_band & edge, vals, NEG_INF)
        variants.append(jnp.transpose(masked.reshape(2, 4, 3 * A_BLOCK, A_BLOCK), (0, 2, 1, 3))
                        .reshape(2, 3 * A_BLOCK, 4 * A_BLOCK))
    bias_a = jnp.stack(variants)
    assert tk >= REL_MAX_DIST
    n_tab = tq // tk + 4
    d = jnp.arange(n_tab)[:, None, None]
    c = jnp.arange(tk)[None, :, None]
    r = jnp.arange(tq)[None, None, :]
    bias_c = _bucket_values(tab[:, A_HEADS:], c + (d - 2) * tk - r) * LOG2E
    return bias_a, bias_c


def _axial_rope(s_len):
    rows = s_len // GRID_W
    t_row = jnp.repeat(jnp.arange(rows), GRID_W).astype(F32)
    t_col = jnp.tile(jnp.arange(GRID_W), rows).astype(F32)
    half = HEAD_DIM // 2
    inv = ROPE_THETA ** (-jnp.arange(0, half, 2, dtype=F32) / half)
    ang_r = t_row[:, None] * inv[None, :]
    ang_c = t_col[:, None] * inv[None, :]
    ang = jnp.concatenate([ang_r, ang_r, ang_c, ang_c] * 2, axis=-1)
    return jnp.cos(ang), jnp.sin(ang)


def _tiles(s_len):
    return dict(tm_in=min(512, s_len), tn_in=U_WIDTH // 2, tq_pool=min(256, s_len), tm_out=min(256, s_len),
                tq_attn=512, tk_attn=256, nb_c=32, nb_d=32, ahead_c=3, ahead_d=3, tq_win=1024, ahead_win=1)


def _layer(x, p, layer_idx, tables, rope, w, tiles):
    bsz, s_len, _ = x.shape
    m = bsz * s_len
    bias_a, bias_c = tables
    cos, sin = rope
    tq, tk = tiles["tq_attn"], tiles["tk_attn"]
    x2 = x.reshape(m, D_MODEL)
    u2 = _in_proj(x2, w["g_pre"], w["w_in"], tiles["tm_in"], tiles["tn_in"])
    u3 = u2.reshape(bsz, s_len, U_WIDTH)
    qd, kd, vtc, vtd, vta, qn, kn = _attn_prep(u2, cos, sin, w["qnorm_d"], w["knorm_d"], bsz, s_len, tk)
    ya = _win_attn(u3, vta, bias_a, w["sink"], tiles["tq_win"], tiles["ahead_win"])
    yb = _pool(u3, w["pool_w"], w["pool_scale"], tiles["tq_pool"])
    lambda_init = 0.8 - 0.6 * math.exp(-0.3 * layer_idx)
    span = math.log2(s_len)

    seg_max = lambda t: jnp.sqrt(jnp.max(t.reshape(bsz, -1, 8, C_HEADS, 2, HEAD_DIM)[:, :, 0, :, :, 0], axis=1))
    q_max, k_max = seg_max(qn), seg_max(kn) * BOUND_SLACK
    b_hi, b_lo = jnp.max(bias_c, axis=(1, 2, 3)), jnp.min(bias_c, axis=(1, 2, 3))
    stat_c = jnp.concatenate([k_max, jnp.broadcast_to(b_hi[None, :, None], (bsz, C_HEADS, 1)),
                              jnp.zeros((bsz, C_HEADS, 5), F32)], axis=2)
    stat_c = jnp.broadcast_to(stat_c[..., None], (bsz, C_HEADS, 8, LANES))
    fits_c = 2.0 * jnp.max(q_max * k_max) + jnp.max(b_hi - b_lo) + span <= MAX_EXP2_SPAN
    attend_c = lambda bounded: _diff_attn(u3, vtc, bias_c, stat_c, w["lam"], w["diff_subln"], lambda_init,
                                          tq, tk, tiles["nb_c"], tiles["ahead_c"], bounded)
    yc = lax.cond(fits_c, lambda: attend_c(True), lambda: attend_c(False))

    bound_d = (HEAD_DIM * SCALE * LOG2E * BOUND_SLACK
               * jnp.max(jnp.abs(w["qnorm_d"])) * jnp.max(jnp.abs(w["knorm_d"])))
    fits_d = 2.0 * bound_d + span <= MAX_EXP2_SPAN
    attend_d = lambda bounded: _grid_attn(qd.reshape(bsz, s_len, -1), kd.reshape(bsz, s_len, -1), vtd, u3,
                                          jnp.full((8, LANES), bound_d, F32), tq, tk, tiles["nb_d"],
                                          tiles["ahead_d"], bounded)
    yd = lax.cond(fits_d, lambda: attend_d(True), lambda: attend_d(False))
    flat = lambda y: y.reshape(m, 4 * LANES)
    out = _out_stage(x2, flat(ya), flat(yb), flat(yc), flat(yd), p.reshape(-1, m, PLE_DIM), layer_idx,
                     w["w_o"], w["g_post"], w["w_pe"], w["w_pg"], tiles["tm_out"])
    return out.reshape(bsz, s_len, D_MODEL)


def _trunk(x, p, rel_bias, layers, tiles=None):
    s_len = x.shape[1]
    tiles = tiles or _tiles(s_len)
    tables = _bias_tables(rel_bias, tiles["tq_attn"], tiles["tk_attn"])
    rope = _axial_rope(s_len)
    for i, w in enumerate(layers):
        x = _layer(x, p, i, tables, rope, w, tiles)
    return x


def _prep_layers(w_in, w_o, g_pre, g_post, sink_a, pool_w, pool_scale, lam_q1, lam_k1, lam_q2, lam_k2,
                 diff_subln, qnorm_d, knorm_d, w_pe, w_pg):
    layers = []
    for i in range(w_in.shape[0]):
        layers.append(dict(
            w_in=_prep_w_in(w_in[i]), w_o=w_o[i].astype(BF16), g_pre=g_pre[i], g_post=g_post[i],
            sink=jnp.repeat(sink_a[i].astype(F32) * LOG2E, A_BLOCK).reshape(2, 1, 4 * A_BLOCK),
            pool_w=pool_w[i], pool_scale=pool_scale[i],
            lam=jnp.stack([lam_q1[i], lam_k1[i], lam_q2[i], lam_k2[i]]).astype(F32),
            diff_subln=diff_subln[i], qnorm_d=qnorm_d[i], knorm_d=knorm_d[i],
            w_pe=w_pe[i].astype(BF16), w_pg=w_pg[i].astype(BF16)))
    return layers


def kernel(x_prompt, x_sample, p_prompt, p_sample, w_in, w_o, g_pre, g_post, sink_a, pool_w, pool_scale,
           lam_q1, lam_k1, lam_q2, lam_k2, diff_subln, qnorm_d, knorm_d, rel_bias, w_pe, w_pg):
    layers = _prep_layers(w_in, w_o, g_pre, g_post, sink_a, pool_w, pool_scale, lam_q1, lam_k1, lam_q2,
                          lam_k2, diff_subln, qnorm_d, knorm_d, w_pe, w_pg)
    y_prompt = _trunk(x_prompt, p_prompt, rel_bias, layers)
    y_sample = _trunk(x_sample, p_sample, rel_bias, layers)
    return (y_prompt, y_sample)
```

```python
import functools
import math

import jax
import jax.numpy as jnp
from jax import lax
from jax.experimental import pallas as pl
from jax.experimental.pallas import tpu as pltpu

F32 = jnp.float32
BF16 = jnp.bfloat16

D_MODEL = 2048
HEAD_DIM = 64
LANES = 128
PLE_DIM = 256
EPS = 1e-6
NEG_INF = -1e30
SCALE = HEAD_DIM ** -0.5
LOG2E = math.log2(math.e)
VT_ONES = 16
KV_ROWS = HEAD_DIM + VT_ONES
BOUND_SLACK = 1.02
MAX_EXP2_SPAN = 100.0

A_HEADS = 8
A_WINDOW = 128
A_BLOCK = 128
B_GROUPS = 4
B_POOL_SIZES = (2, 4, 8, 16)
B_HALO = 64
C_HEADS = 4
REL_BUCKETS = 32
REL_MAX_DIST = 128
ROPE_THETA = 10000.0
GRID_W = 64

IN_SIZES = (512, 128, 128, 512, 512, 512, 512, 512, 512, 512, 512, 128, 128, 512)
MIX_WIDTH = 2048

U_WIDTH = 5632
COL_AQ, COL_AG, COL_BX, COL_BG = 0, 4, 8, 12
COL_CQ, COL_CK, COL_CV, COL_CG = 16, 20, 24, 28
COL_DQ, COL_DG = 32, 36
COL_AK, COL_AV, COL_DK, COL_DV = 40, 41, 42, 43

VMEM_LIMIT = 56 * 1024 * 1024


def _params(sem, vmem=VMEM_LIMIT):
    return pltpu.CompilerParams(dimension_semantics=sem, vmem_limit_bytes=vmem)


def _dot(a, b):
    return jnp.dot(a, b, preferred_element_type=F32)


def _dot_nt(a, b):
    return lax.dot_general(a, b, (((1,), (1,)), ((), ())), preferred_element_type=F32)


def _silu(x):
    return x / (1.0 + jnp.exp(-x))


def _lane_is_low(width=LANES):
    return (lax.broadcasted_iota(jnp.int32, (1, width), 1) % LANES) < HEAD_DIM


def _in_proj_kernel(x_ref, g_ref, w_ref, o_ref, h_ref):
    @pl.when(pl.program_id(1) == 0)
    def _():
        x = x_ref[...]
        ms = jnp.mean(x * x, axis=-1, keepdims=True)
        h_ref[...] = (x * lax.rsqrt(ms + EPS) * g_ref[...]).astype(BF16)

    o_ref[...] = _dot(h_ref[...], w_ref[...]).astype(o_ref.dtype)


def _in_proj(x2, g_pre, w_in_p, tm, tn):
    m = x2.shape[0]
    return pl.pallas_call(
        _in_proj_kernel,
        grid=(m // tm, U_WIDTH // tn),
        in_specs=[
            pl.BlockSpec((tm, D_MODEL), lambda i, j: (i, 0)),
            pl.BlockSpec((1, D_MODEL), lambda i, j: (0, 0)),
            pl.BlockSpec((D_MODEL, tn), lambda i, j: (0, j)),
        ],
        out_specs=pl.BlockSpec((tm, tn), lambda i, j: (i, j)),
        out_shape=jax.ShapeDtypeStruct((m, U_WIDTH), BF16),
        scratch_shapes=[pltpu.VMEM((tm, D_MODEL), BF16)],
        compiler_params=_params(("parallel", "arbitrary")),
        name="in_proj",
    )(x2, g_pre.reshape(1, D_MODEL), w_in_p)


def _group_queries(x, group, low):
    xr = pltpu.roll(x, HEAD_DIM, 1)
    zero = jnp.zeros_like(x)
    if group == 0:
        return [jnp.where(low, x, zero), jnp.where(low, xr, zero)]
    return [jnp.where(low, zero, xr), jnp.where(low, zero, x)]


def _win_attn_kernel(q_ref, kp_ref, kc_ref, kn_ref, vp_ref, vc_ref, vn_ref, g_ref, bias_ref,
                     sink_ref, o_ref, *, nq, ahead):
    i = pl.program_id(1)
    n = pl.num_programs(1)
    low = _lane_is_low()
    kext = jnp.concatenate([kp_ref[0], kc_ref[0], kn_ref[0]], axis=0)
    vts = [vp_ref[0, 0]] + [vc_ref[0, b] for b in range(nq)] + [vn_ref[0, 0]]
    units = [(qb, grp) for qb in range(nq) for grp in range(2)]

    def qk(u):
        qb, grp = units[u]
        rows = slice(qb * A_BLOCK, (qb + 1) * A_BLOCK)
        heads = []
        for pair in (2 * grp, 2 * grp + 1):
            heads += _group_queries(q_ref[0, rows, pair * LANES:(pair + 1) * LANES], grp, low)
        return _dot_nt(kext[qb * A_BLOCK:(qb + 3) * A_BLOCK], jnp.concatenate(heads, axis=0))

    def softmax(u, st):
        qb, grp = units[u]
        variant = 1
        if qb == 0:
            variant = jnp.where(i == 0, 0, variant)
        if qb == nq - 1:
            variant = jnp.where(i == n - 1, 2, variant)
        sb = _same_bits(st) + bias_ref[variant, grp]
        m = jnp.maximum(jnp.max(sb, axis=0, keepdims=True), sink_ref[grp])
        return m, jnp.exp2(sb - m).astype(BF16)

    def pv(u, probs):
        qb, grp = units[u]
        m, pt = probs
        vt = jnp.concatenate([vts[qb + d][grp] for d in range(3)], axis=1)
        acc = _dot(vt, pt)
        ot = acc[:HEAD_DIM] / (acc[HEAD_DIM:HEAD_DIM + 1] + jnp.exp2(sink_ref[grp] - m))
        rows = slice(qb * A_BLOCK, (qb + 1) * A_BLOCK)
        for pair in range(2):
            yt = jnp.concatenate([ot[:, (2 * pair) * A_BLOCK:(2 * pair + 1) * A_BLOCK],
                                  ot[:, (2 * pair + 1) * A_BLOCK:(2 * pair + 2) * A_BLOCK]], axis=0)
            cols = slice((2 * grp + pair) * LANES, (2 * grp + pair + 1) * LANES)
            gate = g_ref[0, rows, cols].astype(F32)
            o_ref[0, rows, cols] = (yt.T * _silu(gate)).astype(o_ref.dtype)

    _flash_units(len(units), ahead, qk, softmax, pv)


def _win_attn(u3, vta, bias_a, sink_a, tq, ahead):
    bsz, s_len, _ = u3.shape
    nq = tq // A_BLOCK
    nblk = s_len // A_BLOCK
    wide = (1, tq, 4 * LANES)
    const = lambda shape: pl.BlockSpec(shape, lambda bi, i: (0,) * len(shape), pipeline_mode=pl.Buffered(1))
    return pl.pallas_call(
        functools.partial(_win_attn_kernel, nq=nq, ahead=ahead),
        grid=(bsz, s_len // tq),
        in_specs=[
            pl.BlockSpec(wide, lambda bi, i: (bi, i, COL_AQ // 4)),
            pl.BlockSpec((1, A_BLOCK, LANES), lambda bi, i: (bi, jnp.maximum(i * nq - 1, 0), COL_AK)),
            pl.BlockSpec((1, tq, LANES), lambda bi, i: (bi, i, COL_AK)),
            pl.BlockSpec((1, A_BLOCK, LANES), lambda bi, i: (bi, jnp.minimum((i + 1) * nq, nblk - 1), COL_AK)),
            pl.BlockSpec((1, 1, 2, KV_ROWS, A_BLOCK), lambda bi, i: (bi, jnp.maximum(i * nq - 1, 0), 0, 0, 0)),
            pl.BlockSpec((1, nq, 2, KV_ROWS, A_BLOCK), lambda bi, i: (bi, i, 0, 0, 0)),
            pl.BlockSpec((1, 1, 2, KV_ROWS, A_BLOCK),
                         lambda bi, i: (bi, jnp.minimum((i + 1) * nq, nblk - 1), 0, 0, 0)),
            pl.BlockSpec(wide, lambda bi, i: (bi, i, COL_AG // 4)),
            const(bias_a.shape),
            const(sink_a.shape),
        ],
        out_specs=pl.BlockSpec(wide, lambda bi, i: (bi, i, 0)),
        out_shape=jax.ShapeDtypeStruct((bsz, s_len, 4 * LANES), BF16),
        compiler_params=_params(("parallel", "parallel")),
        name="win_attn",
    )(u3, u3, u3, u3, vta, vta, vta, u3, bias_a, sink_a)


def _pool_kernel(xp_ref, xc_ref, xn_ref, g_ref, band_ref, w_ref, sc_ref, o_ref, *, s_len, tq):
    i = pl.program_id(1)
    n = pl.num_programs(1)
    xp = xp_ref[0]
    xn = xn_ref[0]
    xp = jnp.where(i > 0, xp, jnp.zeros_like(xp))
    xn = jnp.where(i < n - 1, xn, jnp.zeros_like(xn))
    xc = xc_ref[0]
    xext = jnp.concatenate([xp, xc, xn], axis=0)
    t = i * tq + lax.broadcasted_iota(jnp.int32, (tq, 1), 0)
    for gi, size in enumerate(B_POOL_SIZES):
        lo = size // 2
        hi = size - lo - 1
        lanes = slice(gi * LANES, (gi + 1) * LANES)
        win_sum = _dot(band_ref[gi], xext[:, lanes])
        start = jnp.maximum(t - lo, 0)
        end = jnp.minimum(t + hi, s_len - 1) + 1
        pooled = win_sum / (end - start).astype(F32) - xc[:, lanes].astype(F32)
        y = _dot(pooled.astype(BF16), w_ref[gi]) * sc_ref[:, lanes]
        gate = g_ref[0, :, lanes].astype(F32)
        o_ref[0, :, lanes] = (y * _silu(gate)).astype(o_ref.dtype)


def _pool_bands(tq):
    r = jnp.arange(tq)[:, None]
    c = jnp.arange(tq + 2 * B_HALO)[None, :] - B_HALO
    bands = []
    for size in B_POOL_SIZES:
        lo = size // 2
        hi = size - lo - 1
        bands.append(((c - r >= -lo) & (c - r <= hi)).astype(BF16))
    return jnp.stack(bands)


def _pool(u3, pool_w, pool_scale, tq):
    bsz, s_len, _ = u3.shape
    nt = s_len // tq
    per = tq // B_HALO
    nh = s_len // B_HALO
    wide = (1, tq, 4 * LANES)
    halo = (1, B_HALO, 4 * LANES)
    return pl.pallas_call(
        functools.partial(_pool_kernel, s_len=s_len, tq=tq),
        grid=(bsz, nt),
        in_specs=[
            pl.BlockSpec(halo, lambda bi, i: (bi, jnp.maximum(i * per - 1, 0), COL_BX // 4)),
            pl.BlockSpec(wide, lambda bi, i: (bi, i, COL_BX // 4)),
            pl.BlockSpec(halo, lambda bi, i: (bi, jnp.minimum((i + 1) * per, nh - 1), COL_BX // 4)),
            pl.BlockSpec(wide, lambda bi, i: (bi, i, COL_BG // 4)),
            pl.BlockSpec((B_GROUPS, tq, tq + 2 * B_HALO), lambda bi, i: (0, 0, 0)),
            pl.BlockSpec((B_GROUPS, LANES, LANES), lambda bi, i: (0, 0, 0)),
            pl.BlockSpec((1, 4 * LANES), lambda bi, i: (0, 0)),
        ],
        out_specs=pl.BlockSpec(wide, lambda bi, i: (bi, i, 0)),
        out_shape=jax.ShapeDtypeStruct((bsz, s_len, 4 * LANES), BF16),
        compiler_params=_params(("parallel", "parallel")),
        name="pool",
    )(u3, u3, u3, u3, _pool_bands(tq), pool_w.astype(BF16), pool_scale.reshape(1, 4 * LANES))


def _softmax_t(st, m):
    m_new = jnp.maximum(m, jnp.max(st, axis=0, keepdims=True))
    return m_new, jnp.exp2(m - m_new), jnp.exp2(st - m_new).astype(BF16)


def _flash_units(n_units, ahead, qk, softmax, pv):
    scores = {u: qk(u) for u in range(min(ahead, n_units))}
    for u in range(n_units):
        probs = softmax(u, scores.pop(u))
        if u + ahead < n_units:
            scores[u + ahead] = qk(u + ahead)
        pv(u, probs)


def _same_bits(x):
    return lax.bitcast_convert_type(lax.bitcast_convert_type(x, jnp.int32), F32)


def _transposed(x):
    return x.astype(F32).T.astype(BF16)


def _masked_halves(q, low):
    zero = jnp.zeros_like(q)
    return jnp.where(low, q, zero), jnp.where(low, zero, q)


def _diff_attn_kernel(q_ref, k_ref, vt_ref, g_ref, bias_ref, stat_ref, lam_ref, sub_ref, o_ref,
                      *, tq, tk, nb, n_iter, ahead, lambda_init, bounded, n_sub):
    for sub in range(n_sub):
        _diff_attn_tile(q_ref, k_ref, vt_ref, g_ref, bias_ref, stat_ref, lam_ref, sub_ref, o_ref, sub,
                        pl.program_id(2) * n_sub + sub, tq=tq, tk=tk, nb=nb, n_iter=n_iter, ahead=ahead,
                        lambda_init=lambda_init, bounded=bounded)


def _diff_attn_tile(q_ref, k_ref, vt_ref, g_ref, bias_ref, stat_ref, lam_ref, sub_ref, o_ref, sub, i,
                    *, tq, tk, nb, n_iter, ahead, lambda_init, bounded):
    rows_q = slice(sub * tq, (sub + 1) * tq)
    ratio = tq // tk
    n_tab = bias_ref.shape[1]
    rows = vt_ref.shape[-2]
    low = _lane_is_low()
    q = q_ref[0, rows_q]
    qs = _masked_halves(q, low)
    bounds = None
    if bounded:
        qf = q.astype(F32)
        halves = jnp.concatenate([jnp.where(low, 1.0, 0.0), jnp.where(low, 0.0, 1.0)] +
                                 [jnp.zeros((1, LANES), F32)] * 6, axis=0).astype(BF16)
        norm2 = _dot_nt(halves, (qf * qf).astype(BF16))
        bounds = [jnp.sqrt(norm2[h:h + 1]) * stat_ref[0, 0, h:h + 1, 0:1] + stat_ref[0, 0, 2:3, 0:1]
                  for h in range(2)]

    def step(j, carry):
        state = list(carry)
        blocks = [nb * j + b for b in range(nb)]
        ks = [k_ref[0, pl.ds(pl.multiple_of(blk * tk, tk), tk), :] for blk in blocks]
        vts = [vt_ref[0, 0, blk] for blk in blocks]
        tabs = [jnp.clip(blk - ratio * i + 2, 0, n_tab - 1) for blk in blocks]

        def qk(u):
            return _dot_nt(ks[u // 2], qs[u % 2])

        def softmax(u, st):
            sb = _same_bits(st) + bias_ref[0, tabs[u // 2]]
            if bounded:
                return None, None, jnp.exp2(sb - bounds[u % 2]).astype(BF16)
            return _softmax_t(sb, state[u % 2][0])

        def pv(u, probs):
            m_new, alpha, pt = probs
            acc = state[u % 2][1]
            state[u % 2] = (m_new, (acc if bounded else alpha * acc) + _dot(vts[u // 2], pt))

        _flash_units(2 * nb, ahead, qk, softmax, pv)
        return tuple(state)

    m0 = None if bounded else jnp.full((1, tq), NEG_INF, F32)
    init = tuple((m0, jnp.zeros((rows, tq), F32)) for _ in range(2))
    (_, a1), (_, a2) = lax.fori_loop(0, n_iter, step, init)

    dot1 = jnp.sum(lam_ref[0:1, :] * lam_ref[1:2, :], axis=-1, keepdims=True)
    dot2 = jnp.sum(lam_ref[2:3, :] * lam_ref[3:4, :], axis=-1, keepdims=True)
    lam = jnp.exp(dot1) - jnp.exp(dot2) + lambda_init
    yt = a1[:LANES] / a1[LANES:LANES + 1] - lam * (a2[:LANES] / a2[LANES:LANES + 1])
    y = yt.T
    ms = jnp.mean(y * y, axis=-1, keepdims=True)
    y = y * lax.rsqrt(ms + EPS) * sub_ref[...] * (1.0 - lambda_init)
    o_ref[0, rows_q] = (y * _silu(g_ref[0, rows_q].astype(F32))).astype(o_ref.dtype)


def _diff_attn(u3, vtc, bias_c, stat_c, lam_vecs, subln, lambda_init, tq, tk, nb, ahead, bounded):
    bsz, s_len, _ = u3.shape
    nk = s_len // tk
    nb = min(nb, nk)
    rows = vtc.shape[-2]
    n_tab = bias_c.shape[1]
    n_sub = 2 if s_len % (2 * tq) == 0 else 1
    return pl.pallas_call(
        functools.partial(_diff_attn_kernel, tq=tq, tk=tk, nb=nb, n_iter=nk // nb, ahead=ahead,
                          lambda_init=lambda_init, bounded=bounded, n_sub=n_sub),
        grid=(bsz, C_HEADS, s_len // (tq * n_sub)),
        in_specs=[
            pl.BlockSpec((1, tq * n_sub, LANES), lambda bi, h, i: (bi, i, COL_CQ + h)),
            pl.BlockSpec((1, s_len, LANES), lambda bi, h, i: (bi, 0, COL_CK + h)),
            pl.BlockSpec((1, 1, nk, rows, tk), lambda bi, h, i: (bi, h, 0, 0, 0)),
            pl.BlockSpec((1, tq * n_sub, LANES), lambda bi, h, i: (bi, i, COL_CG + h)),
            pl.BlockSpec((1, n_tab, tk, tq), lambda bi, h, i: (h, 0, 0, 0)),
            pl.BlockSpec((1, 1, 8, LANES), lambda bi, h, i: (bi, h, 0, 0)),
            pl.BlockSpec((4, HEAD_DIM), lambda bi, h, i: (0, 0)),
            pl.BlockSpec((1, LANES), lambda bi, h, i: (0, 0)),
        ],
        out_specs=pl.BlockSpec((1, tq * n_sub, LANES), lambda bi, h, i: (bi, i, h)),
        out_shape=jax.ShapeDtypeStruct((bsz, s_len, 4 * LANES), BF16),
        compiler_params=_params(("parallel", "parallel", "parallel")),
        name="diff_attn_bounded" if bounded else "diff_attn",
    )(u3, u3, vtc, u3, bias_c, stat_c, lam_vecs, subln.reshape(1, LANES))


def _grid_attn_kernel(q_ref, k_ref, vt_ref, g_ref, bound_ref, o_ref, *, n_sub, **tile_args):
    for sub in range(n_sub):
        _grid_attn_tile(q_ref, k_ref, vt_ref, g_ref, bound_ref, o_ref, sub, **tile_args)


def _grid_attn_tile(q_ref, k_ref, vt_ref, g_ref, bound_ref, o_ref, sub, *, tq, tk, nb, n_iter, ahead, bounded):
    rows_q = slice(sub * tq, (sub + 1) * tq)
    grp = pl.program_id(1)
    lane = lax.broadcasted_iota(jnp.int32, (1, LANES), 1)
    keep = (lane >= grp * HEAD_DIM) & (lane < (grp + 1) * HEAD_DIM)
    rows = vt_ref.shape[-2]
    bound = bound_ref[0:1, 0:1]
    qs = []
    for pair in range(2):
        x = q_ref[0, rows_q, pair * LANES:(pair + 1) * LANES]
        xr = pltpu.roll(x, HEAD_DIM, 1)
        zero = jnp.zeros_like(x)
        qs.append(_transposed(jnp.where(keep, jnp.where(grp == 0, x, xr), zero)))
        qs.append(_transposed(jnp.where(keep, jnp.where(grp == 0, xr, x), zero)))

    def step(j, carry):
        state = list(carry)
        blocks = [nb * j + b for b in range(nb)]
        ks = [k_ref[0, pl.ds(pl.multiple_of(blk * tk, tk), tk), :] for blk in blocks]
        vts = [vt_ref[0, 0, blk] for blk in blocks]

        def qk(u):
            return _dot(ks[u // 4], qs[u % 4])

        def softmax(u, st):
            if bounded:
                return None, None, jnp.exp2(st - bound).astype(BF16)
            return _softmax_t(st, state[u % 4][0])

        def pv(u, probs):
            m_new, alpha, pt = probs
            acc = state[u % 4][1]
            state[u % 4] = (m_new, (acc if bounded else alpha * acc) + _dot(vts[u // 4], pt))

        _flash_units(4 * nb, ahead, qk, softmax, pv)
        return tuple(state)

    m0 = None if bounded else jnp.full((1, tq), NEG_INF, F32)
    init = tuple((m0, jnp.zeros((rows, tq), F32)) for _ in range(4))
    state = lax.fori_loop(0, n_iter, step, init)
    for pair in range(2):
        a0, a1 = state[2 * pair][1], state[2 * pair + 1][1]
        yt = jnp.concatenate([a0[:HEAD_DIM] / a0[HEAD_DIM:HEAD_DIM + 1],
                              a1[:HEAD_DIM] / a1[HEAD_DIM:HEAD_DIM + 1]], axis=0)
        gate = g_ref[0, rows_q, pair * LANES:(pair + 1) * LANES].astype(F32)
        o_ref[0, rows_q, pair * LANES:(pair + 1) * LANES] = (yt.T * _silu(gate)).astype(o_ref.dtype)


def _grid_attn(qd3, kd3, vtd, u3, bound_d, tq, tk, nb, ahead, bounded):
    bsz, s_len, _ = qd3.shape
    nk = s_len // tk
    nb = min(nb, nk)
    rows = vtd.shape[-2]
    n_sub = 2 if nk <= 16 and s_len % (2 * tq) == 0 else 1
    return pl.pallas_call(
        functools.partial(_grid_attn_kernel, tq=tq, tk=tk, nb=nb, n_iter=nk // nb, ahead=ahead,
                          bounded=bounded, n_sub=n_sub),
        grid=(bsz, 2, s_len // (tq * n_sub)),
        in_specs=[
            pl.BlockSpec((1, tq * n_sub, 2 * LANES), lambda bi, h, i: (bi, i, h)),
            pl.BlockSpec((1, s_len, LANES), lambda bi, h, i: (bi, 0, 0)),
            pl.BlockSpec((1, 1, nk, rows, tk), lambda bi, h, i: (bi, h, 0, 0, 0)),
            pl.BlockSpec((1, tq * n_sub, 2 * LANES), lambda bi, h, i: (bi, i, COL_DG // 2 + h)),
            pl.BlockSpec((8, LANES), lambda bi, h, i: (0, 0)),
        ],
        out_specs=pl.BlockSpec((1, tq * n_sub, 2 * LANES), lambda bi, h, i: (bi, i, h)),
        out_shape=jax.ShapeDtypeStruct((bsz, s_len, 4 * LANES), BF16),
        compiler_params=_params(("parallel", "parallel", "parallel")),
        name="grid_attn_bounded" if bounded else "grid_attn",
    )(qd3, kd3, vtd, u3, bound_d)


def _norm_rope(x, gain, cos, sin, seg):
    w = x.shape[-1]
    x2 = x * x
    hi = x2.astype(BF16)
    lo = (x2 - hi.astype(F32)).astype(BF16)
    ss = _dot(hi, seg) + _dot(lo, seg)
    y = x * lax.rsqrt(ss * (1.0 / HEAD_DIM) + EPS) * gain
    quarter = HEAD_DIM // 4
    ahead = pltpu.roll(y, w - quarter, 1)
    behind = pltpu.roll(y, quarter, 1)
    first = (lax.broadcasted_iota(jnp.int32, (1, w), 1) % (2 * quarter)) < quarter
    rot = jnp.where(first, -ahead, behind)
    reps = w // LANES
    cos_w = jnp.concatenate([cos] * reps, axis=1)
    sin_w = jnp.concatenate([sin] * reps, axis=1)
    return y * cos_w + rot * sin_w


def _attn_prep_kernel(q_ref, k_ref, vc_ref, vd_ref, va_ref, qc_ref, kc_ref, cos_ref, sin_ref, gq_ref, gk_ref,
                      seg_ref, qo_ref, ko_ref, vtc_ref, vtd_ref, vta_ref, qn_ref, kn_ref):
    cos = cos_ref[...]
    sin = sin_ref[...]
    seg = seg_ref[...]
    q = _norm_rope(q_ref[...].astype(F32), gq_ref[...], cos, sin, seg)
    qo_ref[...] = (q * (SCALE * LOG2E)).astype(qo_ref.dtype)
    k = _norm_rope(k_ref[...].astype(F32), gk_ref[...], cos, sin, seg[:LANES, :LANES])
    ko_ref[...] = k.astype(ko_ref.dtype)
    tm = q.shape[0]
    ones = jnp.ones((VT_ONES, tm), BF16)
    vct = vc_ref[...].astype(F32).T
    for h in range(C_HEADS):
        vtc_ref[0, h, 0] = jnp.concatenate([vct[h * LANES:(h + 1) * LANES].astype(BF16), ones], axis=0)
    vdt = vd_ref[...].astype(F32).T.astype(BF16)
    vat = va_ref[...].astype(F32).T.astype(BF16)
    for h in range(2):
        vtd_ref[0, h, 0] = jnp.concatenate([vdt[h * HEAD_DIM:(h + 1) * HEAD_DIM], ones], axis=0)
        for kb in range(tm // A_BLOCK):
            cols = slice(kb * A_BLOCK, (kb + 1) * A_BLOCK)
            vta_ref[0, kb, h] = jnp.concatenate([vat[h * HEAD_DIM:(h + 1) * HEAD_DIM, cols], ones[:, cols]], axis=0)
    for src, dst in ((qc_ref, qn_ref), (kc_ref, kn_ref)):
        x = src[...].astype(F32)
        seg_sums = _dot((x * x).astype(BF16), seg)
        dst[0] = jnp.broadcast_to(jnp.max(seg_sums, axis=0, keepdims=True), (8, 4 * LANES))


def _attn_prep(u2, cos, sin, qnorm, knorm, bsz, s_len, tm):
    m = u2.shape[0]
    per_seq = s_len // tm
    per_tile = tm // A_BLOCK
    lane = jnp.arange(4 * LANES)
    seg = (lane[:, None] // HEAD_DIM == lane[None, :] // HEAD_DIM).astype(BF16)
    rows_c = LANES + VT_ONES
    return pl.pallas_call(
        _attn_prep_kernel,
        grid=(m // tm,),
        in_specs=[
            pl.BlockSpec((tm, 4 * LANES), lambda i: (i, COL_DQ // 4)),
            pl.BlockSpec((tm, LANES), lambda i: (i, COL_DK)),
            pl.BlockSpec((tm, 4 * LANES), lambda i: (i, COL_CV // 4)),
            pl.BlockSpec((tm, LANES), lambda i: (i, COL_DV)),
            pl.BlockSpec((tm, LANES), lambda i: (i, COL_AV)),
            pl.BlockSpec((tm, 4 * LANES), lambda i: (i, COL_CQ // 4)),
            pl.BlockSpec((tm, 4 * LANES), lambda i: (i, COL_CK // 4)),
            pl.BlockSpec((tm, LANES), lambda i: (i % per_seq, 0)),
            pl.BlockSpec((tm, LANES), lambda i: (i % per_seq, 0)),
            pl.BlockSpec((1, 4 * LANES), lambda i: (0, 0)),
            pl.BlockSpec((1, LANES), lambda i: (0, 0)),
            pl.BlockSpec((4 * LANES, 4 * LANES), lambda i: (0, 0)),
        ],
        out_specs=[
            pl.BlockSpec((tm, 4 * LANES), lambda i: (i, 0)),
            pl.BlockSpec((tm, LANES), lambda i: (i, 0)),
            pl.BlockSpec((1, C_HEADS, 1, rows_c, tm), lambda i: (i // per_seq, 0, i % per_seq, 0, 0)),
            pl.BlockSpec((1, 2, 1, KV_ROWS, tm), lambda i: (i // per_seq, 0, i % per_seq, 0, 0)),
            pl.BlockSpec((1, per_tile, 2, KV_ROWS, A_BLOCK), lambda i: (i // per_seq, i % per_seq, 0, 0, 0)),
            pl.BlockSpec((1, 8, 4 * LANES), lambda i: (i, 0, 0)),
            pl.BlockSpec((1, 8, 4 * LANES), lambda i: (i, 0, 0)),
        ],
        out_shape=[jax.ShapeDtypeStruct((m, 4 * LANES), BF16),
                   jax.ShapeDtypeStruct((m, LANES), BF16),
                   jax.ShapeDtypeStruct((bsz, C_HEADS, per_seq, rows_c, tm), BF16),
                   jax.ShapeDtypeStruct((bsz, 2, per_seq, KV_ROWS, tm), BF16),
                   jax.ShapeDtypeStruct((bsz, s_len // A_BLOCK, 2, KV_ROWS, A_BLOCK), BF16),
                   jax.ShapeDtypeStruct((m // tm, 8, 4 * LANES), F32),
                   jax.ShapeDtypeStruct((m // tm, 8, 4 * LANES), F32)],
        compiler_params=_params(("parallel",)),
        name="attn_prep",
    )(u2, u2, u2, u2, u2, u2, u2, cos, sin, jnp.tile(qnorm, 8).reshape(1, 4 * LANES),
      jnp.tile(knorm, 2).reshape(1, LANES), seg)


def _out_kernel(x_ref, ya_ref, yb_ref, yc_ref, yd_ref, pe_ref, wo_ref, gp_ref, wpe_ref, wpg_ref, o_ref):
    width = 4 * LANES
    z = _dot(ya_ref[...], wo_ref[0 * width:1 * width, :])
    z += _dot(yb_ref[...], wo_ref[1 * width:2 * width, :])
    z += _dot(yc_ref[...], wo_ref[2 * width:3 * width, :])
    z += _dot(yd_ref[...], wo_ref[3 * width:4 * width, :])
    ms = jnp.mean(z * z, axis=-1, keepdims=True)
    x = x_ref[...] + z * lax.rsqrt(ms + EPS) * gp_ref[...]
    logits = _dot(x.astype(BF16), wpg_ref[...])
    gate = 1.0 / (1.0 + jnp.exp(-logits))
    pe = _dot(pe_ref[0].astype(BF16), wpe_ref[...])
    o_ref[...] = x + pe * gate


def _out_stage(x2, ya, yb, yc, yd, p3, layer_idx, w_o, g_post, w_pe, w_pg, tm):
    m = x2.shape[0]
    row = lambda w: pl.BlockSpec((tm, w), lambda i: (i, 0))
    const = lambda r, c: pl.BlockSpec((r, c), lambda i: (0, 0), pipeline_mode=pl.Buffered(1))
    return pl.pallas_call(
        _out_kernel,
        grid=(m // tm,),
        in_specs=[row(D_MODEL), row(4 * LANES), row(4 * LANES), row(4 * LANES), row(4 * LANES),
                  pl.BlockSpec((1, tm, PLE_DIM), lambda i: (layer_idx, i, 0)), const(MIX_WIDTH, D_MODEL),
                  const(1, D_MODEL),
                  const(PLE_DIM, D_MODEL), const(D_MODEL, D_MODEL)],
        out_specs=row(D_MODEL),
        out_shape=jax.ShapeDtypeStruct((m, D_MODEL), F32),
        compiler_params=_params(("parallel",)),
        name="out_stage",
    )(x2, ya, yb, yc, yd, p3, w_o, g_post.reshape(1, D_MODEL), w_pe, w_pg)


def _prep_w_in(w):
    parts, start = [], 0
    for n in IN_SIZES:
        parts.append(w[:, start:start + n])
        start += n
    aq, ak, av, ag, bx, bg, cq, ck, cv, cg, dq, dk, dv, dg = parts

    log2_scale = SCALE * LOG2E
    cols = [aq * log2_scale, ag, bx, bg, cq * log2_scale, ck, cv, cg, dq, dg, ak, av, dk, dv]
    return jnp.concatenate(cols, axis=1).astype(BF16)


def _rel_bucket(rel):
    nb = REL_BUCKETS // 2
    max_exact = nb // 2
    ret = jnp.where(rel > 0, nb, 0)
    n = jnp.abs(rel)
    nf = jnp.maximum(n, 1).astype(F32)
    large = max_exact + (jnp.log(nf / max_exact) / math.log(REL_MAX_DIST / max_exact)
                         * (nb - max_exact)).astype(jnp.int32)
    large = jnp.minimum(large, nb - 1)
    return ret + jnp.where(n < max_exact, n, large)


def _bucket_values(tab, rel):
    bucket = _rel_bucket(rel)[None]
    out = jnp.zeros((tab.shape[1],) + rel.shape, F32)
    for b in range(REL_BUCKETS):
        out = jnp.where(bucket == b, tab[b].reshape((-1,) + (1,) * rel.ndim), out)
    return out


def _bias_tables(rel_bias, tq, tk):
    tab = rel_bias.astype(F32)
    ck = jnp.arange(3 * A_BLOCK)[:, None]
    rq = jnp.arange(A_BLOCK)[None, :]
    rel_a = ck - A_BLOCK - rq
    vals = _bucket_values(tab[:, :A_HEADS], rel_a) * LOG2E
    in_band = jnp.abs(rel_a) <= A_WINDOW
    variants = []
    for edge in (ck >= A_BLOCK, ck >= 0, ck < 2 * A_BLOCK):
        masked = jnp.where(in_band & edge, vals, NEG_INF)
        variants.append(jnp.transpose(masked.reshape(2, 4, 3 * A_BLOCK, A_BLOCK), (0, 2, 1, 3))
                        .reshape(2, 3 * A_BLOCK, 4 * A_BLOCK))
    bias_a = jnp.stack(variants)
    assert tk >= REL_MAX_DIST
    n_tab = tq // tk + 4
    d = jnp.arange(n_tab)[:, None, None]
    c = jnp.arange(tk)[None, :, None]
    r = jnp.arange(tq)[None, None, :]
    bias_c = _bucket_values(tab[:, A_HEADS:], c + (d - 2) * tk - r) * LOG2E
    return bias_a, bias_c


def _axial_rope(s_len):
    rows = s_len // GRID_W
    t_row = jnp.repeat(jnp.arange(rows), GRID_W).astype(F32)
    t_col = jnp.tile(jnp.arange(GRID_W), rows).astype(F32)
    half = HEAD_DIM // 2
    inv = ROPE_THETA ** (-jnp.arange(0, half, 2, dtype=F32) / half)
    ang_r = t_row[:, None] * inv[None, :]
    ang_c = t_col[:, None] * inv[None, :]
    ang = jnp.concatenate([ang_r, ang_r, ang_c, ang_c] * 2, axis=-1)
    return jnp.cos(ang), jnp.sin(ang)


def _tiles(s_len):
    return dict(tm_in=min(512, s_len), tn_in=U_WIDTH // 2, tq_pool=min(256, s_len), tm_out=min(256, s_len),
                tq_attn=512, tk_attn=256, nb_c=16, nb_d=32, ahead_c=3, ahead_d=3, tq_win=1024, ahead_win=1)


def _layer(x, p, layer_idx, tables, rope, w, tiles):
    bsz, s_len, _ = x.shape
    m = bsz * s_len
    bias_a, bias_c = tables
    cos, sin = rope
    tq, tk = tiles["tq_attn"], tiles["tk_attn"]
    x2 = x.reshape(m, D_MODEL)
    u2 = _in_proj(x2, w["g_pre"], w["w_in"], tiles["tm_in"], tiles["tn_in"])
    u3 = u2.reshape(bsz, s_len, U_WIDTH)
    qd, kd, vtc, vtd, vta, qn, kn = _attn_prep(u2, cos, sin, w["qnorm_d"], w["knorm_d"], bsz, s_len, tk)
    ya = _win_attn(u3, vta, bias_a, w["sink"], tiles["tq_win"], tiles["ahead_win"])
    yb = _pool(u3, w["pool_w"], w["pool_scale"], tiles["tq_pool"])
    lambda_init = 0.8 - 0.6 * math.exp(-0.3 * layer_idx)
    span = math.log2(s_len)

    seg_max = lambda t: jnp.sqrt(jnp.max(t.reshape(bsz, -1, 8, C_HEADS, 2, HEAD_DIM)[:, :, 0, :, :, 0], axis=1))
    q_max, k_max = seg_max(qn), seg_max(kn) * BOUND_SLACK
    b_hi, b_lo = jnp.max(bias_c, axis=(1, 2, 3)), jnp.min(bias_c, axis=(1, 2, 3))
    stat_c = jnp.concatenate([k_max, jnp.broadcast_to(b_hi[None, :, None], (bsz, C_HEADS, 1)),
                              jnp.zeros((bsz, C_HEADS, 5), F32)], axis=2)
    stat_c = jnp.broadcast_to(stat_c[..., None], (bsz, C_HEADS, 8, LANES))
    fits_c = 2.0 * jnp.max(q_max * k_max) + jnp.max(b_hi - b_lo) + span <= MAX_EXP2_SPAN
    attend_c = lambda bounded: _diff_attn(u3, vtc, bias_c, stat_c, w["lam"], w["diff_subln"], lambda_init,
                                          tq, tk, tiles["nb_c"], tiles["ahead_c"], bounded)
    yc = lax.cond(fits_c, lambda: attend_c(True), lambda: attend_c(False))

    bound_d = (HEAD_DIM * SCALE * LOG2E * BOUND_SLACK
               * jnp.max(jnp.abs(w["qnorm_d"])) * jnp.max(jnp.abs(w["knorm_d"])))
    fits_d = 2.0 * bound_d + span <= MAX_EXP2_SPAN
    attend_d = lambda bounded: _grid_attn(qd.reshape(bsz, s_len, -1), kd.reshape(bsz, s_len, -1), vtd, u3,
                                          jnp.full((8, LANES), bound_d, F32), tq, tk, tiles["nb_d"],
                                          tiles["ahead_d"], bounded)
    yd = lax.cond(fits_d, lambda: attend_d(True), lambda: attend_d(False))
    flat = lambda y: y.reshape(m, 4 * LANES)
    out = _out_stage(x2, flat(ya), flat(yb), flat(yc), flat(yd), p.reshape(-1, m, PLE_DIM), layer_idx,
                     w["w_o"], w["g_post"], w["w_pe"], w["w_pg"], tiles["tm_out"])
    return out.reshape(bsz, s_len, D_MODEL)


def _trunk(x, p, rel_bias, layers, tiles=None):
    s_len = x.shape[1]
    tiles = tiles or _tiles(s_len)
    tables = _bias_tables(rel_bias, tiles["tq_attn"], tiles["tk_attn"])
    rope = _axial_rope(s_len)
    for i, w in enumerate(layers):
        x = _layer(x, p, i, tables, rope, w, tiles)
    return x


def _prep_layers(w_in, w_o, g_pre, g_post, sink_a, pool_w, pool_scale, lam_q1, lam_k1, lam_q2, lam_k2,
                 diff_subln, qnorm_d, knorm_d, w_pe, w_pg):
    layers = []
    for i in range(w_in.shape[0]):
        layers.append(dict(
            w_in=_prep_w_in(w_in[i]), w_o=w_o[i].astype(BF16), g_pre=g_pre[i], g_post=g_post[i],
            sink=jnp.repeat(sink_a[i].astype(F32) * LOG2E, A_BLOCK).reshape(2, 1, 4 * A_BLOCK),
            pool_w=pool_w[i], pool_scale=pool_scale[i],
            lam=jnp.stack([lam_q1[i], lam_k1[i], lam_q2[i], lam_k2[i]]).astype(F32),
            diff_subln=diff_subln[i], qnorm_d=qnorm_d[i], knorm_d=knorm_d[i],
            w_pe=w_pe[i].astype(BF16), w_pg=w_pg[i].astype(BF16)))
    return layers


def kernel(x_prompt, x_sample, p_prompt, p_sample, w_in, w_o, g_pre, g_post, sink_a, pool_w, pool_scale,
           lam_q1, lam_k1, lam_q2, lam_k2, diff_subln, qnorm_d, knorm_d, rel_bias, w_pe, w_pg):
    layers = _prep_layers(w_in, w_o, g_pre, g_post, sink_a, pool_w, pool_scale, lam_q1, lam_k1, lam_q2,
                          lam_k2, diff_subln, qnorm_d, knorm_d, w_pe, w_pg)
    y_prompt = _trunk(x_prompt, p_prompt, rel_bias, layers)
    y_sample = _trunk(x_sample, p_sample, rel_bias, layers)
    return (y_prompt, y_sample)
```

```python
import functools
import math

import jax
import jax.numpy as jnp
from jax import lax
from jax.experimental import pallas as pl
from jax.experimental.pallas import tpu as pltpu

F32 = jnp.float32
BF16 = jnp.bfloat16

D_MODEL = 2048
HEAD_DIM = 64
LANES = 128
PLE_DIM = 256
EPS = 1e-6
NEG_INF = -1e30
SCALE = HEAD_DIM ** -0.5
LOG2E = math.log2(math.e)
VT_ONES = 16
KV_ROWS = HEAD_DIM + VT_ONES
BOUND_SLACK = 1.02
MAX_EXP2_SPAN = 100.0

A_HEADS = 8
A_WINDOW = 128
A_BLOCK = 128
B_GROUPS = 4
B_POOL_SIZES = (2, 4, 8, 16)
B_HALO = 64
C_HEADS = 4
REL_BUCKETS = 32
REL_MAX_DIST = 128
ROPE_THETA = 10000.0
GRID_W = 64

IN_SIZES = (512, 128, 128, 512, 512, 512, 512, 512, 512, 512, 512, 128, 128, 512)
MIX_WIDTH = 2048

U_WIDTH = 5632
COL_AQ, COL_AG, COL_BX, COL_BG = 0, 4, 8, 12
COL_CQ, COL_CK, COL_CV, COL_CG = 16, 20, 24, 28
COL_DQ, COL_DG = 32, 36
COL_AK, COL_AV, COL_DK, COL_DV = 40, 41, 42, 43

VMEM_LIMIT = 56 * 1024 * 1024


def _params(sem, vmem=VMEM_LIMIT):
    return pltpu.CompilerParams(dimension_semantics=sem, vmem_limit_bytes=vmem)


def _dot(a, b):
    return jnp.dot(a, b, preferred_element_type=F32)


def _dot_nt(a, b):
    return lax.dot_general(a, b, (((1,), (1,)), ((), ())), preferred_element_type=F32)


def _silu(x):
    return x / (1.0 + jnp.exp(-x))


def _lane_is_low(width=LANES):
    return (lax.broadcasted_iota(jnp.int32, (1, width), 1) % LANES) < HEAD_DIM


def _in_proj_kernel(x_ref, g_ref, w_ref, o_ref, h_ref):
    @pl.when(pl.program_id(1) == 0)
    def _():
        x = x_ref[...]
        ms = jnp.mean(x * x, axis=-1, keepdims=True)
        h_ref[...] = (x * lax.rsqrt(ms + EPS) * g_ref[...]).astype(BF16)

    o_ref[...] = _dot(h_ref[...], w_ref[...]).astype(o_ref.dtype)


def _in_proj(x2, g_pre, w_in_p, tm, tn):
    m = x2.shape[0]
    return pl.pallas_call(
        _in_proj_kernel,
        grid=(m // tm, U_WIDTH // tn),
        in_specs=[
            pl.BlockSpec((tm, D_MODEL), lambda i, j: (i, 0)),
            pl.BlockSpec((1, D_MODEL), lambda i, j: (0, 0)),
            pl.BlockSpec((D_MODEL, tn), lambda i, j: (0, j)),
        ],
        out_specs=pl.BlockSpec((tm, tn), lambda i, j: (i, j)),
        out_shape=jax.ShapeDtypeStruct((m, U_WIDTH), BF16),
        scratch_shapes=[pltpu.VMEM((tm, D_MODEL), BF16)],
        compiler_params=_params(("parallel", "arbitrary")),
        name="in_proj",
    )(x2, g_pre.reshape(1, D_MODEL), w_in_p)


def _group_queries(x, group, low):
    xr = pltpu.roll(x, HEAD_DIM, 1)
    zero = jnp.zeros_like(x)
    if group == 0:
        return [jnp.where(low, x, zero), jnp.where(low, xr, zero)]
    return [jnp.where(low, zero, xr), jnp.where(low, zero, x)]


def _win_attn_kernel(q_ref, kp_ref, kc_ref, kn_ref, vp_ref, vc_ref, vn_ref, g_ref, bias_ref,
                     sink_ref, o_ref, *, nq, ahead):
    i = pl.program_id(1)
    n = pl.num_programs(1)
    low = _lane_is_low()
    kext = jnp.concatenate([kp_ref[0], kc_ref[0], kn_ref[0]], axis=0)
    vts = [vp_ref[0, 0]] + [vc_ref[0, b] for b in range(nq)] + [vn_ref[0, 0]]
    units = [(qb, grp) for qb in range(nq) for grp in range(2)]

    def qk(u):
        qb, grp = units[u]
        rows = slice(qb * A_BLOCK, (qb + 1) * A_BLOCK)
        heads = []
        for pair in (2 * grp, 2 * grp + 1):
            heads += _group_queries(q_ref[0, rows, pair * LANES:(pair + 1) * LANES], grp, low)
        return _dot_nt(kext[qb * A_BLOCK:(qb + 3) * A_BLOCK], jnp.concatenate(heads, axis=0))

    def softmax(u, st):
        qb, grp = units[u]
        variant = 1
        if qb == 0:
            variant = jnp.where(i == 0, 0, variant)
        if qb == nq - 1:
            variant = jnp.where(i == n - 1, 2, variant)
        sb = _same_bits(st) + bias_ref[variant, grp]
        m = jnp.maximum(jnp.max(sb, axis=0, keepdims=True), sink_ref[grp])
        return m, jnp.exp2(sb - m).astype(BF16)

    def pv(u, probs):
        qb, grp = units[u]
        m, pt = probs
        vt = jnp.concatenate([vts[qb + d][grp] for d in range(3)], axis=1)
        acc = _dot(vt, pt)
        ot = acc[:HEAD_DIM] / (acc[HEAD_DIM:HEAD_DIM + 1] + jnp.exp2(sink_ref[grp] - m))
        rows = slice(qb * A_BLOCK, (qb + 1) * A_BLOCK)
        for pair in range(2):
            yt = jnp.concatenate([ot[:, (2 * pair) * A_BLOCK:(2 * pair + 1) * A_BLOCK],
                                  ot[:, (2 * pair + 1) * A_BLOCK:(2 * pair + 2) * A_BLOCK]], axis=0)
            cols = slice((2 * grp + pair) * LANES, (2 * grp + pair + 1) * LANES)
            gate = g_ref[0, rows, cols].astype(F32)
            o_ref[0, rows, cols] = (yt.T * _silu(gate)).astype(o_ref.dtype)

    _flash_units(len(units), ahead, qk, softmax, pv)


def _win_attn(u3, vta, bias_a, sink_a, tq, ahead):
    bsz, s_len, _ = u3.shape
    nq = tq // A_BLOCK
    nblk = s_len // A_BLOCK
    wide = (1, tq, 4 * LANES)
    const = lambda shape: pl.BlockSpec(shape, lambda bi, i: (0,) * len(shape), pipeline_mode=pl.Buffered(1))
    return pl.pallas_call(
        functools.partial(_win_attn_kernel, nq=nq, ahead=ahead),
        grid=(bsz, s_len // tq),
        in_specs=[
            pl.BlockSpec(wide, lambda bi, i: (bi, i, COL_AQ // 4)),
            pl.BlockSpec((1, A_BLOCK, LANES), lambda bi, i: (bi, jnp.maximum(i * nq - 1, 0), COL_AK)),
            pl.BlockSpec((1, tq, LANES), lambda bi, i: (bi, i, COL_AK)),
            pl.BlockSpec((1, A_BLOCK, LANES), lambda bi, i: (bi, jnp.minimum((i + 1) * nq, nblk - 1), COL_AK)),
            pl.BlockSpec((1, 1, 2, KV_ROWS, A_BLOCK), lambda bi, i: (bi, jnp.maximum(i * nq - 1, 0), 0, 0, 0)),
            pl.BlockSpec((1, nq, 2, KV_ROWS, A_BLOCK), lambda bi, i: (bi, i, 0, 0, 0)),
            pl.BlockSpec((1, 1, 2, KV_ROWS, A_BLOCK),
                         lambda bi, i: (bi, jnp.minimum((i + 1) * nq, nblk - 1), 0, 0, 0)),
            pl.BlockSpec(wide, lambda bi, i: (bi, i, COL_AG // 4)),
            const(bias_a.shape),
            const(sink_a.shape),
        ],
        out_specs=pl.BlockSpec(wide, lambda bi, i: (bi, i, 0)),
        out_shape=jax.ShapeDtypeStruct((bsz, s_len, 4 * LANES), BF16),
        compiler_params=_params(("parallel", "parallel")),
        name="win_attn",
    )(u3, u3, u3, u3, vta, vta, vta, u3, bias_a, sink_a)


def _pool_kernel(xp_ref, xc_ref, xn_ref, g_ref, band_ref, w_ref, sc_ref, o_ref, *, s_len, tq):
    i = pl.program_id(1)
    n = pl.num_programs(1)
    xp = xp_ref[0]
    xn = xn_ref[0]
    xp = jnp.where(i > 0, xp, jnp.zeros_like(xp))
    xn = jnp.where(i < n - 1, xn, jnp.zeros_like(xn))
    xc = xc_ref[0]
    xext = jnp.concatenate([xp, xc, xn], axis=0)
    t = i * tq + lax.broadcasted_iota(jnp.int32, (tq, 1), 0)
    for gi, size in enumerate(B_POOL_SIZES):
        lo = size // 2
        hi = size - lo - 1
        lanes = slice(gi * LANES, (gi + 1) * LANES)
        win_sum = _dot(band_ref[gi], xext[:, lanes])
        start = jnp.maximum(t - lo, 0)
        end = jnp.minimum(t + hi, s_len - 1) + 1
        pooled = win_sum / (end - start).astype(F32) - xc[:, lanes].astype(F32)
        y = _dot(pooled.astype(BF16), w_ref[gi]) * sc_ref[:, lanes]
        gate = g_ref[0, :, lanes].astype(F32)
        o_ref[0, :, lanes] = (y * _silu(gate)).astype(o_ref.dtype)


def _pool_bands(tq):
    r = jnp.arange(tq)[:, None]
    c = jnp.arange(tq + 2 * B_HALO)[None, :] - B_HALO
    bands = []
    for size in B_POOL_SIZES:
        lo = size // 2
        hi = size - lo - 1
        bands.append(((c - r >= -lo) & (c - r <= hi)).astype(BF16))
    return jnp.stack(bands)


def _pool(u3, pool_w, pool_scale, tq):
    bsz, s_len, _ = u3.shape
    nt = s_len // tq
    per = tq // B_HALO
    nh = s_len // B_HALO
    wide = (1, tq, 4 * LANES)
    halo = (1, B_HALO, 4 * LANES)
    return pl.pallas_call(
        functools.partial(_pool_kernel, s_len=s_len, tq=tq),
        grid=(bsz, nt),
        in_specs=[
            pl.BlockSpec(halo, lambda bi, i: (bi, jnp.maximum(i * per - 1, 0), COL_BX // 4)),
            pl.BlockSpec(wide, lambda bi, i: (bi, i, COL_BX // 4)),
            pl.BlockSpec(halo, lambda bi, i: (bi, jnp.minimum((i + 1) * per, nh - 1), COL_BX // 4)),
            pl.BlockSpec(wide, lambda bi, i: (bi, i, COL_BG // 4)),
            pl.BlockSpec((B_GROUPS, tq, tq + 2 * B_HALO), lambda bi, i: (0, 0, 0)),
            pl.BlockSpec((B_GROUPS, LANES, LANES), lambda bi, i: (0, 0, 0)),
            pl.BlockSpec((1, 4 * LANES), lambda bi, i: (0, 0)),
        ],
        out_specs=pl.BlockSpec(wide, lambda bi, i: (bi, i, 0)),
        out_shape=jax.ShapeDtypeStruct((bsz, s_len, 4 * LANES), BF16),
        compiler_params=_params(("parallel", "parallel")),
        name="pool",
    )(u3, u3, u3, u3, _pool_bands(tq), pool_w.astype(BF16), pool_scale.reshape(1, 4 * LANES))


def _softmax_t(st, m):
    m_new = jnp.maximum(m, jnp.max(st, axis=0, keepdims=True))
    return m_new, jnp.exp2(m - m_new), jnp.exp2(st - m_new).astype(BF16)


def _flash_units(n_units, ahead, qk, softmax, pv):
    scores = {u: qk(u) for u in range(min(ahead, n_units))}
    for u in range(n_units):
        probs = softmax(u, scores.pop(u))
        if u + ahead < n_units:
            scores[u + ahead] = qk(u + ahead)
        pv(u, probs)


def _same_bits(x):
    return lax.bitcast_convert_type(lax.bitcast_convert_type(x, jnp.int32), F32)


def _transposed(x):
    return x.astype(F32).T.astype(BF16)


def _masked_halves(q, low):
    zero = jnp.zeros_like(q)
    return jnp.where(low, q, zero), jnp.where(low, zero, q)


def _diff_attn_kernel(q_ref, k_ref, vt_ref, g_ref, bias_ref, stat_ref, lam_ref, sub_ref, o_ref,
                      *, tq, tk, nb, n_iter, ahead, lambda_init, bounded, n_sub):
    for sub in range(n_sub):
        _diff_attn_tile(q_ref, k_ref, vt_ref, g_ref, bias_ref, stat_ref, lam_ref, sub_ref, o_ref, sub,
                        pl.program_id(2) * n_sub + sub, tq=tq, tk=tk, nb=nb, n_iter=n_iter, ahead=ahead,
                        lambda_init=lambda_init, bounded=bounded)


def _diff_attn_tile(q_ref, k_ref, vt_ref, g_ref, bias_ref, stat_ref, lam_ref, sub_ref, o_ref, sub, i,
                    *, tq, tk, nb, n_iter, ahead, lambda_init, bounded):
    rows_q = slice(sub * tq, (sub + 1) * tq)
    ratio = tq // tk
    n_tab = bias_ref.shape[1]
    rows = vt_ref.shape[-2]
    low = _lane_is_low()
    q = q_ref[0, rows_q]
    qs = _masked_halves(q, low)
    bounds = None
    if bounded:
        qf = q.astype(F32)
        halves = jnp.concatenate([jnp.where(low, 1.0, 0.0), jnp.where(low, 0.0, 1.0)] +
                                 [jnp.zeros((1, LANES), F32)] * 6, axis=0).astype(BF16)
        norm2 = _dot_nt(halves, (qf * qf).astype(BF16))
        bounds = [jnp.sqrt(norm2[h:h + 1]) * stat_ref[0, 0, h:h + 1, 0:1] + stat_ref[0, 0, 2:3, 0:1]
                  for h in range(2)]

    def step(j, carry):
        state = list(carry)
        blocks = [nb * j + b for b in range(nb)]
        ks = [k_ref[0, pl.ds(pl.multiple_of(blk * tk, tk), tk), :] for blk in blocks]
        vts = [vt_ref[0, 0, blk] for blk in blocks]
        tabs = [jnp.clip(blk - ratio * i + 2, 0, n_tab - 1) for blk in blocks]

        def qk(u):
            return _dot_nt(ks[u // 2], qs[u % 2])

        def softmax(u, st):
            sb = _same_bits(st) + bias_ref[0, tabs[u // 2]]
            if bounded:
                return None, None, jnp.exp2(sb - bounds[u % 2]).astype(BF16)
            return _softmax_t(sb, state[u % 2][0])

        def pv(u, probs):
            m_new, alpha, pt = probs
            acc = state[u % 2][1]
            state[u % 2] = (m_new, (acc if bounded else alpha * acc) + _dot(vts[u // 2], pt))

        _flash_units(2 * nb, ahead, qk, softmax, pv)
        return tuple(state)

    m0 = None if bounded else jnp.full((1, tq), NEG_INF, F32)
    init = tuple((m0, jnp.zeros((rows, tq), F32)) for _ in range(2))
    (_, a1), (_, a2) = lax.fori_loop(0, n_iter, step, init)

    dot1 = jnp.sum(lam_ref[0:1, :] * lam_ref[1:2, :], axis=-1, keepdims=True)
    dot2 = jnp.sum(lam_ref[2:3, :] * lam_ref[3:4, :], axis=-1, keepdims=True)
    lam = jnp.exp(dot1) - jnp.exp(dot2) + lambda_init
    yt = a1[:LANES] / a1[LANES:LANES + 1] - lam * (a2[:LANES] / a2[LANES:LANES + 1])
    y = yt.T
    ms = jnp.mean(y * y, axis=-1, keepdims=True)
    y = y * lax.rsqrt(ms + EPS) * sub_ref[...] * (1.0 - lambda_init)
    o_ref[0, rows_q] = (y * _silu(g_ref[0, rows_q].astype(F32))).astype(o_ref.dtype)


def _diff_attn(u3, vtc, bias_c, stat_c, lam_vecs, subln, lambda_init, tq, tk, nb, ahead, bounded):
    bsz, s_len, _ = u3.shape
    nk = s_len // tk
    nb = min(nb, nk)
    rows = vtc.shape[-2]
    n_tab = bias_c.shape[1]
    n_sub = 2 if s_len % (2 * tq) == 0 else 1
    return pl.pallas_call(
        functools.partial(_diff_attn_kernel, tq=tq, tk=tk, nb=nb, n_iter=nk // nb, ahead=ahead,
                          lambda_init=lambda_init, bounded=bounded, n_sub=n_sub),
        grid=(bsz, C_HEADS, s_len // (tq * n_sub)),
        in_specs=[
            pl.BlockSpec((1, tq * n_sub, LANES), lambda bi, h, i: (bi, i, COL_CQ + h)),
            pl.BlockSpec((1, s_len, LANES), lambda bi, h, i: (bi, 0, COL_CK + h)),
            pl.BlockSpec((1, 1, nk, rows, tk), lambda bi, h, i: (bi, h, 0, 0, 0)),
            pl.BlockSpec((1, tq * n_sub, LANES), lambda bi, h, i: (bi, i, COL_CG + h)),
            pl.BlockSpec((1, n_tab, tk, tq), lambda bi, h, i: (h, 0, 0, 0)),
            pl.BlockSpec((1, 1, 8, LANES), lambda bi, h, i: (bi, h, 0, 0)),
            pl.BlockSpec((4, HEAD_DIM), lambda bi, h, i: (0, 0)),
            pl.BlockSpec((1, LANES), lambda bi, h, i: (0, 0)),
        ],
        out_specs=pl.BlockSpec((1, tq * n_sub, LANES), lambda bi, h, i: (bi, i, h)),
        out_shape=jax.ShapeDtypeStruct((bsz, s_len, 4 * LANES), BF16),
        compiler_params=_params(("parallel", "parallel", "parallel")),
        name="diff_attn_bounded" if bounded else "diff_attn",
    )(u3, u3, vtc, u3, bias_c, stat_c, lam_vecs, subln.reshape(1, LANES))


def _grid_attn_kernel(q_ref, k_ref, vt_ref, g_ref, bound_ref, o_ref, *, n_sub, **tile_args):
    for sub in range(n_sub):
        _grid_attn_tile(q_ref, k_ref, vt_ref, g_ref, bound_ref, o_ref, sub, **tile_args)


def _grid_attn_tile(q_ref, k_ref, vt_ref, g_ref, bound_ref, o_ref, sub, *, tq, tk, nb, n_iter, ahead, bounded):
    rows_q = slice(sub * tq, (sub + 1) * tq)
    grp = pl.program_id(1)
    lane = lax.broadcasted_iota(jnp.int32, (1, LANES), 1)
    keep = (lane >= grp * HEAD_DIM) & (lane < (grp + 1) * HEAD_DIM)
    rows = vt_ref.shape[-2]
    bound = bound_ref[0:1, 0:1]
    qs = []
    for pair in range(2):
        x = q_ref[0, rows_q, pair * LANES:(pair + 1) * LANES]
        xr = pltpu.roll(x, HEAD_DIM, 1)
        zero = jnp.zeros_like(x)
        qs.append(_transposed(jnp.where(keep, jnp.where(grp == 0, x, xr), zero)))
        qs.append(_transposed(jnp.where(keep, jnp.where(grp == 0, xr, x), zero)))

    def step(j, carry):
        state = list(carry)
        blocks = [nb * j + b for b in range(nb)]
        ks = [k_ref[0, pl.ds(pl.multiple_of(blk * tk, tk), tk), :] for blk in blocks]
        vts = [vt_ref[0, 0, blk] for blk in blocks]

        def qk(u):
            return _dot(ks[u // 4], qs[u % 4])

        def softmax(u, st):
            if bounded:
                return None, None, jnp.exp2(st - bound).astype(BF16)
            return _softmax_t(st, state[u % 4][0])

        def pv(u, probs):
            m_new, alpha, pt = probs
            acc = state[u % 4][1]
            state[u % 4] = (m_new, (acc if bounded else alpha * acc) + _dot(vts[u // 4], pt))

        _flash_units(4 * nb, ahead, qk, softmax, pv)
        return tuple(state)

    m0 = None if bounded else jnp.full((1, tq), NEG_INF, F32)
    init = tuple((m0, jnp.zeros((rows, tq), F32)) for _ in range(4))
    state = lax.fori_loop(0, n_iter, step, init)
    for pair in range(2):
        a0, a1 = state[2 * pair][1], state[2 * pair + 1][1]
        yt = jnp.concatenate([a0[:HEAD_DIM] / a0[HEAD_DIM:HEAD_DIM + 1],
                              a1[:HEAD_DIM] / a1[HEAD_DIM:HEAD_DIM + 1]], axis=0)
        gate = g_ref[0, rows_q, pair * LANES:(pair + 1) * LANES].astype(F32)
        o_ref[0, rows_q, pair * LANES:(pair + 1) * LANES] = (yt.T * _silu(gate)).astype(o_ref.dtype)


def _grid_attn(qd3, kd3, vtd, u3, bound_d, tq, tk, nb, ahead, bounded):
    bsz, s_len, _ = qd3.shape
    nk = s_len // tk
    nb = min(nb, nk)
    rows = vtd.shape[-2]
    n_sub = 2 if nk <= 16 and s_len % (2 * tq) == 0 else 1
    return pl.pallas_call(
        functools.partial(_grid_attn_kernel, tq=tq, tk=tk, nb=nb, n_iter=nk // nb, ahead=ahead,
                          bounded=bounded, n_sub=n_sub),
        grid=(bsz, 2, s_len // (tq * n_sub)),
        in_specs=[
            pl.BlockSpec((1, tq * n_sub, 2 * LANES), lambda bi, h, i: (bi, i, h)),
            pl.BlockSpec((1, s_len, LANES), lambda bi, h, i: (bi, 0, 0)),
            pl.BlockSpec((1, 1, nk, rows, tk), lambda bi, h, i: (bi, h, 0, 0, 0)),
            pl.BlockSpec((1, tq * n_sub, 2 * LANES), lambda bi, h, i: (bi, i, COL_DG // 2 + h)),
            pl.BlockSpec((8, LANES), lambda bi, h, i: (0, 0)),
        ],
        out_specs=pl.BlockSpec((1, tq * n_sub, 2 * LANES), lambda bi, h, i: (bi, i, h)),
        out_shape=jax.ShapeDtypeStruct((bsz, s_len, 4 * LANES), BF16),
        compiler_params=_params(("parallel", "parallel", "parallel")),
        name="grid_attn_bounded" if bounded else "grid_attn",
    )(qd3, kd3, vtd, u3, bound_d)


def _norm_rope(x, gain, cos, sin, seg):
    w = x.shape[-1]
    x2 = x * x
    hi = x2.astype(BF16)
    lo = (x2 - hi.astype(F32)).astype(BF16)
    ss = _dot(hi, seg) + _dot(lo, seg)
    y = x * lax.rsqrt(ss * (1.0 / HEAD_DIM) + EPS) * gain
    quarter = HEAD_DIM // 4
    ahead = pltpu.roll(y, w - quarter, 1)
    behind = pltpu.roll(y, quarter, 1)
    first = (lax.broadcasted_iota(jnp.int32, (1, w), 1) % (2 * quarter)) < quarter
    rot = jnp.where(first, -ahead, behind)
    reps = w // LANES
    cos_w = jnp.concatenate([cos] * reps, axis=1)
    sin_w = jnp.concatenate([sin] * reps, axis=1)
    return y * cos_w + rot * sin_w


def _attn_prep_kernel(q_ref, k_ref, vc_ref, vd_ref, va_ref, qc_ref, kc_ref, cos_ref, sin_ref, gq_ref, gk_ref,
                      seg_ref, qo_ref, ko_ref, vtc_ref, vtd_ref, vta_ref, qn_ref, kn_ref):
    cos = cos_ref[...]
    sin = sin_ref[...]
    seg = seg_ref[...]
    q = _norm_rope(q_ref[...].astype(F32), gq_ref[...], cos, sin, seg)
    qo_ref[...] = (q * (SCALE * LOG2E)).astype(qo_ref.dtype)
    k = _norm_rope(k_ref[...].astype(F32), gk_ref[...], cos, sin, seg[:LANES, :LANES])
    ko_ref[...] = k.astype(ko_ref.dtype)
    tm = q.shape[0]
    ones = jnp.ones((VT_ONES, tm), BF16)
    vct = vc_ref[...].astype(F32).T
    for h in range(C_HEADS):
        vtc_ref[0, h, 0] = jnp.concatenate([vct[h * LANES:(h + 1) * LANES].astype(BF16), ones], axis=0)
    vdt = vd_ref[...].astype(F32).T.astype(BF16)
    vat = va_ref[...].astype(F32).T.astype(BF16)
    for h in range(2):
        vtd_ref[0, h, 0] = jnp.concatenate([vdt[h * HEAD_DIM:(h + 1) * HEAD_DIM], ones], axis=0)
        for kb in range(tm // A_BLOCK):
            cols = slice(kb * A_BLOCK, (kb + 1) * A_BLOCK)
            vta_ref[0, kb, h] = jnp.concatenate([vat[h * HEAD_DIM:(h + 1) * HEAD_DIM, cols], ones[:, cols]], axis=0)
    for src, dst in ((qc_ref, qn_ref), (kc_ref, kn_ref)):
        x = src[...].astype(F32)
        seg_sums = _dot((x * x).astype(BF16), seg)
        dst[0] = jnp.broadcast_to(jnp.max(seg_sums, axis=0, keepdims=True), (8, 4 * LANES))


def _attn_prep(u2, cos, sin, qnorm, knorm, bsz, s_len, tm):
    m = u2.shape[0]
    per_seq = s_len // tm
    per_tile = tm // A_BLOCK
    lane = jnp.arange(4 * LANES)
    seg = (lane[:, None] // HEAD_DIM == lane[None, :] // HEAD_DIM).astype(BF16)
    rows_c = LANES + VT_ONES
    return pl.pallas_call(
        _attn_prep_kernel,
        grid=(m // tm,),
        in_specs=[
            pl.BlockSpec((tm, 4 * LANES), lambda i: (i, COL_DQ // 4)),
            pl.BlockSpec((tm, LANES), lambda i: (i, COL_DK)),
            pl.BlockSpec((tm, 4 * LANES), lambda i: (i, COL_CV // 4)),
            pl.BlockSpec((tm, LANES), lambda i: (i, COL_DV)),
            pl.BlockSpec((tm, LANES), lambda i: (i, COL_AV)),
            pl.BlockSpec((tm, 4 * LANES), lambda i: (i, COL_CQ // 4)),
            pl.BlockSpec((tm, 4 * LANES), lambda i: (i, COL_CK // 4)),
            pl.BlockSpec((tm, LANES), lambda i: (i % per_seq, 0)),
            pl.BlockSpec((tm, LANES), lambda i: (i % per_seq, 0)),
            pl.BlockSpec((1, 4 * LANES), lambda i: (0, 0)),
            pl.BlockSpec((1, LANES), lambda i: (0, 0)),
            pl.BlockSpec((4 * LANES, 4 * LANES), lambda i: (0, 0)),
        ],
        out_specs=[
            pl.BlockSpec((tm, 4 * LANES), lambda i: (i, 0)),
            pl.BlockSpec((tm, LANES), lambda i: (i, 0)),
            pl.BlockSpec((1, C_HEADS, 1, rows_c, tm), lambda i: (i // per_seq, 0, i % per_seq, 0, 0)),
            pl.BlockSpec((1, 2, 1, KV_ROWS, tm), lambda i: (i // per_seq, 0, i % per_seq, 0, 0)),
            pl.BlockSpec((1, per_tile, 2, KV_ROWS, A_BLOCK), lambda i: (i // per_seq, i % per_seq, 0, 0, 0)),
            pl.BlockSpec((1, 8, 4 * LANES), lambda i: (i, 0, 0)),
            pl.BlockSpec((1, 8, 4 * LANES), lambda i: (i, 0, 0)),
        ],
        out_shape=[jax.ShapeDtypeStruct((m, 4 * LANES), BF16),
                   jax.ShapeDtypeStruct((m, LANES), BF16),
                   jax.ShapeDtypeStruct((bsz, C_HEADS, per_seq, rows_c, tm), BF16),
                   jax.ShapeDtypeStruct((bsz, 2, per_seq, KV_ROWS, tm), BF16),
                   jax.ShapeDtypeStruct((bsz, s_len // A_BLOCK, 2, KV_ROWS, A_BLOCK), BF16),
                   jax.ShapeDtypeStruct((m // tm, 8, 4 * LANES), F32),
                   jax.ShapeDtypeStruct((m // tm, 8, 4 * LANES), F32)],
        compiler_params=_params(("parallel",)),
        name="attn_prep",
    )(u2, u2, u2, u2, u2, u2, u2, cos, sin, jnp.tile(qnorm, 8).reshape(1, 4 * LANES),
      jnp.tile(knorm, 2).reshape(1, LANES), seg)


def _out_kernel(x_ref, ya_ref, yb_ref, yc_ref, yd_ref, pe_ref, wo_ref, gp_ref, wpe_ref, wpg_ref, o_ref):
    width = 4 * LANES
    z = _dot(ya_ref[...], wo_ref[0 * width:1 * width, :])
    z += _dot(yb_ref[...], wo_ref[1 * width:2 * width, :])
    z += _dot(yc_ref[...], wo_ref[2 * width:3 * width, :])
    z += _dot(yd_ref[...], wo_ref[3 * width:4 * width, :])
    ms = jnp.mean(z * z, axis=-1, keepdims=True)
    x = x_ref[...] + z * lax.rsqrt(ms + EPS) * gp_ref[...]
    logits = _dot(x.astype(BF16), wpg_ref[...])
    gate = 1.0 / (1.0 + jnp.exp(-logits))
    pe = _dot(pe_ref[0].astype(BF16), wpe_ref[...])
    o_ref[...] = x + pe * gate


def _out_stage(x2, ya, yb, yc, yd, p3, layer_idx, w_o, g_post, w_pe, w_pg, tm):
    m = x2.shape[0]
    row = lambda w: pl.BlockSpec((tm, w), lambda i: (i, 0))
    const = lambda r, c: pl.BlockSpec((r, c), lambda i: (0, 0), pipeline_mode=pl.Buffered(1))
    return pl.pallas_call(
        _out_kernel,
        grid=(m // tm,),
        in_specs=[row(D_MODEL), row(4 * LANES), row(4 * LANES), row(4 * LANES), row(4 * LANES),
                  pl.BlockSpec((1, tm, PLE_DIM), lambda i: (layer_idx, i, 0)), const(MIX_WIDTH, D_MODEL),
                  const(1, D_MODEL),
                  const(PLE_DIM, D_MODEL), const(D_MODEL, D_MODEL)],
        out_specs=row(D_MODEL),
        out_shape=jax.ShapeDtypeStruct((m, D_MODEL), F32),
        compiler_params=_params(("parallel",)),
        name="out_stage",
    )(x2, ya, yb, yc, yd, p3, w_o, g_post.reshape(1, D_MODEL), w_pe, w_pg)


def _prep_w_in(w):
    parts, start = [], 0
    for n in IN_SIZES:
        parts.append(w[:, start:start + n])
        start += n
    aq, ak, av, ag, bx, bg, cq, ck, cv, cg, dq, dk, dv, dg = parts

    log2_scale = SCALE * LOG2E
    cols = [aq * log2_scale, ag, bx, bg, cq * log2_scale, ck, cv, cg, dq, dg, ak, av, dk, dv]
    return jnp.concatenate(cols, axis=1).astype(BF16)


def _rel_bucket(rel):
    nb = REL_BUCKETS // 2
    max_exact = nb // 2
    ret = jnp.where(rel > 0, nb, 0)
    n = jnp.abs(rel)
    nf = jnp.maximum(n, 1).astype(F32)
    large = max_exact + (jnp.log(nf / max_exact) / math.log(REL_MAX_DIST / max_exact)
                         * (nb - max_exact)).astype(jnp.int32)
    large = jnp.minimum(large, nb - 1)
    return ret + jnp.where(n < max_exact, n, large)


def _bucket_values(tab, rel):
    bucket = _rel_bucket(rel)[None]
    out = jnp.zeros((tab.shape[1],) + rel.shape, F32)
    for b in range(REL_BUCKETS):
        out = jnp.where(bucket == b, tab[b].reshape((-1,) + (1,) * rel.ndim), out)
    return out


def _bias_tables(rel_bias, tq, tk):
    tab = rel_bias.astype(F32)
    ck = jnp.arange(3 * A_BLOCK)[:, None]
    rq = jnp.arange(A_BLOCK)[None, :]
    rel_a = ck - A_BLOCK - rq
    vals = _bucket_values(tab[:, :A_HEADS], rel_a) * LOG2E
    in_band = jnp.abs(rel_a) <= A_WINDOW
    variants = []
    for edge in (ck >= A_BLOCK, ck >= 0, ck < 2 * A_BLOCK):
        masked = jnp.where(in_band & edge, vals, NEG_INF)
        variants.append(jnp.transpose(masked.reshape(2, 4, 3 * A_BLOCK, A_BLOCK), (0, 2, 1, 3))
                        .reshape(2, 3 * A_BLOCK, 4 * A_BLOCK))
    bias_a = jnp.stack(variants)
    assert tk >= REL_MAX_DIST
    n_tab = tq // tk + 4
    d = jnp.arange(n_tab)[:, None, None]
    c = jnp.arange(tk)[None, :, None]
    r = jnp.arange(tq)[None, None, :]
    bias_c = _bucket_values(tab[:, A_HEADS:], c + (d - 2) * tk - r) * LOG2E
    return bias_a, bias_c


def _axial_rope(s_len):
    rows = s_len // GRID_W
    t_row = jnp.repeat(jnp.arange(rows), GRID_W).astype(F32)
    t_col = jnp.tile(jnp.arange(GRID_W), rows).astype(F32)
    half = HEAD_DIM // 2
    inv = ROPE_THETA ** (-jnp.arange(0, half, 2, dtype=F32) / half)
    ang_r = t_row[:, None] * inv[None, :]
    ang_c = t_col[:, None] * inv[None, :]
    ang = jnp.concatenate([ang_r, ang_r, ang_c, ang_c] * 2, axis=-1)
    return jnp.cos(ang), jnp.sin(ang)


def _tiles(s_len):
    return dict(tm_in=min(512, s_len), tn_in=U_WIDTH // 2, tq_pool=min(256, s_len), tm_out=min(512, s_len),
                tq_attn=512, tk_attn=256, nb_c=16, nb_d=32, ahead_c=3, ahead_d=3, tq_win=1024, ahead_win=1)


def _layer(x, p, layer_idx, tables, rope, w, tiles):
    bsz, s_len, _ = x.shape
    m = bsz * s_len
    bias_a, bias_c = tables
    cos, sin = rope
    tq, tk = tiles["tq_attn"], tiles["tk_attn"]
    x2 = x.reshape(m, D_MODEL)
    u2 = _in_proj(x2, w["g_pre"], w["w_in"], tiles["tm_in"], tiles["tn_in"])
    u3 = u2.reshape(bsz, s_len, U_WIDTH)
    qd, kd, vtc, vtd, vta, qn, kn = _attn_prep(u2, cos, sin, w["qnorm_d"], w["knorm_d"], bsz, s_len, tk)
    ya = _win_attn(u3, vta, bias_a, w["sink"], tiles["tq_win"], tiles["ahead_win"])
    yb = _pool(u3, w["pool_w"], w["pool_scale"], tiles["tq_pool"])
    lambda_init = 0.8 - 0.6 * math.exp(-0.3 * layer_idx)
    span = math.log2(s_len)

    seg_max = lambda t: jnp.sqrt(jnp.max(t.reshape(bsz, -1, 8, C_HEADS, 2, HEAD_DIM)[:, :, 0, :, :, 0], axis=1))
    q_max, k_max = seg_max(qn), seg_max(kn) * BOUND_SLACK
    b_hi, b_lo = jnp.max(bias_c, axis=(1, 2, 3)), jnp.min(bias_c, axis=(1, 2, 3))
    stat_c = jnp.concatenate([k_max, jnp.broadcast_to(b_hi[None, :, None], (bsz, C_HEADS, 1)),
                              jnp.zeros((bsz, C_HEADS, 5), F32)], axis=2)
    stat_c = jnp.broadcast_to(stat_c[..., None], (bsz, C_HEADS, 8, LANES))
    fits_c = 2.0 * jnp.max(q_max * k_max) + jnp.max(b_hi - b_lo) + span <= MAX_EXP2_SPAN
    attend_c = lambda bounded: _diff_attn(u3, vtc, bias_c, stat_c, w["lam"], w["diff_subln"], lambda_init,
                                          tq, tk, tiles["nb_c"], tiles["ahead_c"], bounded)
    yc = lax.cond(fits_c, lambda: attend_c(True), lambda: attend_c(False))

    bound_d = (HEAD_DIM * SCALE * LOG2E * BOUND_SLACK
               * jnp.max(jnp.abs(w["qnorm_d"])) * jnp.max(jnp.abs(w["knorm_d"])))
    fits_d = 2.0 * bound_d + span <= MAX_EXP2_SPAN
    attend_d = lambda bounded: _grid_attn(qd.reshape(bsz, s_len, -1), kd.reshape(bsz, s_len, -1), vtd, u3,
                                          jnp.full((8, LANES), bound_d, F32), tq, tk, tiles["nb_d"],
                                          tiles["ahead_d"], bounded)
    yd = lax.cond(fits_d, lambda: attend_d(True), lambda: attend_d(False))
    flat = lambda y: y.reshape(m, 4 * LANES)
    out = _out_stage(x2, flat(ya), flat(yb), flat(yc), flat(yd), p.reshape(-1, m, PLE_DIM), layer_idx,
                     w["w_o"], w["g_post"], w["w_pe"], w["w_pg"], tiles["tm_out"])
    return out.reshape(bsz, s_len, D_MODEL)


def _trunk(x, p, rel_bias, layers, tiles=None):
    s_len = x.shape[1]
    tiles = tiles or _tiles(s_len)
    tables = _bias_tables(rel_bias, tiles["tq_attn"], tiles["tk_attn"])
    rope = _axial_rope(s_len)
    for i, w in enumerate(layers):
        x = _layer(x, p, i, tables, rope, w, tiles)
    return x


def _prep_layers(w_in, w_o, g_pre, g_post, sink_a, pool_w, pool_scale, lam_q1, lam_k1, lam_q2, lam_k2,
                 diff_subln, qnorm_d, knorm_d, w_pe, w_pg):
    layers = []
    for i in range(w_in.shape[0]):
        layers.append(dict(
            w_in=_prep_w_in(w_in[i]), w_o=w_o[i].astype(BF16), g_pre=g_pre[i], g_post=g_post[i],
            sink=jnp.repeat(sink_a[i].astype(F32) * LOG2E, A_BLOCK).reshape(2, 1, 4 * A_BLOCK),
            pool_w=pool_w[i], pool_scale=pool_scale[i],
            lam=jnp.stack([lam_q1[i], lam_k1[i], lam_q2[i], lam_k2[i]]).astype(F32),
            diff_subln=diff_subln[i], qnorm_d=qnorm_d[i], knorm_d=knorm_d[i],
            w_pe=w_pe[i].astype(BF16), w_pg=w_pg[i].astype(BF16)))
    return layers


def kernel(x_prompt, x_sample, p_prompt, p_sample, w_in, w_o, g_pre, g_post, sink_a, pool_w, pool_scale,
           lam_q1, lam_k1, lam_q2, lam_k2, diff_subln, qnorm_d, knorm_d, rel_bias, w_pe, w_pg):
    layers = _prep_layers(w_in, w_o, g_pre, g_post, sink_a, pool_w, pool_scale, lam_q1, lam_k1, lam_q2,
                          lam_k2, diff_subln, qnorm_d, knorm_d, w_pe, w_pg)
    y_prompt = _trunk(x_prompt, p_prompt, rel_bias, layers)
    y_sample = _trunk(x_sample, p_sample, rel_bias, layers)
    return (y_prompt, y_sample)
```

```python
import functools
import math

import jax
import jax.numpy as jnp
from jax import lax
from jax.experimental import pallas as pl
from jax.experimental.pallas import tpu as pltpu

F32 = jnp.float32
BF16 = jnp.bfloat16

D_MODEL = 2048
HEAD_DIM = 64
LANES = 128
PLE_DIM = 256
EPS = 1e-6
NEG_INF = -1e30
SCALE = HEAD_DIM ** -0.5
LOG2E = math.log2(math.e)
VT_ONES = 16
KV_ROWS = HEAD_DIM + VT_ONES
BOUND_SLACK = 1.02
MAX_EXP2_SPAN = 100.0

A_HEADS = 8
A_WINDOW = 128
A_BLOCK = 128
B_GROUPS = 4
B_POOL_SIZES = (2, 4, 8, 16)
B_HALO = 64
C_HEADS = 4
REL_BUCKETS = 32
REL_MAX_DIST = 128
ROPE_THETA = 10000.0
GRID_W = 64

IN_SIZES = (512, 128, 128, 512, 512, 512, 512, 512, 512, 512, 512, 128, 128, 512)
MIX_WIDTH = 2048

U_WIDTH = 5632
COL_AQ, COL_AG, COL_BX, COL_BG = 0, 4, 8, 12
COL_CQ, COL_CK, COL_CV, COL_CG = 16, 20, 24, 28
COL_DQ, COL_DG = 32, 36
COL_AK, COL_AV, COL_DK, COL_DV = 40, 41, 42, 43

VMEM_LIMIT = 56 * 1024 * 1024


def _params(sem, vmem=VMEM_LIMIT):
    return pltpu.CompilerParams(dimension_semantics=sem, vmem_limit_bytes=vmem)


def _dot(a, b):
    return jnp.dot(a, b, preferred_element_type=F32)


def _dot_nt(a, b):
    return lax.dot_general(a, b, (((1,), (1,)), ((), ())), preferred_element_type=F32)


def _silu(x):
    return x / (1.0 + jnp.exp(-x))


def _lane_is_low(width=LANES):
    return (lax.broadcasted_iota(jnp.int32, (1, width), 1) % LANES) < HEAD_DIM


def _in_proj_kernel(x_ref, g_ref, w_ref, o_ref, h_ref):
    @pl.when(pl.program_id(1) == 0)
    def _():
        x = x_ref[...]
        ms = jnp.mean(x * x, axis=-1, keepdims=True)
        h_ref[...] = (x * lax.rsqrt(ms + EPS) * g_ref[...]).astype(BF16)

    o_ref[...] = _dot(h_ref[...], w_ref[...]).astype(o_ref.dtype)


def _in_proj(x2, g_pre, w_in_p, tm, tn):
    m = x2.shape[0]
    return pl.pallas_call(
        _in_proj_kernel,
        grid=(m // tm, U_WIDTH // tn),
        in_specs=[
            pl.BlockSpec((tm, D_MODEL), lambda i, j: (i, 0)),
            pl.BlockSpec((1, D_MODEL), lambda i, j: (0, 0)),
            pl.BlockSpec((D_MODEL, tn), lambda i, j: (0, j)),
        ],
        out_specs=pl.BlockSpec((tm, tn), lambda i, j: (i, j)),
        out_shape=jax.ShapeDtypeStruct((m, U_WIDTH), BF16),
        scratch_shapes=[pltpu.VMEM((tm, D_MODEL), BF16)],
        compiler_params=_params(("parallel", "arbitrary")),
        name="in_proj",
    )(x2, g_pre.reshape(1, D_MODEL), w_in_p)


def _group_queries(x, group, low):
    xr = pltpu.roll(x, HEAD_DIM, 1)
    zero = jnp.zeros_like(x)
    if group == 0:
        return [jnp.where(low, x, zero), jnp.where(low, xr, zero)]
    return [jnp.where(low, zero, xr), jnp.where(low, zero, x)]


def _win_attn_kernel(q_ref, kp_ref, kc_ref, kn_ref, vp_ref, vc_ref, vn_ref, g_ref, bias_ref,
                     sink_ref, o_ref, *, nq, ahead):
    i = pl.program_id(1)
    n = pl.num_programs(1)
    low = _lane_is_low()
    kext = jnp.concatenate([kp_ref[0], kc_ref[0], kn_ref[0]], axis=0)
    vts = [vp_ref[0, 0]] + [vc_ref[0, b] for b in range(nq)] + [vn_ref[0, 0]]
    units = [(qb, grp) for qb in range(nq) for grp in range(2)]

    def qk(u):
        qb, grp = units[u]
        rows = slice(qb * A_BLOCK, (qb + 1) * A_BLOCK)
        heads = []
        for pair in (2 * grp, 2 * grp + 1):
            heads += _group_queries(q_ref[0, rows, pair * LANES:(pair + 1) * LANES], grp, low)
        return _dot_nt(kext[qb * A_BLOCK:(qb + 3) * A_BLOCK], jnp.concatenate(heads, axis=0))

    def softmax(u, st):
        qb, grp = units[u]
        variant = 1
        if qb == 0:
            variant = jnp.where(i == 0, 0, variant)
        if qb == nq - 1:
            variant = jnp.where(i == n - 1, 2, variant)
        sb = _same_bits(st) + bias_ref[variant, grp]
        m = jnp.maximum(jnp.max(sb, axis=0, keepdims=True), sink_ref[grp])
        return m, jnp.exp2(sb - m).astype(BF16)

    def pv(u, probs):
        qb, grp = units[u]
        m, pt = probs
        vt = jnp.concatenate([vts[qb + d][grp] for d in range(3)], axis=1)
        acc = _dot(vt, pt)
        ot = acc[:HEAD_DIM] / (acc[HEAD_DIM:HEAD_DIM + 1] + jnp.exp2(sink_ref[grp] - m))
        rows = slice(qb * A_BLOCK, (qb + 1) * A_BLOCK)
        for pair in range(2):
            yt = jnp.concatenate([ot[:, (2 * pair) * A_BLOCK:(2 * pair + 1) * A_BLOCK],
                                  ot[:, (2 * pair + 1) * A_BLOCK:(2 * pair + 2) * A_BLOCK]], axis=0)
            cols = slice((2 * grp + pair) * LANES, (2 * grp + pair + 1) * LANES)
            gate = g_ref[0, rows, cols].astype(F32)
            o_ref[0, rows, cols] = (yt.T * _silu(gate)).astype(o_ref.dtype)

    _flash_units(len(units), ahead, qk, softmax, pv)


def _win_attn(u3, vta, bias_a, sink_a, tq, ahead):
    bsz, s_len, _ = u3.shape
    nq = tq // A_BLOCK
    nblk = s_len // A_BLOCK
    wide = (1, tq, 4 * LANES)
    const = lambda shape: pl.BlockSpec(shape, lambda bi, i: (0,) * len(shape), pipeline_mode=pl.Buffered(1))
    return pl.pallas_call(
        functools.partial(_win_attn_kernel, nq=nq, ahead=ahead),
        grid=(bsz, s_len // tq),
        in_specs=[
            pl.BlockSpec(wide, lambda bi, i: (bi, i, COL_AQ // 4)),
            pl.BlockSpec((1, A_BLOCK, LANES), lambda bi, i: (bi, jnp.maximum(i * nq - 1, 0), COL_AK)),
            pl.BlockSpec((1, tq, LANES), lambda bi, i: (bi, i, COL_AK)),
            pl.BlockSpec((1, A_BLOCK, LANES), lambda bi, i: (bi, jnp.minimum((i + 1) * nq, nblk - 1), COL_AK)),
            pl.BlockSpec((1, 1, 2, KV_ROWS, A_BLOCK), lambda bi, i: (bi, jnp.maximum(i * nq - 1, 0), 0, 0, 0)),
            pl.BlockSpec((1, nq, 2, KV_ROWS, A_BLOCK), lambda bi, i: (bi, i, 0, 0, 0)),
            pl.BlockSpec((1, 1, 2, KV_ROWS, A_BLOCK),
                         lambda bi, i: (bi, jnp.minimum((i + 1) * nq, nblk - 1), 0, 0, 0)),
            pl.BlockSpec(wide, lambda bi, i: (bi, i, COL_AG // 4)),
            const(bias_a.shape),
            const(sink_a.shape),
        ],
        out_specs=pl.BlockSpec(wide, lambda bi, i: (bi, i, 0)),
        out_shape=jax.ShapeDtypeStruct((bsz, s_len, 4 * LANES), BF16),
        compiler_params=_params(("parallel", "parallel")),
        name="win_attn",
    )(u3, u3, u3, u3, vta, vta, vta, u3, bias_a, sink_a)


def _pool_kernel(xp_ref, xc_ref, xn_ref, g_ref, band_ref, w_ref, sc_ref, o_ref, *, s_len, tq):
    i = pl.program_id(1)
    n = pl.num_programs(1)
    xp = xp_ref[0]
    xn = xn_ref[0]
    xp = jnp.where(i > 0, xp, jnp.zeros_like(xp))
    xn = jnp.where(i < n - 1, xn, jnp.zeros_like(xn))
    xc = xc_ref[0]
    xext = jnp.concatenate([xp, xc, xn], axis=0)
    t = i * tq + lax.broadcasted_iota(jnp.int32, (tq, 1), 0)
    for gi, size in enumerate(B_POOL_SIZES):
        lo = size // 2
        hi = size - lo - 1
        lanes = slice(gi * LANES, (gi + 1) * LANES)
        win_sum = _dot(band_ref[gi], xext[:, lanes])
        start = jnp.maximum(t - lo, 0)
        end = jnp.minimum(t + hi, s_len - 1) + 1
        pooled = win_sum / (end - start).astype(F32) - xc[:, lanes].astype(F32)
        y = _dot(pooled.astype(BF16), w_ref[gi]) * sc_ref[:, lanes]
        gate = g_ref[0, :, lanes].astype(F32)
        o_ref[0, :, lanes] = (y * _silu(gate)).astype(o_ref.dtype)


def _pool_bands(tq):
    r = jnp.arange(tq)[:, None]
    c = jnp.arange(tq + 2 * B_HALO)[None, :] - B_HALO
    bands = []
    for size in B_POOL_SIZES:
        lo = size // 2
        hi = size - lo - 1
        bands.append(((c - r >= -lo) & (c - r <= hi)).astype(BF16))
    return jnp.stack(bands)


def _pool(u3, pool_w, pool_scale, tq):
    bsz, s_len, _ = u3.shape
    nt = s_len // tq
    per = tq // B_HALO
    nh = s_len // B_HALO
    wide = (1, tq, 4 * LANES)
    halo = (1, B_HALO, 4 * LANES)
    return pl.pallas_call(
        functools.partial(_pool_kernel, s_len=s_len, tq=tq),
        grid=(bsz, nt),
        in_specs=[
            pl.BlockSpec(halo, lambda bi, i: (bi, jnp.maximum(i * per - 1, 0), COL_BX // 4)),
            pl.BlockSpec(wide, lambda bi, i: (bi, i, COL_BX // 4)),
            pl.BlockSpec(halo, lambda bi, i: (bi, jnp.minimum((i + 1) * per, nh - 1), COL_BX // 4)),
            pl.BlockSpec(wide, lambda bi, i: (bi, i, COL_BG // 4)),
            pl.BlockSpec((B_GROUPS, tq, tq + 2 * B_HALO), lambda bi, i: (0, 0, 0)),
            pl.BlockSpec((B_GROUPS, LANES, LANES), lambda bi, i: (0, 0, 0)),
            pl.BlockSpec((1, 4 * LANES), lambda bi, i: (0, 0)),
        ],
        out_specs=pl.BlockSpec(wide, lambda bi, i: (bi, i, 0)),
        out_shape=jax.ShapeDtypeStruct((bsz, s_len, 4 * LANES), BF16),
        compiler_params=_params(("parallel", "parallel")),
        name="pool",
    )(u3, u3, u3, u3, _pool_bands(tq), pool_w.astype(BF16), pool_scale.reshape(1, 4 * LANES))


def _softmax_t(st, m):
    m_new = jnp.maximum(m, jnp.max(st, axis=0, keepdims=True))
    return m_new, jnp.exp2(m - m_new), jnp.exp2(st - m_new).astype(BF16)


def _flash_units(n_units, ahead, qk, softmax, pv):
    scores = {u: qk(u) for u in range(min(ahead, n_units))}
    for u in range(n_units):
        probs = softmax(u, scores.pop(u))
        if u + ahead < n_units:
            scores[u + ahead] = qk(u + ahead)
        pv(u, probs)


def _same_bits(x):
    return lax.bitcast_convert_type(lax.bitcast_convert_type(x, jnp.int32), F32)


def _transposed(x):
    return x.astype(F32).T.astype(BF16)


def _masked_halves(q, low):
    zero = jnp.zeros_like(q)
    return jnp.where(low, q, zero), jnp.where(low, zero, q)


def _diff_attn_kernel(q_ref, k_ref, vt_ref, g_ref, bias_ref, stat_ref, lam_ref, sub_ref, o_ref,
                      *, tq, tk, nb, n_iter, ahead, lambda_init, bounded, n_sub):
    for sub in range(n_sub):
        _diff_attn_tile(q_ref, k_ref, vt_ref, g_ref, bias_ref, stat_ref, lam_ref, sub_ref, o_ref, sub,
                        pl.program_id(2) * n_sub + sub, tq=tq, tk=tk, nb=nb, n_iter=n_iter, ahead=ahead,
                        lambda_init=lambda_init, bounded=bounded)


def _diff_attn_tile(q_ref, k_ref, vt_ref, g_ref, bias_ref, stat_ref, lam_ref, sub_ref, o_ref, sub, i,
                    *, tq, tk, nb, n_iter, ahead, lambda_init, bounded):
    rows_q = slice(sub * tq, (sub + 1) * tq)
    ratio = tq // tk
    n_tab = bias_ref.shape[1]
    rows = vt_ref.shape[-2]
    low = _lane_is_low()
    q = q_ref[0, rows_q]
    qs = _masked_halves(q, low)
    bounds = None
    if bounded:
        qf = q.astype(F32)
        halves = jnp.concatenate([jnp.where(low, 1.0, 0.0), jnp.where(low, 0.0, 1.0)] +
                                 [jnp.zeros((1, LANES), F32)] * 6, axis=0).astype(BF16)
        norm2 = _dot_nt(halves, (qf * qf).astype(BF16))
        bounds = [jnp.sqrt(norm2[h:h + 1]) * stat_ref[0, 0, h:h + 1, 0:1] + stat_ref[0, 0, 2:3, 0:1]
                  for h in range(2)]

    def step(j, carry):
        state = list(carry)
        blocks = [nb * j + b for b in range(nb)]
        ks = [k_ref[0, pl.ds(pl.multiple_of(blk * tk, tk), tk), :] for blk in blocks]
        vts = [vt_ref[0, 0, blk] for blk in blocks]
        tabs = [jnp.clip(blk - ratio * i + 2, 0, n_tab - 1) for blk in blocks]

        def qk(u):
            return _dot_nt(ks[u // 2], qs[u % 2])

        def softmax(u, st):
            sb = _same_bits(st) + bias_ref[0, tabs[u // 2]]
            if bounded:
                return None, None, jnp.exp2(sb - bounds[u % 2]).astype(BF16)
            return _softmax_t(sb, state[u % 2][0])

        def pv(u, probs):
            m_new, alpha, pt = probs
            acc = state[u % 2][1]
            state[u % 2] = (m_new, (acc if bounded else alpha * acc) + _dot(vts[u // 2], pt))

        _flash_units(2 * nb, ahead, qk, softmax, pv)
        return tuple(state)

    m0 = None if bounded else jnp.full((1, tq), NEG_INF, F32)
    init = tuple((m0, jnp.zeros((rows, tq), F32)) for _ in range(2))
    (_, a1), (_, a2) = lax.fori_loop(0, n_iter, step, init)

    dot1 = jnp.sum(lam_ref[0:1, :] * lam_ref[1:2, :], axis=-1, keepdims=True)
    dot2 = jnp.sum(lam_ref[2:3, :] * lam_ref[3:4, :], axis=-1, keepdims=True)
    lam = jnp.exp(dot1) - jnp.exp(dot2) + lambda_init
    yt = a1[:LANES] / a1[LANES:LANES + 1] - lam * (a2[:LANES] / a2[LANES:LANES + 1])
    y = yt.T
    ms = jnp.mean(y * y, axis=-1, keepdims=True)
    y = y * lax.rsqrt(ms + EPS) * sub_ref[...] * (1.0 - lambda_init)
    o_ref[0, rows_q] = (y * _silu(g_ref[0, rows_q].astype(F32))).astype(o_ref.dtype)


def _diff_attn(u3, vtc, bias_c, stat_c, lam_vecs, subln, lambda_init, tq, tk, nb, ahead, bounded):
    bsz, s_len, _ = u3.shape
    nk = s_len // tk
    nb = min(nb, nk)
    rows = vtc.shape[-2]
    n_tab = bias_c.shape[1]
    n_sub = 2 if s_len % (2 * tq) == 0 else 1
    return pl.pallas_call(
        functools.partial(_diff_attn_kernel, tq=tq, tk=tk, nb=nb, n_iter=nk // nb, ahead=ahead,
                          lambda_init=lambda_init, bounded=bounded, n_sub=n_sub),
        grid=(bsz, C_HEADS, s_len // (tq * n_sub)),
        in_specs=[
            pl.BlockSpec((1, tq * n_sub, LANES), lambda bi, h, i: (bi, i, COL_CQ + h)),
            pl.BlockSpec((1, s_len, LANES), lambda bi, h, i: (bi, 0, COL_CK + h)),
            pl.BlockSpec((1, 1, nk, rows, tk), lambda bi, h, i: (bi, h, 0, 0, 0)),
            pl.BlockSpec((1, tq * n_sub, LANES), lambda bi, h, i: (bi, i, COL_CG + h)),
            pl.BlockSpec((1, n_tab, tk, tq), lambda bi, h, i: (h, 0, 0, 0)),
            pl.BlockSpec((1, 1, 8, LANES), lambda bi, h, i: (bi, h, 0, 0)),
            pl.BlockSpec((4, HEAD_DIM), lambda bi, h, i: (0, 0)),
            pl.BlockSpec((1, LANES), lambda bi, h, i: (0, 0)),
        ],
        out_specs=pl.BlockSpec((1, tq * n_sub, LANES), lambda bi, h, i: (bi, i, h)),
        out_shape=jax.ShapeDtypeStruct((bsz, s_len, 4 * LANES), BF16),
        compiler_params=_params(("parallel", "parallel", "parallel")),
        name="diff_attn_bounded" if bounded else "diff_attn",
    )(u3, u3, vtc, u3, bias_c, stat_c, lam_vecs, subln.reshape(1, LANES))


def _grid_attn_kernel(q_ref, k_ref, vt_ref, g_ref, bound_ref, o_ref, *, n_sub, **tile_args):
    for sub in range(n_sub):
        _grid_attn_tile(q_ref, k_ref, vt_ref, g_ref, bound_ref, o_ref, sub, **tile_args)


def _grid_attn_tile(q_ref, k_ref, vt_ref, g_ref, bound_ref, o_ref, sub, *, tq, tk, nb, n_iter, ahead, bounded):
    rows_q = slice(sub * tq, (sub + 1) * tq)
    grp = pl.program_id(1)
    lane = lax.broadcasted_iota(jnp.int32, (1, LANES), 1)
    keep = (lane >= grp * HEAD_DIM) & (lane < (grp + 1) * HEAD_DIM)
    rows = vt_ref.shape[-2]
    bound = bound_ref[0:1, 0:1]
    qs = []
    for pair in range(2):
        x = q_ref[0, rows_q, pair * LANES:(pair + 1) * LANES]
        xr = pltpu.roll(x, HEAD_DIM, 1)
        zero = jnp.zeros_like(x)
        qs.append(_transposed(jnp.where(keep, jnp.where(grp == 0, x, xr), zero)))
        qs.append(_transposed(jnp.where(keep, jnp.where(grp == 0, xr, x), zero)))

    def step(j, carry):
        state = list(carry)
        blocks = [nb * j + b for b in range(nb)]
        ks = [k_ref[0, pl.ds(pl.multiple_of(blk * tk, tk), tk), :] for blk in blocks]
        vts = [vt_ref[0, 0, blk] for blk in blocks]

        def qk(u):
            return _dot(ks[u // 4], qs[u % 4])

        def softmax(u, st):
            if bounded:
                return None, None, jnp.exp2(st - bound).astype(BF16)
            return _softmax_t(st, state[u % 4][0])

        def pv(u, probs):
            m_new, alpha, pt = probs
            acc = state[u % 4][1]
            state[u % 4] = (m_new, (acc if bounded else alpha * acc) + _dot(vts[u // 4], pt))

        _flash_units(4 * nb, ahead, qk, softmax, pv)
        return tuple(state)

    m0 = None if bounded else jnp.full((1, tq), NEG_INF, F32)
    init = tuple((m0, jnp.zeros((rows, tq), F32)) for _ in range(4))
    state = lax.fori_loop(0, n_iter, step, init)
    for pair in range(2):
        a0, a1 = state[2 * pair][1], state[2 * pair + 1][1]
        yt = jnp.concatenate([a0[:HEAD_DIM] / a0[HEAD_DIM:HEAD_DIM + 1],
                              a1[:HEAD_DIM] / a1[HEAD_DIM:HEAD_DIM + 1]], axis=0)
        gate = g_ref[0, rows_q, pair * LANES:(pair + 1) * LANES].astype(F32)
        o_ref[0, rows_q, pair * LANES:(pair + 1) * LANES] = (yt.T * _silu(gate)).astype(o_ref.dtype)


def _grid_attn(qd3, kd3, vtd, u3, bound_d, tq, tk, nb, ahead, bounded):
    bsz, s_len, _ = qd3.shape
    nk = s_len // tk
    nb = min(nb, nk)
    rows = vtd.shape[-2]
    n_sub = 2 if nk <= 16 and s_len % (2 * tq) == 0 else 1
    return pl.pallas_call(
        functools.partial(_grid_attn_kernel, tq=tq, tk=tk, nb=nb, n_iter=nk // nb, ahead=ahead,
                          bounded=bounded, n_sub=n_sub),
        grid=(bsz, 2, s_len // (tq * n_sub)),
        in_specs=[
            pl.BlockSpec((1, tq * n_sub, 2 * LANES), lambda bi, h, i: (bi, i, h)),
            pl.BlockSpec((1, s_len, LANES), lambda bi, h, i: (bi, 0, 0)),
            pl.BlockSpec((1, 1, nk, rows, tk), lambda bi, h, i: (bi, h, 0, 0, 0)),
            pl.BlockSpec((1, tq * n_sub, 2 * LANES), lambda bi, h, i: (bi, i, COL_DG // 2 + h)),
            pl.BlockSpec((8, LANES), lambda bi, h, i: (0, 0)),
        ],
        out_specs=pl.BlockSpec((1, tq * n_sub, 2 * LANES), lambda bi, h, i: (bi, i, h)),
        out_shape=jax.ShapeDtypeStruct((bsz, s_len, 4 * LANES), BF16),
        compiler_params=_params(("parallel", "parallel", "parallel")),
        name="grid_attn_bounded" if bounded else "grid_attn",
    )(qd3, kd3, vtd, u3, bound_d)


def _norm_rope(x, gain, cos, sin, seg):
    w = x.shape[-1]
    x2 = x * x
    hi = x2.astype(BF16)
    lo = (x2 - hi.astype(F32)).astype(BF16)
    ss = _dot(hi, seg) + _dot(lo, seg)
    y = x * lax.rsqrt(ss * (1.0 / HEAD_DIM) + EPS) * gain
    quarter = HEAD_DIM // 4
    ahead = pltpu.roll(y, w - quarter, 1)
    behind = pltpu.roll(y, quarter, 1)
    first = (lax.broadcasted_iota(jnp.int32, (1, w), 1) % (2 * quarter)) < quarter
    rot = jnp.where(first, -ahead, behind)
    reps = w // LANES
    cos_w = jnp.concatenate([cos] * reps, axis=1)
    sin_w = jnp.concatenate([sin] * reps, axis=1)
    return y * cos_w + rot * sin_w


def _attn_prep_kernel(q_ref, k_ref, vc_ref, vd_ref, va_ref, qc_ref, kc_ref, cos_ref, sin_ref, gq_ref, gk_ref,
                      seg_ref, qo_ref, ko_ref, vtc_ref, vtd_ref, vta_ref, qn_ref, kn_ref):
    cos = cos_ref[...]
    sin = sin_ref[...]
    seg = seg_ref[...]
    q = _norm_rope(q_ref[...].astype(F32), gq_ref[...], cos, sin, seg)
    qo_ref[...] = (q * (SCALE * LOG2E)).astype(qo_ref.dtype)
    k = _norm_rope(k_ref[...].astype(F32), gk_ref[...], cos, sin, seg[:LANES, :LANES])
    ko_ref[...] = k.astype(ko_ref.dtype)
    tm = q.shape[0]
    ones = jnp.ones((VT_ONES, tm), BF16)
    vct = vc_ref[...].astype(F32).T
    tk = vtc_ref.shape[-1]
    blocks = [slice(b * tk, (b + 1) * tk) for b in range(tm // tk)]
    for h in range(C_HEADS):
        for b, cols in enumerate(blocks):
            vtc_ref[0, h, b] = jnp.concatenate([vct[h * LANES:(h + 1) * LANES, cols].astype(BF16), ones[:, cols]], axis=0)
    vdt = vd_ref[...].astype(F32).T.astype(BF16)
    vat = va_ref[...].astype(F32).T.astype(BF16)
    for h in range(2):
        for b, cols in enumerate(blocks):
            vtd_ref[0, h, b] = jnp.concatenate([vdt[h * HEAD_DIM:(h + 1) * HEAD_DIM, cols], ones[:, cols]], axis=0)
        for kb in range(tm // A_BLOCK):
            cols = slice(kb * A_BLOCK, (kb + 1) * A_BLOCK)
            vta_ref[0, kb, h] = jnp.concatenate([vat[h * HEAD_DIM:(h + 1) * HEAD_DIM, cols], ones[:, cols]], axis=0)
    for src, dst in ((qc_ref, qn_ref), (kc_ref, kn_ref)):
        x = src[...].astype(F32)
        seg_sums = _dot((x * x).astype(BF16), seg)
        dst[0] = jnp.broadcast_to(jnp.max(seg_sums, axis=0, keepdims=True), (8, 4 * LANES))


def _attn_prep(u2, cos, sin, qnorm, knorm, bsz, s_len, tm, tk):
    m = u2.shape[0]
    per_seq = s_len // tm
    per_tile = tm // A_BLOCK
    per_kv = tm // tk
    lane = jnp.arange(4 * LANES)
    seg = (lane[:, None] // HEAD_DIM == lane[None, :] // HEAD_DIM).astype(BF16)
    rows_c = LANES + VT_ONES
    return pl.pallas_call(
        _attn_prep_kernel,
        grid=(m // tm,),
        in_specs=[
            pl.BlockSpec((tm, 4 * LANES), lambda i: (i, COL_DQ // 4)),
            pl.BlockSpec((tm, LANES), lambda i: (i, COL_DK)),
            pl.BlockSpec((tm, 4 * LANES), lambda i: (i, COL_CV // 4)),
            pl.BlockSpec((tm, LANES), lambda i: (i, COL_DV)),
            pl.BlockSpec((tm, LANES), lambda i: (i, COL_AV)),
            pl.BlockSpec((tm, 4 * LANES), lambda i: (i, COL_CQ // 4)),
            pl.BlockSpec((tm, 4 * LANES), lambda i: (i, COL_CK // 4)),
            pl.BlockSpec((tm, LANES), lambda i: (i % per_seq, 0)),
            pl.BlockSpec((tm, LANES), lambda i: (i % per_seq, 0)),
            pl.BlockSpec((1, 4 * LANES), lambda i: (0, 0)),
            pl.BlockSpec((1, LANES), lambda i: (0, 0)),
            pl.BlockSpec((4 * LANES, 4 * LANES), lambda i: (0, 0)),
        ],
        out_specs=[
            pl.BlockSpec((tm, 4 * LANES), lambda i: (i, 0)),
            pl.BlockSpec((tm, LANES), lambda i: (i, 0)),
            pl.BlockSpec((1, C_HEADS, per_kv, rows_c, tk), lambda i: (i // per_seq, 0, i % per_seq, 0, 0)),
            pl.BlockSpec((1, 2, per_kv, KV_ROWS, tk), lambda i: (i // per_seq, 0, i % per_seq, 0, 0)),
            pl.BlockSpec((1, per_tile, 2, KV_ROWS, A_BLOCK), lambda i: (i // per_seq, i % per_seq, 0, 0, 0)),
            pl.BlockSpec((1, 8, 4 * LANES), lambda i: (i, 0, 0)),
            pl.BlockSpec((1, 8, 4 * LANES), lambda i: (i, 0, 0)),
        ],
        out_shape=[jax.ShapeDtypeStruct((m, 4 * LANES), BF16),
                   jax.ShapeDtypeStruct((m, LANES), BF16),
                   jax.ShapeDtypeStruct((bsz, C_HEADS, s_len // tk, rows_c, tk), BF16),
                   jax.ShapeDtypeStruct((bsz, 2, s_len // tk, KV_ROWS, tk), BF16),
                   jax.ShapeDtypeStruct((bsz, s_len // A_BLOCK, 2, KV_ROWS, A_BLOCK), BF16),
                   jax.ShapeDtypeStruct((m // tm, 8, 4 * LANES), F32),
                   jax.ShapeDtypeStruct((m // tm, 8, 4 * LANES), F32)],
        compiler_params=_params(("parallel",)),
        name="attn_prep",
    )(u2, u2, u2, u2, u2, u2, u2, cos, sin, jnp.tile(qnorm, 8).reshape(1, 4 * LANES),
      jnp.tile(knorm, 2).reshape(1, LANES), seg)


def _out_kernel(x_ref, ya_ref, yb_ref, yc_ref, yd_ref, pe_ref, wo_ref, gp_ref, wpe_ref, wpg_ref, o_ref):
    width = 4 * LANES
    z = _dot(ya_ref[...], wo_ref[0 * width:1 * width, :])
    z += _dot(yb_ref[...], wo_ref[1 * width:2 * width, :])
    z += _dot(yc_ref[...], wo_ref[2 * width:3 * width, :])
    z += _dot(yd_ref[...], wo_ref[3 * width:4 * width, :])
    ms = jnp.mean(z * z, axis=-1, keepdims=True)
    x = x_ref[...] + z * lax.rsqrt(ms + EPS) * gp_ref[...]
    logits = _dot(x.astype(BF16), wpg_ref[...])
    gate = 1.0 / (1.0 + jnp.exp(-logits))
    pe = _dot(pe_ref[0].astype(BF16), wpe_ref[...])
    o_ref[...] = x + pe * gate


def _out_stage(x2, ya, yb, yc, yd, p3, layer_idx, w_o, g_post, w_pe, w_pg, tm):
    m = x2.shape[0]
    row = lambda w: pl.BlockSpec((tm, w), lambda i: (i, 0))
    const = lambda r, c: pl.BlockSpec((r, c), lambda i: (0, 0), pipeline_mode=pl.Buffered(1))
    return pl.pallas_call(
        _out_kernel,
        grid=(m // tm,),
        in_specs=[row(D_MODEL), row(4 * LANES), row(4 * LANES), row(4 * LANES), row(4 * LANES),
                  pl.BlockSpec((1, tm, PLE_DIM), lambda i: (layer_idx, i, 0)), const(MIX_WIDTH, D_MODEL),
                  const(1, D_MODEL),
                  const(PLE_DIM, D_MODEL), const(D_MODEL, D_MODEL)],
        out_specs=row(D_MODEL),
        out_shape=jax.ShapeDtypeStruct((m, D_MODEL), F32),
        compiler_params=_params(("parallel",)),
        name="out_stage",
    )(x2, ya, yb, yc, yd, p3, w_o, g_post.reshape(1, D_MODEL), w_pe, w_pg)


def _prep_w_in(w):
    parts, start = [], 0
    for n in IN_SIZES:
        parts.append(w[:, start:start + n])
        start += n
    aq, ak, av, ag, bx, bg, cq, ck, cv, cg, dq, dk, dv, dg = parts

    log2_scale = SCALE * LOG2E
    cols = [aq * log2_scale, ag, bx, bg, cq * log2_scale, ck, cv, cg, dq, dg, ak, av, dk, dv]
    return jnp.concatenate(cols, axis=1).astype(BF16)


def _rel_bucket(rel):
    nb = REL_BUCKETS // 2
    max_exact = nb // 2
    ret = jnp.where(rel > 0, nb, 0)
    n = jnp.abs(rel)
    nf = jnp.maximum(n, 1).astype(F32)
    large = max_exact + (jnp.log(nf / max_exact) / math.log(REL_MAX_DIST / max_exact)
                         * (nb - max_exact)).astype(jnp.int32)
    large = jnp.minimum(large, nb - 1)
    return ret + jnp.where(n < max_exact, n, large)


def _bucket_values(tab, rel):
    bucket = _rel_bucket(rel)[None]
    out = jnp.zeros((tab.shape[1],) + rel.shape, F32)
    for b in range(REL_BUCKETS):
        out = jnp.where(bucket == b, tab[b].reshape((-1,) + (1,) * rel.ndim), out)
    return out


def _bias_tables(rel_bias, tq, tk):
    tab = rel_bias.astype(F32)
    ck = jnp.arange(3 * A_BLOCK)[:, None]
    rq = jnp.arange(A_BLOCK)[None, :]
    rel_a = ck - A_BLOCK - rq
    vals = _bucket_values(tab[:, :A_HEADS], rel_a) * LOG2E
    in_band = jnp.abs(rel_a) <= A_WINDOW
    variants = []
    for edge in (ck >= A_BLOCK, ck >= 0, ck < 2 * A_BLOCK):
        masked = jnp.where(in_band & edge, vals, NEG_INF)
        variants.append(jnp.transpose(masked.reshape(2, 4, 3 * A_BLOCK, A_BLOCK), (0, 2, 1, 3))
                        .reshape(2, 3 * A_BLOCK, 4 * A_BLOCK))
    bias_a = jnp.stack(variants)
    assert tk >= REL_MAX_DIST
    n_tab = tq // tk + 4
    d = jnp.arange(n_tab)[:, None, None]
    c = jnp.arange(tk)[None, :, None]
    r = jnp.arange(tq)[None, None, :]
    bias_c = _bucket_values(tab[:, A_HEADS:], c + (d - 2) * tk - r) * LOG2E
    return bias_a, bias_c


def _axial_rope(s_len):
    rows = s_len // GRID_W
    t_row = jnp.repeat(jnp.arange(rows), GRID_W).astype(F32)
    t_col = jnp.tile(jnp.arange(GRID_W), rows).astype(F32)
    half = HEAD_DIM // 2
    inv = ROPE_THETA ** (-jnp.arange(0, half, 2, dtype=F32) / half)
    ang_r = t_row[:, None] * inv[None, :]
    ang_c = t_col[:, None] * inv[None, :]
    ang = jnp.concatenate([ang_r, ang_r, ang_c, ang_c] * 2, axis=-1)
    return jnp.cos(ang), jnp.sin(ang)


def _tiles(s_len):
    return dict(tm_in=min(512, s_len), tn_in=U_WIDTH // 2, tq_pool=min(256, s_len), tm_out=min(512, s_len),
                tq_attn=512, tk_attn=256, nb_c=16, nb_d=32, ahead_c=3, ahead_d=3, tq_win=1024, ahead_win=1, tm_prep=512)


def _layer(x, p, layer_idx, tables, rope, w, tiles):
    bsz, s_len, _ = x.shape
    m = bsz * s_len
    bias_a, bias_c = tables
    cos, sin = rope
    tq, tk = tiles["tq_attn"], tiles["tk_attn"]
    x2 = x.reshape(m, D_MODEL)
    u2 = _in_proj(x2, w["g_pre"], w["w_in"], tiles["tm_in"], tiles["tn_in"])
    u3 = u2.reshape(bsz, s_len, U_WIDTH)
    qd, kd, vtc, vtd, vta, qn, kn = _attn_prep(u2, cos, sin, w["qnorm_d"], w["knorm_d"], bsz, s_len,
                                               tiles["tm_prep"], tk)
    ya = _win_attn(u3, vta, bias_a, w["sink"], tiles["tq_win"], tiles["ahead_win"])
    yb = _pool(u3, w["pool_w"], w["pool_scale"], tiles["tq_pool"])
    lambda_init = 0.8 - 0.6 * math.exp(-0.3 * layer_idx)
    span = math.log2(s_len)

    seg_max = lambda t: jnp.sqrt(jnp.max(t.reshape(bsz, -1, 8, C_HEADS, 2, HEAD_DIM)[:, :, 0, :, :, 0], axis=1))
    q_max, k_max = seg_max(qn), seg_max(kn) * BOUND_SLACK
    b_hi, b_lo = jnp.max(bias_c, axis=(1, 2, 3)), jnp.min(bias_c, axis=(1, 2, 3))
    stat_c = jnp.concatenate([k_max, jnp.broadcast_to(b_hi[None, :, None], (bsz, C_HEADS, 1)),
                              jnp.zeros((bsz, C_HEADS, 5), F32)], axis=2)
    stat_c = jnp.broadcast_to(stat_c[..., None], (bsz, C_HEADS, 8, LANES))
    fits_c = 2.0 * jnp.max(q_max * k_max) + jnp.max(b_hi - b_lo) + span <= MAX_EXP2_SPAN
    attend_c = lambda bounded: _diff_attn(u3, vtc, bias_c, stat_c, w["lam"], w["diff_subln"], lambda_init,
                                          tq, tk, tiles["nb_c"], tiles["ahead_c"], bounded)
    yc = lax.cond(fits_c, lambda: attend_c(True), lambda: attend_c(False))

    bound_d = (HEAD_DIM * SCALE * LOG2E * BOUND_SLACK
               * jnp.max(jnp.abs(w["qnorm_d"])) * jnp.max(jnp.abs(w["knorm_d"])))
    fits_d = 2.0 * bound_d + span <= MAX_EXP2_SPAN
    attend_d = lambda bounded: _grid_attn(qd.reshape(bsz, s_len, -1), kd.reshape(bsz, s_len, -1), vtd, u3,
                                          jnp.full((8, LANES), bound_d, F32), tq, tk, tiles["nb_d"],
                                          tiles["ahead_d"], bounded)
    yd = lax.cond(fits_d, lambda: attend_d(True), lambda: attend_d(False))
    flat = lambda y: y.reshape(m, 4 * LANES)
    out = _out_stage(x2, flat(ya), flat(yb), flat(yc), flat(yd), p.reshape(-1, m, PLE_DIM), layer_idx,
                     w["w_o"], w["g_post"], w["w_pe"], w["w_pg"], tiles["tm_out"])
    return out.reshape(bsz, s_len, D_MODEL)


def _trunk(x, p, rel_bias, layers, tiles=None):
    s_len = x.shape[1]
    tiles = tiles or _tiles(s_len)
    tables = _bias_tables(rel_bias, tiles["tq_attn"], tiles["tk_attn"])
    rope = _axial_rope(s_len)
    for i, w in enumerate(layers):
        x = _layer(x, p, i, tables, rope, w, tiles)
    return x


def _prep_layers(w_in, w_o, g_pre, g_post, sink_a, pool_w, pool_scale, lam_q1, lam_k1, lam_q2, lam_k2,
                 diff_subln, qnorm_d, knorm_d, w_pe, w_pg):
    layers = []
    for i in range(w_in.shape[0]):
        layers.append(dict(
            w_in=_prep_w_in(w_in[i]), w_o=w_o[i].astype(BF16), g_pre=g_pre[i], g_post=g_post[i],
            sink=jnp.repeat(sink_a[i].astype(F32) * LOG2E, A_BLOCK).reshape(2, 1, 4 * A_BLOCK),
            pool_w=pool_w[i], pool_scale=pool_scale[i],
            lam=jnp.stack([lam_q1[i], lam_k1[i], lam_q2[i], lam_k2[i]]).astype(F32),
            diff_subln=diff_subln[i], qnorm_d=qnorm_d[i], knorm_d=knorm_d[i],
            w_pe=w_pe[i].astype(BF16), w_pg=w_pg[i].astype(BF16)))
    return layers


def kernel(x_prompt, x_sample, p_prompt, p_sample, w_in, w_o, g_pre, g_post, sink_a, pool_w, pool_scale,
           lam_q1, lam_k1, lam_q2, lam_k2, diff_subln, qnorm_d, knorm_d, rel_bias, w_pe, w_pg):
    layers = _prep_layers(w_in, w_o, g_pre, g_post, sink_a, pool_w, pool_scale, lam_q1, lam_k1, lam_q2,
                          lam_k2, diff_subln, qnorm_d, knorm_d, w_pe, w_pg)
    y_prompt = _trunk(x_prompt, p_prompt, rel_bias, layers)
    y_sample = _trunk(x_sample, p_sample, rel_bias, layers)
    return (y_prompt, y_sample)
```
